```python
import math
import jax, jax.numpy as jnp
from jax import lax
import numpy as np

D_MODEL = 2048
BATCH = 4
SEQ = 8192
DEPTH = 4

CHUNK = 64
Q_BLOCK = 128
N_MIXERS = 4
BRANCH = D_MODEL
EPS = 1e-6
NEG_INF = -1e30
DIFF_V_DIM = 256
DIFF_HEADS = BRANCH // DIFF_V_DIM
DIFF_QK_DIM = DIFF_V_DIM // 2
BAND_HEAD_DIM = 64
BAND_HEADS = BRANCH // BAND_HEAD_DIM
BAND_PAST_CHUNKS = 8
BAND_PAST = BAND_PAST_CHUNKS * CHUNK
BAND_LEN = BAND_PAST + CHUNK
REL_CLIP = 256
FOX_HEAD_DIM = 128
FOX_HEADS = BRANCH // FOX_HEAD_DIM
POOL_WINDOWS = (2, 4, 8, 16)
POOL_GROUP = BRANCH // len(POOL_WINDOWS)

kernel_name = "hybrid_interleaved_streaming_encoder"


def rms_norm(x, g):
    xf = x.astype(jnp.float32)
    y = xf * lax.rsqrt(jnp.mean(xf * xf, axis=-1, keepdims=True) + EPS)
    return (y * g.astype(jnp.float32)).astype(x.dtype)


def proj_slice(h, w_in, i):
    return h @ w_in[:, i * BRANCH:(i + 1) * BRANCH]


def to_blocks(a, size):
    b, s = a.shape[0], a.shape[1]
    return jnp.moveaxis(a.reshape((b, s // size, size) + a.shape[2:]), 1, 0)


def from_blocks(o):
    o = jnp.moveaxis(o, 0, 1)
    return o.reshape((o.shape[0], o.shape[1] * o.shape[2]) + o.shape[3:])


def diff_attention(h, w_in, w_out, lq1, lk1, lq2, lk2, subln_g, lambda_init):
    b, s, _ = h.shape
    q = proj_slice(h, w_in, 0).reshape(b, s, DIFF_HEADS, 2, DIFF_QK_DIM)
    k = proj_slice(h, w_in, 1).reshape(b, s, DIFF_HEADS, 2, DIFF_QK_DIM)
    v = proj_slice(h, w_in, 2).reshape(b, s, DIFF_HEADS, DIFF_V_DIM)
    z = proj_slice(h, w_in, 3)
    f32 = jnp.float32
    lam = (jnp.exp(jnp.sum(lq1.astype(f32) * lk1.astype(f32)))
           - jnp.exp(jnp.sum(lq2.astype(f32) * lk2.astype(f32))) + lambda_init)
    scale = DIFF_QK_DIM ** -0.5
    key_chunk = jnp.arange(s) // CHUNK

    def block(args):
        qb, start = args
        sc = jnp.einsum('bqhmd,bkhmd->bhmqk', qb, k).astype(f32) * scale
        q_chunk = (start + jnp.arange(Q_BLOCK)) // CHUNK
        mask = key_chunk[None, :] <= q_chunk[:, None]
        p = jax.nn.softmax(jnp.where(mask[None, None, None], sc, NEG_INF), axis=-1)
        attn = p[:, :, 0] - lam * p[:, :, 1]
        return jnp.einsum('bhqk,bkhd->bqhd', attn.astype(v.dtype), v)

    starts = jnp.arange(s // Q_BLOCK) * Q_BLOCK
    o = from_blocks(lax.map(block, (to_blocks(q, Q_BLOCK), starts)))
    o = rms_norm(o, subln_g) * (1.0 - lambda_init)
    o = o.reshape(b, s, BRANCH) * jax.nn.silu(z)
    return o @ w_out


def band_attention(h, w_in, w_out, rel_table):
    b, s, _ = h.shape
    q = proj_slice(h, w_in, 0).reshape(b, s, BAND_HEADS, BAND_HEAD_DIM)
    k = proj_slice(h, w_in, 1).reshape(b, s, BAND_HEADS, BAND_HEAD_DIM)
    v = proj_slice(h, w_in, 2).reshape(b, s, BAND_HEADS, BAND_HEAD_DIM)
    z = proj_slice(h, w_in, 3)
    pad = ((0, 0), (BAND_PAST, 0), (0, 0), (0, 0))
    k_pad = jnp.pad(k, pad)
    v_pad = jnp.pad(v, pad)
    qi = jnp.arange(CHUNK)
    kj = jnp.arange(BAND_LEN)
    rel = qi[:, None] + BAND_PAST - kj[None, :]
    rel_idx = jnp.clip(rel, -REL_CLIP, REL_CLIP) + REL_CLIP
    bias = rel_table[:, rel_idx].astype(jnp.float32)
    scale = BAND_HEAD_DIM ** -0.5

    def chunk(args):
        qc, ci = args
        start = ci * CHUNK
        kb = lax.dynamic_slice_in_dim(k_pad, start, BAND_LEN, axis=1)
        vb = lax.dynamic_slice_in_dim(v_pad, start, BAND_LEN, axis=1)
        sc = jnp.einsum('bqhd,bkhd->bhqk', qc, kb).astype(jnp.float32) * scale + bias[None]
        valid = (start - BAND_PAST + kj) >= 0
        p = jax.nn.softmax(jnp.where(valid[None, None, None, :], sc, NEG_INF), axis=-1)
        return jnp.einsum('bhqk,bkhd->bqhd', p.astype(vb.dtype), vb)

    n_chunks = s // CHUNK
    o = from_blocks(lax.map(chunk, (to_blocks(q, CHUNK), jnp.arange(n_chunks))))
    o = o.reshape(b, s, BRANCH) * jax.nn.silu(z)
    return o @ w_out


def forgetting_attention(h, w_in, w_out, f_bias):
    b, s, _ = h.shape
    q = proj_slice(h, w_in, 0).reshape(b, s, FOX_HEADS, FOX_HEAD_DIM)
    k = proj_slice(h, w_in, 1).reshape(b, s, FOX_HEADS, FOX_HEAD_DIM)
    v = proj_slice(h, w_in, 2).reshape(b, s, FOX_HEADS, FOX_HEAD_DIM)
    z = proj_slice(h, w_in, 3)
    f_logit = h @ w_in[:, 4 * BRANCH:]
    log_f = jax.nn.log_sigmoid(f_logit.astype(jnp.float32) + f_bias.astype(jnp.float32))
    F = jnp.cumsum(log_f, axis=1)
    Fk = jnp.transpose(F, (0, 2, 1))
    key_pos = jnp.arange(s)
    scale = FOX_HEAD_DIM ** -0.5

    def block(args):
        qb, Fq, start = args
        sc = jnp.einsum('bqhd,bkhd->bhqk', qb, k).astype(jnp.float32) * scale
        sc = sc + jnp.transpose(Fq, (0, 2, 1))[..., None] - Fk[:, :, None, :]
        mask = key_pos[None, :] <= (start + jnp.arange(Q_BLOCK))[:, None]
        p = jax.nn.softmax(jnp.where(mask[None, None], sc, NEG_INF), axis=-1)
        return jnp.einsum('bhqk,bkhd->bqhd', p.astype(v.dtype), v)

    starts = jnp.arange(s // Q_BLOCK) * Q_BLOCK
    o = from_blocks(lax.map(block, (to_blocks(q, Q_BLOCK), to_blocks(F, Q_BLOCK), starts)))
    o = o.reshape(b, s, BRANCH) * jax.nn.silu(z)
    return o @ w_out


def multiscale_pool(h, w_in, w_out, pool_w, pool_scale):
    b, s, _ = h.shape
    u = proj_slice(h, w_in, 0)
    z = proj_slice(h, w_in, 1)
    uf = u.astype(jnp.float32)
    csum = jnp.cumsum(uf, axis=1)
    t1 = jnp.arange(s) + 1
    outs = []
    for g, win in enumerate(POOL_WINDOWS):
        sl = slice(g * POOL_GROUP, (g + 1) * POOL_GROUP)
        cg = csum[..., sl]
        shifted = jnp.pad(cg, ((0, 0), (win, 0), (0, 0)))[:, :s]
        mean = (cg - shifted) / jnp.minimum(t1, win).astype(jnp.float32)[None, :, None]
        outs.append(jnp.einsum('bsc,cd->bsd', (mean - uf[..., sl]).astype(u.dtype), pool_w[g]))
    y = jnp.concatenate(outs, axis=-1) * pool_scale
    return (y * jax.nn.silu(z)) @ w_out


def setup_inputs(seed: int = 0) -> dict:
    key = jax.random.key(seed)
    ks = jax.random.split(key, 24)
    f32 = jnp.float32

    def nrm(k, shape, scale):
        return jax.random.normal(k, shape, f32) * scale

    D, W = D_MODEL, BRANCH
    return {
        "x": nrm(ks[0], (BATCH, SEQ, D), 1.0),
        "c": nrm(ks[1], (BATCH, D), 1.0),
        "ada_w": nrm(ks[2], (DEPTH, D, 3 * D), 0.5 * D ** -0.5),
        "ada_b": nrm(ks[3], (DEPTH, 3 * D), 0.02),
        "norm_pre": 1.0 + nrm(ks[4], (DEPTH, D), 0.05),
        "norm_post": 1.0 + nrm(ks[5], (DEPTH, D), 0.05),
        "diff_w_in": nrm(ks[6], (D, 4 * W), D ** -0.5),
        "diff_w_out": nrm(ks[7], (W, D), W ** -0.5),
        "diff_lambda_q1": nrm(ks[8], (DIFF_QK_DIM,), 0.1),
        "diff_lambda_k1": nrm(ks[9], (DIFF_QK_DIM,), 0.1),
        "diff_lambda_q2": nrm(ks[10], (DIFF_QK_DIM,), 0.1),
        "diff_lambda_k2": nrm(ks[11], (DIFF_QK_DIM,), 0.1),
        "diff_subln": 1.0 + nrm(ks[12], (DIFF_V_DIM,), 0.05),
        "band_w_in": nrm(ks[13], (D, 4 * W), D ** -0.5),
        "band_w_out": nrm(ks[14], (W, D), W ** -0.5),
        "band_rel_bias": nrm(ks[15], (BAND_HEADS, 2 * REL_CLIP + 1), 0.5),
        "fox_w_in": nrm(ks[16], (D, 4 * W + FOX_HEADS), D ** -0.5),
        "fox_w_out": nrm(ks[17], (W, D), W ** -0.5),
        "fox_forget_bias": 2.0 + nrm(ks[18], (FOX_HEADS,), 0.5),
        "pool_w_in": nrm(ks[19], (D, 2 * W), D ** -0.5),
        "pool_w_out": nrm(ks[20], (W, D), W ** -0.5),
        "pool_group_w": nrm(ks[21], (len(POOL_WINDOWS), POOL_GROUP, POOL_GROUP), POOL_GROUP ** -0.5),
        "pool_scale": 1.0 + nrm(ks[22], (W,), 0.1),
    }


def reference(x, c, ada_w, ada_b, norm_pre, norm_post,
              diff_w_in, diff_w_out, diff_lambda_q1, diff_lambda_k1, diff_lambda_q2,
              diff_lambda_k2, diff_subln,
              band_w_in, band_w_out, band_rel_bias,
              fox_w_in, fox_w_out, fox_forget_bias,
              pool_w_in, pool_w_out, pool_group_w, pool_scale):
    c_act = jax.nn.silu(c)
    for i in range(DEPTH):
        mod = c_act @ ada_w[i] + ada_b[i]
        shift, scale, gate = jnp.split(mod, 3, axis=-1)
        h = rms_norm(x, norm_pre[i]) * (1.0 + scale[:, None, :]) + shift[:, None, :]
        kind = i % N_MIXERS
        if kind == 0:
            lambda_init = 0.8 - 0.6 * math.exp(-0.3 * i)
            y = diff_attention(h, diff_w_in, diff_w_out, diff_lambda_q1, diff_lambda_k1,
                               diff_lambda_q2, diff_lambda_k2, diff_subln, lambda_init)
        elif kind == 1:
            y = band_attention(h, band_w_in, band_w_out, band_rel_bias)
        elif kind == 2:
            y = forgetting_attention(h, fox_w_in, fox_w_out, fox_forget_bias)
        else:
            y = multiscale_pool(h, pool_w_in, pool_w_out, pool_group_w, pool_scale)
        x = x + gate[:, None, :] * rms_norm(y, norm_post[i])
    return x
```

```python
import functools
import math

import jax
import jax.numpy as jnp
from jax import lax
from jax.experimental import pallas as pl
from jax.experimental.pallas import tpu as pltpu

F32 = jnp.float32
BF16 = jnp.bfloat16

D_MODEL = 2048
EPS = 1e-6
NEG_INF = -1e30
CHUNK = 64
DIFF_HEADS, DIFF_V_DIM, DIFF_QK_DIM = 8, 256, 128
BAND_HEADS, BAND_HEAD_DIM, BAND_PAST, REL_CLIP = 32, 64, 512, 256
FOX_HEADS, FOX_HEAD_DIM = 16, 128
POOL_WINDOWS = (2, 4, 8, 16)
POOL_GROUP = D_MODEL // len(POOL_WINDOWS)
POOL_HALO = 16

LANES = 128
VMEM_LIMIT = 52 * 1024 * 1024

NT_DIMS = (((1,), (1,)), ((), ()))


def _silu(x):
    return x / (1.0 + jnp.exp(-x))


def _params(*sem):
    return pltpu.CompilerParams(dimension_semantics=sem, vmem_limit_bytes=VMEM_LIMIT)


def _ada_kernel(c_ref, w_ref, b_ref, o_ref):
    ca = _silu(c_ref[...]).astype(BF16)
    o_ref[0] = jnp.dot(ca, w_ref[0].astype(BF16), preferred_element_type=F32) + b_ref[0]


def _ada_mod(c, ada_w, ada_b):
    b, d = c.shape
    bp = -(-b // 8) * 8
    depth, _, n3 = ada_w.shape
    tn = 768
    cp = jnp.pad(c, ((0, bp - b), (0, 0)))
    return pl.pallas_call(
        _ada_kernel,
        name="ada_mod",
        out_shape=jax.ShapeDtypeStruct((depth, bp, n3), F32),
        grid=(depth, n3 // tn),
        in_specs=[pl.BlockSpec((bp, d), lambda l, j: (0, 0)),
                  pl.BlockSpec((1, d, tn), lambda l, j: (l, 0, j)),
                  pl.BlockSpec((1, 1, tn), lambda l, j: (l, 0, j))],
        out_specs=pl.BlockSpec((1, bp, tn), lambda l, j: (l, 0, j)),
        compiler_params=_params("parallel", "parallel"),
    )(cp, ada_w, ada_b.reshape(depth, 1, n3))


def _norm_proj_kernel(x_ref, g_ref, sh_ref, sc_ref, w_ref, *rest, has_f, sub):
    if has_f:
        wf_ref, o_ref, f_ref, h_ref = rest
    else:
        o_ref, h_ref = rest

    @pl.when(pl.program_id(1) == 0)
    def _():
        g = g_ref[...]
        sc = 1.0 + sc_ref[0]
        sh = sh_ref[0]
        for r in range(x_ref.shape[0] // sub):
            x = x_ref[r * sub:(r + 1) * sub, :]
            y = x * lax.rsqrt(jnp.mean(x * x, axis=-1, keepdims=True) + EPS) * g
            h_ref[r * sub:(r + 1) * sub, :] = (y * sc + sh).astype(BF16)
        if has_f:
            f_ref[...] = jnp.dot(h_ref[...], wf_ref[...], preferred_element_type=F32)

    o_ref[...] = jnp.dot(h_ref[...], w_ref[...], preferred_element_type=F32).astype(o_ref.dtype)


def _norm_proj(x2, seq, g, mod3, w, wf=None, out_dtype=BF16, tm=512, tn=1024):
    t, d = x2.shape
    n = w.shape[1]
    tm = min(tm, seq)
    nb = seq // tm
    in_specs = [pl.BlockSpec((tm, d), lambda i, j: (i, 0)),
                pl.BlockSpec((1, d), lambda i, j: (0, 0)),
                pl.BlockSpec((1, 1, d), lambda i, j: (i // nb, 0, 0)),
                pl.BlockSpec((1, 1, d), lambda i, j: (i // nb, 0, 1)),
                pl.BlockSpec((d, tn), lambda i, j: (0, j))]
    out_shape = [jax.ShapeDtypeStruct((t, n), out_dtype)]
    out_specs = [pl.BlockSpec((tm, tn), lambda i, j: (i, j))]
    args = [x2, g.reshape(1, d), mod3, mod3, w]
    if wf is not None:
        in_specs.append(pl.BlockSpec((d, LANES), lambda i, j: (0, 0)))
        out_shape.append(jax.ShapeDtypeStruct((t, LANES), F32))
        out_specs.append(pl.BlockSpec((tm, LANES), lambda i, j: (i, 0)))
        args.append(wf)
    res = pl.pallas_call(
        functools.partial(_norm_proj_kernel, has_f=wf is not None, sub=min(256, tm)),
        name="norm_proj",
        out_shape=out_shape,
        grid=(t // tm, n // tn),
        in_specs=in_specs,
        out_specs=out_specs,
        scratch_shapes=[pltpu.VMEM((tm, d), BF16)],
        compiler_params=_params("parallel", "arbitrary"),
    )(*args)
    return res if wf is not None else res[0]


def _out_proj_kernel(o_ref, w_ref, x_ref, gate_ref, g_ref, out_ref):
    y = jnp.dot(o_ref[...], w_ref[...], preferred_element_type=F32)
    yn = y * lax.rsqrt(jnp.mean(y * y, axis=-1, keepdims=True) + EPS) * g_ref[...]
    out_ref[...] = x_ref[...] + gate_ref[0] * yn


def _out_proj_residual(o2, w, x2, seq, mod3, g, tm=256):
    t, d = x2.shape
    tm = min(tm, seq)
    nb = seq // tm
    return pl.pallas_call(
        _out_proj_kernel,
        name="out_proj",
        out_shape=jax.ShapeDtypeStruct((t, d), F32),
        grid=(t // tm,),
        in_specs=[pl.BlockSpec((tm, d), lambda i: (i, 0)),
                  pl.BlockSpec((d, d), lambda i: (0, 0)),
                  pl.BlockSpec((tm, d), lambda i: (i, 0)),
                  pl.BlockSpec((1, 1, d), lambda i: (i // nb, 0, 2)),
                  pl.BlockSpec((1, d), lambda i: (0, 0))],
        out_specs=pl.BlockSpec((tm, d), lambda i: (i, 0)),
        compiler_params=_params("parallel"),
    )(o2, w, x2, mod3, g.reshape(1, d))


def _online_softmax_step(s, v, m_ref, l_ref, acc_ref, idx):
    m_prev = m_ref[idx]
    m_new = jnp.maximum(m_prev, jnp.max(s, axis=-1, keepdims=True))
    alpha = jnp.exp(m_prev - m_new)
    p = jnp.exp(s - m_new)
    l_ref[idx] = alpha * l_ref[idx] + jnp.sum(p, axis=-1, keepdims=True)
    acc_ref[idx] = alpha * acc_ref[idx] + jnp.dot(p.astype(BF16), v, preferred_element_type=F32)
    m_ref[idx] = m_new


def _diff_attn_kernel(q_ref, k_ref, v_ref, z_ref, lq1_ref, lk1_ref, lq2_ref, lk2_ref, g_ref,
                      o_ref, m_ref, l_ref, acc_ref, *, tq, lambda_init):
    qi = pl.program_id(2)
    scale = DIFF_QK_DIM ** -0.5
    m_ref[...] = jnp.full(m_ref.shape, NEG_INF, F32)
    l_ref[...] = jnp.zeros(l_ref.shape, F32)
    acc_ref[...] = jnp.zeros(acc_ref.shape, F32)
    q = q_ref[...]

    def step(ki, masked):
        off = pl.multiple_of(ki * tq, tq)
        k = k_ref[pl.ds(off, tq), :]
        v = v_ref[pl.ds(off, tq), :]
        for m in range(2):
            sl = slice(m * DIFF_QK_DIM, (m + 1) * DIFF_QK_DIM)
            s = lax.dot_general(q[:, sl], k[:, sl], NT_DIMS, preferred_element_type=F32) * scale
            if masked:
                rc = lax.broadcasted_iota(jnp.int32, s.shape, 0) // CHUNK
                kc = lax.broadcasted_iota(jnp.int32, s.shape, 1) // CHUNK
                s = jnp.where(kc <= rc, s, NEG_INF)
            _online_softmax_step(s, v, m_ref, l_ref, acc_ref, m)

    def body(ki, carry):
        step(ki, False)
        return carry

    lax.fori_loop(0, qi, body, 0)
    step(qi, True)

    lam = (jnp.exp(jnp.sum(lq1_ref[...] * lk1_ref[...], axis=-1, keepdims=True))
           - jnp.exp(jnp.sum(lq2_ref[...] * lk2_ref[...], axis=-1, keepdims=True)) + lambda_init)
    o = acc_ref[0] / l_ref[0] - lam * (acc_ref[1] / l_ref[1])
    on = o * lax.rsqrt(jnp.mean(o * o, axis=-1, keepdims=True) + EPS) * g_ref[...]
    on = on * (1.0 - lambda_init)
    o_ref[...] = (on * _silu(z_ref[...].astype(F32))).astype(o_ref.dtype)


def _diff_attention(qkvz, batch, seq, lq1, lk1, lq2, lk2, subln_g, lambda_init, tq=512):
    t = qkvz.shape[0]
    tq = min(tq, seq)
    nq = seq // tq
    h, dv = DIFF_HEADS, DIFF_V_DIM
    vec = lambda a: a.reshape(1, -1).astype(F32)
    small = lambda n: pl.BlockSpec((1, n), lambda b, hh, i: (0, 0))
    return pl.pallas_call(
        functools.partial(_diff_attn_kernel, tq=tq, lambda_init=lambda_init),
        name="diff_attn",
        out_shape=jax.ShapeDtypeStruct((t, h * dv), BF16),
        grid=(batch, h, nq),
        in_specs=[pl.BlockSpec((tq, dv), lambda b, hh, i: (b * nq + i, hh)),
                  pl.BlockSpec((seq, dv), lambda b, hh, i: (b, h + hh)),
                  pl.BlockSpec((seq, dv), lambda b, hh, i: (b, 2 * h + hh)),
                  pl.BlockSpec((tq, dv), lambda b, hh, i: (b * nq + i, 3 * h + hh)),
                  small(DIFF_QK_DIM), small(DIFF_QK_DIM), small(DIFF_QK_DIM), small(DIFF_QK_DIM),
                  small(dv)],
        out_specs=pl.BlockSpec((tq, dv), lambda b, hh, i: (b * nq + i, hh)),
        scratch_shapes=[pltpu.VMEM((2, tq, 1), F32), pltpu.VMEM((2, tq, 1), F32),
                        pltpu.VMEM((2, tq, dv), F32)],
        compiler_params=_params("parallel", "parallel", "arbitrary"),
    )(qkvz, qkvz, qkvz, qkvz, vec(lq1), vec(lk1), vec(lq2), vec(lk2), vec(subln_g))


def _fox_gate_kernel(f_ref, b_ref, fc_ref, fr_ref):
    x = f_ref[...] + b_ref[...]
    lf = jnp.minimum(x, 0.0) - jnp.log1p(jnp.exp(-jnp.abs(x)))
    n = lf.shape[0]
    row = lax.broadcasted_iota(jnp.int32, lf.shape, 0)
    d = 1
    while d < n:
        lf = lf + jnp.where(row >= d, pltpu.roll(lf, d, 0), 0.0)
        d *= 2
    fc_ref[...] = lf
    fr_ref[0] = lf.T[:FOX_HEADS, :]


def _fox_gates(f_logit, bias, batch, seq):
    bpad = jnp.zeros((1, LANES), F32).at[0, :FOX_HEADS].set(bias.astype(F32))
    return pl.pallas_call(
        _fox_gate_kernel,
        name="fox_gates",
        out_shape=[jax.ShapeDtypeStruct((batch * seq, LANES), F32),
                   jax.ShapeDtypeStruct((batch, FOX_HEADS, seq), F32)],
        grid=(batch,),
        in_specs=[pl.BlockSpec((seq, LANES), lambda b: (b, 0)),
                  pl.BlockSpec((1, LANES), lambda b: (0, 0))],
        out_specs=[pl.BlockSpec((seq, LANES), lambda b: (b, 0)),
                   pl.BlockSpec((1, FOX_HEADS, seq), lambda b: (b, 0, 0))],
        compiler_params=_params("parallel"),
    )(f_logit, bpad)


def _fox_attn_kernel(q_ref, k_ref, v_ref, z_ref, fc_ref, fr_ref, o_ref, m_ref, l_ref, acc_ref, *, tq):
    hh = pl.program_id(1)
    qi = pl.program_id(2)
    scale = FOX_HEAD_DIM ** -0.5
    m_ref[...] = jnp.full(m_ref.shape, NEG_INF, F32)
    l_ref[...] = jnp.zeros(l_ref.shape, F32)
    acc_ref[...] = jnp.zeros(acc_ref.shape, F32)
    q = q_ref[...]
    fc = fc_ref[...]
    lane = lax.broadcasted_iota(jnp.int32, fc.shape, 1)
    fq = jnp.sum(jnp.where(lane == hh, fc, 0.0), axis=-1, keepdims=True)

    def step(ki, masked):
        off = pl.multiple_of(ki * tq, tq)
        k = k_ref[pl.ds(off, tq), :]
        v = v_ref[pl.ds(off, tq), :]
        fk = fr_ref[0, pl.ds(hh, 1), pl.ds(off, tq)]
        s = lax.dot_general(q, k, NT_DIMS, preferred_element_type=F32) * scale
        s = s + fq - fk
        if masked:
            r = lax.broadcasted_iota(jnp.int32, s.shape, 0)
            c = lax.broadcasted_iota(jnp.int32, s.shape, 1)
            s = jnp.where(c <= r, s, NEG_INF)
        _online_softmax_step(s, v, m_ref, l_ref, acc_ref, 0)

    def body(ki, carry):
        step(ki, False)
        return carry

    lax.fori_loop(0, qi, body, 0)
    step(qi, True)
    o = acc_ref[0] / l_ref[0]
    o_ref[...] = (o * _silu(z_ref[...].astype(F32))).astype(o_ref.dtype)


def _fox_attention(qkvz, f_cols, f_rows, batch, seq, tq=512):
    t = qkvz.shape[0]
    tq = min(tq, seq)
    nq = seq // tq
    h, dh = FOX_HEADS, FOX_HEAD_DIM
    return pl.pallas_call(
        functools.partial(_fox_attn_kernel, tq=tq),
        name="fox_attn",
        out_shape=jax.ShapeDtypeStruct((t, h * dh), BF16),
        grid=(batch, h, nq),
        in_specs=[pl.BlockSpec((tq, dh), lambda b, hh, i: (b * nq + i, hh)),
                  pl.BlockSpec((seq, dh), lambda b, hh, i: (b, h + hh)),
                  pl.BlockSpec((seq, dh), lambda b, hh, i: (b, 2 * h + hh)),
                  pl.BlockSpec((tq, dh), lambda b, hh, i: (b * nq + i, 3 * h + hh)),
                  pl.BlockSpec((tq, LANES), lambda b, hh, i: (b * nq + i, 0)),
                  pl.BlockSpec((1, h, seq), lambda b, hh, i: (b, 0, 0))],
        out_specs=pl.BlockSpec((tq, dh), lambda b, hh, i: (b * nq + i, hh)),
        scratch_shapes=[pltpu.VMEM((1, tq, 1), F32), pltpu.VMEM((1, tq, 1), F32),
                        pltpu.VMEM((1, tq, dh), F32)],
        compiler_params=_params("parallel", "parallel", "arbitrary"),
    )(qkvz, qkvz, qkvz, qkvz, f_cols, f_rows)


BAND_TQ = 256
BAND_WIN = BAND_PAST + BAND_TQ
BAND_HPB = LANES // BAND_HEAD_DIM


def _band_attn_kernel(q_ref, k_ref, v_ref, z_ref, bias_ref, o_ref, kp_ref, vp_ref):
    qb = pl.program_id(2)
    scale = BAND_HEAD_DIM ** -0.5

    @pl.when(qb == 0)
    def _():
        zeros = jnp.zeros((BAND_PAST, LANES), BF16)
        kp_ref[:BAND_PAST, :] = zeros
        vp_ref[:BAND_PAST, :] = zeros
        kp_ref[BAND_PAST:, :] = k_ref[...]
        vp_ref[BAND_PAST:, :] = v_ref[...]

    start = pl.multiple_of(qb * BAND_TQ, BAND_TQ)
    kw = kp_ref[pl.ds(start, BAND_WIN), :]
    vw = vp_ref[pl.ds(start, BAND_WIN), :]
    kpos = start - BAND_PAST + lax.broadcasted_iota(jnp.int32, (1, BAND_WIN), 1)
    valid = kpos >= 0
    q = q_ref[...]
    z = z_ref[...].astype(F32)
    outs = []
    for hh in range(BAND_HPB):
        sl = slice(hh * BAND_HEAD_DIM, (hh + 1) * BAND_HEAD_DIM)
        s = lax.dot_general(q[:, sl], kw[:, sl], NT_DIMS, preferred_element_type=F32) * scale
        s = jnp.where(valid, s + bias_ref[hh], NEG_INF)
        p = jnp.exp(s - jnp.max(s, axis=-1, keepdims=True))
        l = jnp.sum(p, axis=-1, keepdims=True)
        o = jnp.dot(p.astype(BF16), vw[:, sl], preferred_element_type=F32) / l
        outs.append(o * _silu(z[:, sl]))
    o_ref[...] = jnp.concatenate(outs, axis=-1).astype(o_ref.dtype)


def _band_bias(rel_table):
    qi = jnp.arange(BAND_TQ)[:, None]
    kj = jnp.arange(BAND_WIN)[None, :]
    rel = qi + BAND_PAST - kj
    idx = jnp.clip(rel, -REL_CLIP, REL_CLIP) + REL_CLIP
    dchunk = kj // CHUNK - qi // CHUNK
    band = (dchunk >= 0) & (dchunk <= BAND_PAST // CHUNK)
    return jnp.where(band[None], rel_table[:, idx].astype(F32), NEG_INF)


def _band_attention(qkvz, bias, batch, seq):
    t = qkvz.shape[0]
    nq = seq // BAND_TQ
    hp = BAND_HEADS // BAND_HPB
    return pl.pallas_call(
        _band_attn_kernel,
        name="band_attn",
        out_shape=jax.ShapeDtypeStruct((t, BAND_HEADS * BAND_HEAD_DIM), BF16),
        grid=(batch, hp, nq),
        in_specs=[pl.BlockSpec((BAND_TQ, LANES), lambda b, g, i: (b * nq + i, g)),
                  pl.BlockSpec((seq, LANES), lambda b, g, i: (b, hp + g)),
                  pl.BlockSpec((seq, LANES), lambda b, g, i: (b, 2 * hp + g)),
                  pl.BlockSpec((BAND_TQ, LANES), lambda b, g, i: (b * nq + i, 3 * hp + g)),
                  pl.BlockSpec((BAND_HPB, BAND_TQ, BAND_WIN), lambda b, g, i: (g, 0, 0))],
        out_specs=pl.BlockSpec((BAND_TQ, LANES), lambda b, g, i: (b * nq + i, g)),
        scratch_shapes=[pltpu.VMEM((seq + BAND_PAST, LANES), BF16),
                        pltpu.VMEM((seq + BAND_PAST, LANES), BF16)],
        compiler_params=_params("parallel", "parallel", "arbitrary"),
    )(qkvz, qkvz, qkvz, qkvz, bias)


def _pool_kernel(u_ref, up_ref, z_ref, w_ref, ps_ref, o_ref, *, tm, nb):
    i = pl.program_id(0)
    t0 = (i % nb) * tm
    has_history = t0 > 0
    t = t0 + lax.broadcasted_iota(jnp.int32, (tm, 1), 0)
    for g, win in enumerate(POOL_WINDOWS):
        sl = slice(g * POOL_GROUP, (g + 1) * POOL_GROUP)
        u = u_ref[:, sl]
        acc = jnp.concatenate([jnp.where(has_history, up_ref[:, sl], 0.0), u], axis=0)
        d = 1
        while d < win:
            acc = acc + pltpu.roll(acc, d, 0)
            d *= 2
        wsum = acc[POOL_HALO:, :]
        cnt = jnp.minimum(t + 1, win).astype(F32)
        delta = (wsum / cnt - u).astype(BF16)
        y = jnp.dot(delta, w_ref[g], preferred_element_type=F32) * ps_ref[:, sl]
        o_ref[:, sl] = (y * _silu(z_ref[:, sl])).astype(o_ref.dtype)


def _pool_mix(uz, seq, pool_w, pool_scale, tm=256):
    t = uz.shape[0]
    d = D_MODEL
    tm = min(tm, seq)
    nb = seq // tm
    hb = tm // POOL_HALO
    return pl.pallas_call(
        functools.partial(_pool_kernel, tm=tm, nb=nb),
        name="pool_mix",
        out_shape=jax.ShapeDtypeStruct((t, d), BF16),
        grid=(t // tm,),
        in_specs=[pl.BlockSpec((tm, d), lambda i: (i, 0)),
                  pl.BlockSpec((POOL_HALO, d), lambda i: (jnp.maximum(i * hb - 1, 0), 0)),
                  pl.BlockSpec((tm, d), lambda i: (i, 1)),
                  pl.BlockSpec((len(POOL_WINDOWS), POOL_GROUP, POOL_GROUP), lambda i: (0, 0, 0)),
                  pl.BlockSpec((1, d), lambda i: (0, 0))],
        out_specs=pl.BlockSpec((tm, d), lambda i: (i, 0)),
        compiler_params=_params("parallel"),
    )(uz, uz, uz, pool_w, pool_scale.reshape(1, d).astype(F32))


def kernel(x, c, ada_w, ada_b, norm_pre, norm_post, diff_w_in, diff_w_out, diff_lambda_q1, diff_lambda_k1, diff_lambda_q2, diff_lambda_k2, diff_subln, band_w_in, band_w_out, band_rel_bias, fox_w_in, fox_w_out, fox_forget_bias, pool_w_in, pool_w_out, pool_group_w, pool_scale):
    batch, seq, d = x.shape
    depth = ada_w.shape[0]
    branch = d
    mod = _ada_mod(c, ada_w, ada_b)
    x2 = x.reshape(batch * seq, d)
    for i in range(depth):
        mod3 = mod[i].reshape(mod.shape[1], 1, 3 * d)
        kind = i % 4
        if kind == 0:
            lambda_init = 0.8 - 0.6 * math.exp(-0.3 * i)
            qkvz = _norm_proj(x2, seq, norm_pre[i], mod3, diff_w_in.astype(BF16))
            o = _diff_attention(qkvz, batch, seq, diff_lambda_q1, diff_lambda_k1, diff_lambda_q2,
                                diff_lambda_k2, diff_subln, lambda_init)
            w_out = diff_w_out
        elif kind == 1:
            qkvz = _norm_proj(x2, seq, norm_pre[i], mod3, band_w_in.astype(BF16))
            o = _band_attention(qkvz, _band_bias(band_rel_bias), batch, seq)
            w_out = band_w_out
        elif kind == 2:
            w = fox_w_in[:, :4 * branch].astype(BF16)
            wf = jnp.pad(fox_w_in[:, 4 * branch:], ((0, 0), (0, LANES - FOX_HEADS))).astype(BF16)
            qkvz, f_logit = _norm_proj(x2, seq, norm_pre[i], mod3, w, wf=wf)
            f_cols, f_rows = _fox_gates(f_logit, fox_forget_bias, batch, seq)
            o = _fox_attention(qkvz, f_cols, f_rows, batch, seq)
            w_out = fox_w_out
        else:
            uz = _norm_proj(x2, seq, norm_pre[i], mod3, pool_w_in.astype(BF16), out_dtype=F32)
            o = _pool_mix(uz, seq, pool_group_w.astype(BF16), pool_scale)
            w_out = pool_w_out
        x2 = _out_proj_residual(o, w_out.astype(BF16), x2, seq, mod3, norm_post[i])
    return x2.reshape(batch, seq, d)
```

```python
import functools
import math

import jax
import jax.numpy as jnp
from jax import lax
from jax.experimental import pallas as pl
from jax.experimental.pallas import tpu as pltpu

F32 = jnp.float32
BF16 = jnp.bfloat16

D_MODEL = 2048
EPS = 1e-6
NEG_INF = -1e30
LOG2E = math.log2(math.e)
CHUNK = 64
DIFF_HEADS, DIFF_V_DIM, DIFF_QK_DIM = 8, 256, 128
BAND_HEADS, BAND_HEAD_DIM, BAND_PAST, REL_CLIP = 32, 64, 512, 256
FOX_HEADS, FOX_HEAD_DIM = 16, 128
POOL_WINDOWS = (2, 4, 8, 16)
POOL_GROUP = D_MODEL // len(POOL_WINDOWS)
POOL_HALO = 16

LANES = 128
VMEM_LIMIT = 52 * 1024 * 1024

NT_DIMS = (((1,), (1,)), ((), ()))


def _silu(x):
    return x / (1.0 + jnp.exp(-x))


def _params(*sem):
    return pltpu.CompilerParams(dimension_semantics=sem, vmem_limit_bytes=VMEM_LIMIT)


def _ada_kernel(c_ref, w_ref, b_ref, o_ref):
    ca = _silu(c_ref[...]).astype(BF16)
    o_ref[0] = jnp.dot(ca, w_ref[0].astype(BF16), preferred_element_type=F32) + b_ref[0]


def _ada_mod(c, ada_w, ada_b):
    b, d = c.shape
    bp = -(-b // 8) * 8
    depth, _, n3 = ada_w.shape
    tn = 768
    cp = jnp.pad(c, ((0, bp - b), (0, 0)))
    return pl.pallas_call(
        _ada_kernel,
        name="ada_mod",
        out_shape=jax.ShapeDtypeStruct((depth, bp, n3), F32),
        grid=(depth, n3 // tn),
        in_specs=[pl.BlockSpec((bp, d), lambda l, j: (0, 0)),
                  pl.BlockSpec((1, d, tn), lambda l, j: (l, 0, j)),
                  pl.BlockSpec((1, 1, tn), lambda l, j: (l, 0, j))],
        out_specs=pl.BlockSpec((1, bp, tn), lambda l, j: (l, 0, j)),
        compiler_params=_params("parallel", "parallel"),
    )(cp, ada_w, ada_b.reshape(depth, 1, n3))


def _norm_proj_kernel(x_ref, g_ref, sh_ref, sc_ref, w_ref, cs_ref, *rest, has_f, emit_h, sub):
    rest = list(rest)
    wf_ref = rest.pop(0) if has_f else None
    o_ref = rest.pop(0)
    hout_ref = rest.pop(0) if emit_h else None
    f_ref = rest.pop(0) if has_f else None
    h_ref = rest.pop(0)

    @pl.when(pl.program_id(1) == 0)
    def _():
        g = g_ref[...]
        sc = 1.0 + sc_ref[0]
        sh = sh_ref[0]
        for r in range(x_ref.shape[0] // sub):
            x = x_ref[r * sub:(r + 1) * sub, :]
            y = x * lax.rsqrt(jnp.mean(x * x, axis=-1, keepdims=True) + EPS) * g
            h_ref[r * sub:(r + 1) * sub, :] = (y * sc + sh).astype(BF16)
        if emit_h:
            hout_ref[...] = h_ref[...]
        if has_f:
            f_ref[...] = jnp.dot(h_ref[...], wf_ref[...], preferred_element_type=F32)

    y = jnp.dot(h_ref[...], w_ref[...], preferred_element_type=F32)
    o_ref[...] = (y * cs_ref[...]).astype(o_ref.dtype)


def _norm_proj(x2, seq, g, mod3, w, col_blocks, col_scale, wf=None, emit_h=False, out_dtype=BF16,
               tm=512, tn=1024):
    t, d = x2.shape
    tm = min(tm, seq)
    nb = seq // tm
    nj = len(col_blocks)
    n = nj * tn
    first, skip_from, skip = col_blocks[0], None, 0
    for a, b2 in zip(col_blocks, col_blocks[1:]):
        if b2 != a + 1:
            skip_from, skip = a - first + 1, b2 - a - 1
    if skip_from is None:
        wmap = lambda i, j: (0, j + first)
    else:
        wmap = lambda i, j: (0, j + first + jnp.where(j >= skip_from, skip, 0))
    in_specs = [pl.BlockSpec((tm, d), lambda i, j: (i, 0)),
                pl.BlockSpec((1, d), lambda i, j: (0, 0)),
                pl.BlockSpec((1, 1, d), lambda i, j: (i // nb, 0, 0)),
                pl.BlockSpec((1, 1, d), lambda i, j: (i // nb, 0, 1)),
                pl.BlockSpec((d, tn), wmap),
                pl.BlockSpec((1, tn), lambda i, j: (0, j))]
    out_shape = [jax.ShapeDtypeStruct((t, n), out_dtype)]
    out_specs = [pl.BlockSpec((tm, tn), lambda i, j: (i, j))]
    args = [x2, g.reshape(1, d), mod3, mod3, w, col_scale.reshape(1, n).astype(F32)]
    if wf is not None:
        in_specs.append(pl.BlockSpec((d, LANES), lambda i, j: (0, 0)))
        args.append(wf)
    if emit_h:
        out_shape.append(jax.ShapeDtypeStruct((t, d), BF16))
        out_specs.append(pl.BlockSpec((tm, d), lambda i, j: (i, 0)))
    if wf is not None:
        out_shape.append(jax.ShapeDtypeStruct((t, LANES), F32))
        out_specs.append(pl.BlockSpec((tm, LANES), lambda i, j: (i, 0)))
    return pl.pallas_call(
        functools.partial(_norm_proj_kernel, has_f=wf is not None, emit_h=emit_h, sub=min(256, tm)),
        name="norm_proj",
        out_shape=out_shape,
        grid=(t // tm, nj),
        in_specs=in_specs,
        out_specs=out_specs,
        scratch_shapes=[pltpu.VMEM((tm, d), BF16)],
        compiler_params=_params("parallel", "arbitrary"),
    )(*args)


def _proj_t_kernel(w_ref, h_ref, o_ref):
    o_ref[...] = lax.dot_general(w_ref[...], h_ref[...], NT_DIMS,
                                 preferred_element_type=F32).astype(o_ref.dtype)


def _proj_t(w_t, h, tn=512, tm=1024):
    n, d = w_t.shape
    t = h.shape[0]
    tm = min(tm, t)
    return pl.pallas_call(
        _proj_t_kernel,
        name="proj_t",
        out_shape=jax.ShapeDtypeStruct((n, t), BF16),
        grid=(t // tm, n // tn),
        in_specs=[pl.BlockSpec((tn, d), lambda i, j: (j, 0)),
                  pl.BlockSpec((tm, d), lambda i, j: (i, 0))],
        out_specs=pl.BlockSpec((tn, tm), lambda i, j: (j, i)),
        compiler_params=_params("parallel", "arbitrary"),
    )(w_t, h)


def _out_proj_kernel(o_ref, w_ref, x_ref, gate_ref, g_ref, out_ref):
    y = jnp.dot(o_ref[...], w_ref[...], preferred_element_type=F32)
    yn = y * lax.rsqrt(jnp.mean(y * y, axis=-1, keepdims=True) + EPS) * g_ref[...]
    out_ref[...] = x_ref[...] + gate_ref[0] * yn


def _out_proj_residual(o2, w, x2, seq, mod3, g, tm=256):
    t, d = x2.shape
    tm = min(tm, seq)
    nb = seq // tm
    return pl.pallas_call(
        _out_proj_kernel,
        name="out_proj",
        out_shape=jax.ShapeDtypeStruct((t, d), F32),
        grid=(t // tm,),
        in_specs=[pl.BlockSpec((tm, d), lambda i: (i, 0)),
                  pl.BlockSpec((d, d), lambda i: (0, 0)),
                  pl.BlockSpec((tm, d), lambda i: (i, 0)),
                  pl.BlockSpec((1, 1, d), lambda i: (i // nb, 0, 2)),
                  pl.BlockSpec((1, d), lambda i: (0, 0))],
        out_specs=pl.BlockSpec((tm, d), lambda i: (i, 0)),
        compiler_params=_params("parallel"),
    )(o2, w, x2, mod3, g.reshape(1, d))


def _flash_step_t(s_t, v_t, m_ref, l_ref, acc_ref, idx):
    m_prev = m_ref[idx]
    m_new = jnp.maximum(m_prev, jnp.max(s_t, axis=0, keepdims=True))
    alpha = jnp.exp2(m_prev - m_new)
    p_t = jnp.exp2(s_t - m_new)
    l_ref[idx] = alpha * l_ref[idx] + jnp.sum(p_t, axis=0, keepdims=True)
    acc_ref[idx] = alpha * acc_ref[idx] + jnp.dot(v_t, p_t.astype(BF16), preferred_element_type=F32)
    m_ref[idx] = m_new


def _flash_init(m_ref, l_ref, acc_ref):
    m_ref[...] = jnp.full(m_ref.shape, NEG_INF, F32)
    l_ref[...] = jnp.zeros(l_ref.shape, F32)
    acc_ref[...] = jnp.zeros(acc_ref.shape, F32)


def _diff_attn_kernel(q_ref, k_ref, vt_ref, z_ref, lq1_ref, lk1_ref, lq2_ref, lk2_ref, g_ref,
                      o_ref, m_ref, l_ref, acc_ref, *, tq, lambda_init):
    qi = pl.program_id(2)
    _flash_init(m_ref, l_ref, acc_ref)
    q = q_ref[...]

    def step(ki, masked):
        off = pl.multiple_of(ki * tq, tq)
        k = k_ref[pl.ds(off, tq), :]
        v_t = vt_ref[:, pl.ds(off, tq)]
        for m in range(2):
            sl = slice(m * DIFF_QK_DIM, (m + 1) * DIFF_QK_DIM)
            s_t = lax.dot_general(k[:, sl], q[:, sl], NT_DIMS, preferred_element_type=F32)
            if masked:
                kc = lax.broadcasted_iota(jnp.int32, s_t.shape, 0) // CHUNK
                qc = lax.broadcasted_iota(jnp.int32, s_t.shape, 1) // CHUNK
                s_t = jnp.where(kc <= qc, s_t, NEG_INF)
            _flash_step_t(s_t, v_t, m_ref, l_ref, acc_ref, m)

    def body(ki, carry):
        step(ki, False)
        return carry

    lax.fori_loop(0, qi, body, 0)
    step(qi, True)

    lam = (jnp.exp(jnp.sum(lq1_ref[...] * lk1_ref[...], axis=-1, keepdims=True))
           - jnp.exp(jnp.sum(lq2_ref[...] * lk2_ref[...], axis=-1, keepdims=True)) + lambda_init)
    o_t = acc_ref[0] / l_ref[0] - lam * (acc_ref[1] / l_ref[1])
    o = o_t.T
    on = o * lax.rsqrt(jnp.mean(o * o, axis=-1, keepdims=True) + EPS) * g_ref[...]
    on = on * (1.0 - lambda_init)
    o_ref[...] = (on * _silu(z_ref[...].astype(F32))).astype(o_ref.dtype)


def _diff_attention(qkz, v_t, batch, seq, lq1, lk1, lq2, lk2, subln_g, lambda_init, tq=512):
    t = qkz.shape[0]
    tq = min(tq, seq)
    nq = seq // tq
    h, dv = DIFF_HEADS, DIFF_V_DIM
    vec = lambda a: a.reshape(1, -1).astype(F32)
    small = lambda n: pl.BlockSpec((1, n), lambda b, hh, i: (0, 0))
    return pl.pallas_call(
        functools.partial(_diff_attn_kernel, tq=tq, lambda_init=lambda_init),
        name="diff_attn",
        out_shape=jax.ShapeDtypeStruct((t, h * dv), BF16),
        grid=(batch, h, nq),
        in_specs=[pl.BlockSpec((tq, dv), lambda b, hh, i: (b * nq + i, hh)),
                  pl.BlockSpec((seq, dv), lambda b, hh, i: (b, h + hh)),
                  pl.BlockSpec((dv, seq), lambda b, hh, i: (hh, b)),
                  pl.BlockSpec((tq, dv), lambda b, hh, i: (b * nq + i, 2 * h + hh)),
                  small(DIFF_QK_DIM), small(DIFF_QK_DIM), small(DIFF_QK_DIM), small(DIFF_QK_DIM),
                  small(dv)],
        out_specs=pl.BlockSpec((tq, dv), lambda b, hh, i: (b * nq + i, hh)),
        scratch_shapes=[pltpu.VMEM((2, 1, tq), F32), pltpu.VMEM((2, 1, tq), F32),
                        pltpu.VMEM((2, dv, tq), F32)],
        compiler_params=_params("parallel", "parallel", "arbitrary"),
    )(qkz, qkz, v_t, qkz, vec(lq1), vec(lk1), vec(lq2), vec(lk2), vec(subln_g))


def _fox_gate_kernel(f_ref, b_ref, fc_ref, fr_ref):
    x = f_ref[...] + b_ref[...]
    lf = jnp.minimum(x, 0.0) - jnp.log1p(jnp.exp(-jnp.abs(x)))
    n = lf.shape[0]
    row = lax.broadcasted_iota(jnp.int32, lf.shape, 0)
    d = 1
    while d < n:
        lf = lf + jnp.where(row >= d, pltpu.roll(lf, d, 0), 0.0)
        d *= 2
    lf = lf * LOG2E
    fc_ref[...] = lf
    fr_ref[0] = lf.T[:FOX_HEADS, :]


def _fox_gates(f_logit, bias, batch, seq):
    bpad = jnp.zeros((1, LANES), F32).at[0, :FOX_HEADS].set(bias.astype(F32))
    return pl.pallas_call(
        _fox_gate_kernel,
        name="fox_gates",
        out_shape=[jax.ShapeDtypeStruct((batch * seq, LANES), F32),
                   jax.ShapeDtypeStruct((batch, FOX_HEADS, seq), F32)],
        grid=(batch,),
        in_specs=[pl.BlockSpec((seq, LANES), lambda b: (b, 0)),
                  pl.BlockSpec((1, LANES), lambda b: (0, 0))],
        out_specs=[pl.BlockSpec((seq, LANES), lambda b: (b, 0)),
                   pl.BlockSpec((1, FOX_HEADS, seq), lambda b: (b, 0, 0))],
        compiler_params=_params("parallel"),
    )(f_logit, bpad)


def _fox_attn_kernel(q_ref, k_ref, vt_ref, z_ref, fc_ref, fr_ref, o_ref, m_ref, l_ref, acc_ref, fk_ref, *, tq):
    hh = pl.program_id(1)
    qi = pl.program_id(2)
    _flash_init(m_ref, l_ref, acc_ref)

    @pl.when(qi == 0)
    def _():
        fc = fc_ref[...]
        lane = lax.broadcasted_iota(jnp.int32, fc.shape, 1)
        col = jnp.sum(jnp.where(lane == hh, fc, 0.0), axis=-1, keepdims=True)
        fk_ref[...] = jnp.broadcast_to(col, fk_ref.shape)

    q = q_ref[...]
    fq = fr_ref[0, pl.ds(hh, 1), pl.ds(pl.multiple_of(qi * tq, tq), tq)]

    def step(ki, masked):
        off = pl.multiple_of(ki * tq, tq)
        k = k_ref[pl.ds(off, tq), :]
        v_t = vt_ref[:, pl.ds(off, tq)]
        fk = fk_ref[pl.ds(off, tq), :]
        s_t = lax.dot_general(k, q, NT_DIMS, preferred_element_type=F32)
        s_t = s_t + fq - jnp.concatenate([fk] * (tq // LANES), axis=1)
        if masked:
            kp = lax.broadcasted_iota(jnp.int32, s_t.shape, 0)
            qp = lax.broadcasted_iota(jnp.int32, s_t.shape, 1)
            s_t = jnp.where(kp <= qp, s_t, NEG_INF)
        _flash_step_t(s_t, v_t, m_ref, l_ref, acc_ref, 0)

    def body(ki, carry):
        step(ki, False)
        return carry

    lax.fori_loop(0, qi, body, 0)
    step(qi, True)
    o = (acc_ref[0] / l_ref[0]).T
    o_ref[...] = (o * _silu(z_ref[...].astype(F32))).astype(o_ref.dtype)


def _fox_attention(qkz, v_t, f_cols, f_rows, batch, seq, tq=512):
    t = qkz.shape[0]
    tq = min(tq, seq)
    nq = seq // tq
    h, dh = FOX_HEADS, FOX_HEAD_DIM
    return pl.pallas_call(
        functools.partial(_fox_attn_kernel, tq=tq),
        name="fox_attn",
        out_shape=jax.ShapeDtypeStruct((t, h * dh), BF16),
        grid=(batch, h, nq),
        in_specs=[pl.BlockSpec((tq, dh), lambda b, hh, i: (b * nq + i, hh)),
                  pl.BlockSpec((seq, dh), lambda b, hh, i: (b, h + hh)),
                  pl.BlockSpec((dh, seq), lambda b, hh, i: (hh, b)),
                  pl.BlockSpec((tq, dh), lambda b, hh, i: (b * nq + i, 2 * h + hh)),
                  pl.BlockSpec((seq, LANES), lambda b, hh, i: (b, 0)),
                  pl.BlockSpec((1, h, seq), lambda b, hh, i: (b, 0, 0))],
        out_specs=pl.BlockSpec((tq, dh), lambda b, hh, i: (b * nq + i, hh)),
        scratch_shapes=[pltpu.VMEM((1, 1, tq), F32), pltpu.VMEM((1, 1, tq), F32),
                        pltpu.VMEM((1, dh, tq), F32), pltpu.VMEM((seq, LANES), F32)],
        compiler_params=_params("parallel", "parallel", "arbitrary"),
    )(qkz, qkz, v_t, qkz, f_cols, f_rows)


BAND_TQ = 256
BAND_WIN = BAND_PAST + BAND_TQ
BAND_HPB = LANES // BAND_HEAD_DIM
BAND_REV = 1024


def _band_bias_kernel(rev_ref, o_ref):
    x = jnp.broadcast_to(rev_ref[0], (BAND_TQ, BAND_REV))
    x = pltpu.roll(x, BAND_REV - (BAND_TQ - 1), 1, stride=1, stride_axis=0)
    x = x[:, :BAND_WIN]
    qc = lax.broadcasted_iota(jnp.int32, x.shape, 0) // CHUNK
    kc = lax.broadcasted_iota(jnp.int32, x.shape, 1) // CHUNK
    band = (kc >= qc) & (kc <= qc + BAND_PAST // CHUNK)
    o_ref[0] = jnp.where(band, x, NEG_INF)


def _band_bias(rel_table):
    nh = rel_table.shape[0]
    t = rel_table.astype(F32)
    rev = jnp.concatenate([jnp.broadcast_to(t[:, 2 * REL_CLIP:], (nh, BAND_REV // 2)),
                           t[:, :2 * REL_CLIP][:, ::-1]], axis=1)
    return pl.pallas_call(
        _band_bias_kernel,
        name="band_bias",
        out_shape=jax.ShapeDtypeStruct((nh, BAND_TQ, BAND_WIN), F32),
        grid=(nh,),
        in_specs=[pl.BlockSpec((1, 1, BAND_REV), lambda h: (h, 0, 0))],
        out_specs=pl.BlockSpec((1, BAND_TQ, BAND_WIN), lambda h: (h, 0, 0)),
        compiler_params=_params("parallel"),
    )(rev.reshape(nh, 1, BAND_REV))


def _band_attn_kernel(q_ref, k_ref, v_ref, z_ref, bias_ref, o_ref, kp_ref, vp_ref):
    qb = pl.program_id(2)
    scale = BAND_HEAD_DIM ** -0.5

    @pl.when(qb == 0)
    def _():
        zeros = jnp.zeros((BAND_PAST, LANES), BF16)
        kp_ref[:BAND_PAST, :] = zeros
        vp_ref[:BAND_PAST, :] = zeros
        kp_ref[BAND_PAST:, :] = k_ref[...]
        vp_ref[BAND_PAST:, :] = v_ref[...]

    start = pl.multiple_of(qb * BAND_TQ, BAND_TQ)
    kw = kp_ref[pl.ds(start, BAND_WIN), :]
    vw = vp_ref[pl.ds(start, BAND_WIN), :]
    kpos = start - BAND_PAST + lax.broadcasted_iota(jnp.int32, (1, BAND_WIN), 1)
    valid = kpos >= 0
    q = q_ref[...]
    z = z_ref[...].astype(F32)
    outs = []
    for hh in range(BAND_HPB):
        sl = slice(hh * BAND_HEAD_DIM, (hh + 1) * BAND_HEAD_DIM)
        s = lax.dot_general(q[:, sl], kw[:, sl], NT_DIMS, preferred_element_type=F32) * scale
        s = jnp.where(valid, s + bias_ref[hh], NEG_INF)
        p = jnp.exp(s - jnp.max(s, axis=-1, keepdims=True))
        l = jnp.sum(p, axis=-1, keepdims=True)
        o = jnp.dot(p.astype(BF16), vw[:, sl], preferred_element_type=F32) / l
        outs.append(o * _silu(z[:, sl]))
    o_ref[...] = jnp.concatenate(outs, axis=-1).astype(o_ref.dtype)


def _band_attention(qkvz, bias, batch, seq):
    t = qkvz.shape[0]
    nq = seq // BAND_TQ
    hp = BAND_HEADS // BAND_HPB
    return pl.pallas_call(
        _band_attn_kernel,
        name="band_attn",
        out_shape=jax.ShapeDtypeStruct((t, BAND_HEADS * BAND_HEAD_DIM), BF16),
        grid=(batch, hp, nq),
        in_specs=[pl.BlockSpec((BAND_TQ, LANES), lambda b, g, i: (b * nq + i, g)),
                  pl.BlockSpec((seq, LANES), lambda b, g, i: (b, hp + g)),
                  pl.BlockSpec((seq, LANES), lambda b, g, i: (b, 2 * hp + g)),
                  pl.BlockSpec((BAND_TQ, LANES), lambda b, g, i: (b * nq + i, 3 * hp + g)),
                  pl.BlockSpec((BAND_HPB, BAND_TQ, BAND_WIN), lambda b, g, i: (g, 0, 0))],
        out_specs=pl.BlockSpec((BAND_TQ, LANES), lambda b, g, i: (b * nq + i, g)),
        scratch_shapes=[pltpu.VMEM((seq + BAND_PAST, LANES), BF16),
                        pltpu.VMEM((seq + BAND_PAST, LANES), BF16)],
        compiler_params=_params("parallel", "parallel", "arbitrary"),
    )(qkvz, qkvz, qkvz, qkvz, bias)


def _pool_kernel(u_ref, up_ref, z_ref, w_ref, ps_ref, o_ref, *, tm, nb):
    i = pl.program_id(0)
    t0 = (i % nb) * tm
    has_history = t0 > 0
    t = t0 + lax.broadcasted_iota(jnp.int32, (tm, 1), 0)
    for g, win in enumerate(POOL_WINDOWS):
        sl = slice(g * POOL_GROUP, (g + 1) * POOL_GROUP)
        u = u_ref[:, sl]
        acc = jnp.concatenate([jnp.where(has_history, up_ref[:, sl], 0.0), u], axis=0)
        d = 1
        while d < win:
            acc = acc + pltpu.roll(acc, d, 0)
            d *= 2
        wsum = acc[POOL_HALO:, :]
        cnt = jnp.minimum(t + 1, win).astype(F32)
        delta = (wsum / cnt - u).astype(BF16)
        y = jnp.dot(delta, w_ref[g], preferred_element_type=F32) * ps_ref[:, sl]
        o_ref[:, sl] = (y * _silu(z_ref[:, sl])).astype(o_ref.dtype)


def _pool_mix(uz, seq, pool_w, pool_scale, tm=256):
    t = uz.shape[0]
    d = D_MODEL
    tm = min(tm, seq)
    nb = seq // tm
    hb = tm // POOL_HALO
    return pl.pallas_call(
        functools.partial(_pool_kernel, tm=tm, nb=nb),
        name="pool_mix",
        out_shape=jax.ShapeDtypeStruct((t, d), BF16),
        grid=(t // tm,),
        in_specs=[pl.BlockSpec((tm, d), lambda i: (i, 0)),
                  pl.BlockSpec((POOL_HALO, d), lambda i: (jnp.maximum(i * hb - 1, 0), 0)),
                  pl.BlockSpec((tm, d), lambda i: (i, 1)),
                  pl.BlockSpec((len(POOL_WINDOWS), POOL_GROUP, POOL_GROUP), lambda i: (0, 0, 0)),
                  pl.BlockSpec((1, d), lambda i: (0, 0))],
        out_specs=pl.BlockSpec((tm, d), lambda i: (i, 0)),
        compiler_params=_params("parallel"),
    )(uz, uz, uz, pool_w, pool_scale.reshape(1, d).astype(F32))


PROJ_TN = 1024


def _qkz_scale(qk_dim, branch):
    return jnp.concatenate([jnp.full((branch,), qk_dim ** -0.5 * LOG2E, F32), jnp.ones((2 * branch,), F32)])


def kernel(x, c, ada_w, ada_b, norm_pre, norm_post, diff_w_in, diff_w_out, diff_lambda_q1, diff_lambda_k1, diff_lambda_q2, diff_lambda_k2, diff_subln, band_w_in, band_w_out, band_rel_bias, fox_w_in, fox_w_out, fox_forget_bias, pool_w_in, pool_w_out, pool_group_w, pool_scale):
    batch, seq, d = x.shape
    depth = ada_w.shape[0]
    branch = d
    nblk = branch // PROJ_TN
    qkz_blocks = list(range(2 * nblk)) + list(range(3 * nblk, 4 * nblk))
    mod = _ada_mod(c, ada_w, ada_b)
    x2 = x.reshape(batch * seq, d)
    for i in range(depth):
        mod3 = mod[i].reshape(mod.shape[1], 1, 3 * d)
        kind = i % 4
        if kind == 0:
            lambda_init = 0.8 - 0.6 * math.exp(-0.3 * i)
            qkz, h = _norm_proj(x2, seq, norm_pre[i], mod3, diff_w_in.astype(BF16), qkz_blocks,
                                _qkz_scale(DIFF_QK_DIM, branch), emit_h=True)
            v_t = _proj_t(diff_w_in[:, 2 * branch:3 * branch].T.astype(BF16), h)
            o = _diff_attention(qkz, v_t, batch, seq, diff_lambda_q1, diff_lambda_k1, diff_lambda_q2,
                                diff_lambda_k2, diff_subln, lambda_init)
            w_out = diff_w_out
        elif kind == 1:
            qkvz, = _norm_proj(x2, seq, norm_pre[i], mod3, band_w_in.astype(BF16), list(range(4 * nblk)),
                               jnp.ones((4 * branch,), F32))
            o = _band_attention(qkvz, _band_bias(band_rel_bias), batch, seq)
            w_out = band_w_out
        elif kind == 2:
            wf = jnp.pad(fox_w_in[:, 4 * branch:], ((0, 0), (0, LANES - FOX_HEADS))).astype(BF16)
            qkz, h, f_logit = _norm_proj(x2, seq, norm_pre[i], mod3, fox_w_in.astype(BF16), qkz_blocks,
                                         _qkz_scale(FOX_HEAD_DIM, branch), wf=wf, emit_h=True)
            v_t = _proj_t(fox_w_in[:, 2 * branch:3 * branch].T.astype(BF16), h)
            f_cols, f_rows = _fox_gates(f_logit, fox_forget_bias, batch, seq)
            o = _fox_attention(qkz, v_t, f_cols, f_rows, batch, seq)
            w_out = fox_w_out
        else:
            uz, = _norm_proj(x2, seq, norm_pre[i], mod3, pool_w_in.astype(BF16), list(range(2 * nblk)),
                             jnp.ones((2 * branch,), F32), out_dtype=F32)
            o = _pool_mix(uz, seq, pool_group_w.astype(BF16), pool_scale)
            w_out = pool_w_out
        x2 = _out_proj_residual(o, w_out.astype(BF16), x2, seq, mod3, norm_post[i])
    return x2.reshape(batch, seq, d)
```

```python
import functools
import math

import jax
import jax.numpy as jnp
from jax import lax
from jax.experimental import pallas as pl
from jax.experimental.pallas import tpu as pltpu

F32 = jnp.float32
BF16 = jnp.bfloat16

D_MODEL = 2048
EPS = 1e-6
NEG_INF = -1e30
LOG2E = math.log2(math.e)
CHUNK = 64
DIFF_HEADS, DIFF_V_DIM, DIFF_QK_DIM = 8, 256, 128
BAND_HEADS, BAND_HEAD_DIM, BAND_PAST, REL_CLIP = 32, 64, 512, 256
FOX_HEADS, FOX_HEAD_DIM = 16, 128
POOL_WINDOWS = (2, 4, 8, 16)
POOL_GROUP = D_MODEL // len(POOL_WINDOWS)
POOL_HALO = 16

LANES = 128
VMEM_LIMIT = 52 * 1024 * 1024

NT_DIMS = (((1,), (1,)), ((), ()))


def _silu(x):
    return x / (1.0 + jnp.exp(-x))


def _params(*sem):
    return pltpu.CompilerParams(dimension_semantics=sem, vmem_limit_bytes=VMEM_LIMIT)


def _ada_kernel(c_ref, w_ref, b_ref, o_ref):
    ca = _silu(c_ref[...]).astype(BF16)
    o_ref[0] = jnp.dot(ca, w_ref[0].astype(BF16), preferred_element_type=F32) + b_ref[0]


def _ada_mod(c, ada_w, ada_b):
    b, d = c.shape
    bp = -(-b // 8) * 8
    depth, _, n3 = ada_w.shape
    tn = 768
    cp = jnp.pad(c, ((0, bp - b), (0, 0)))
    return pl.pallas_call(
        _ada_kernel,
        name="ada_mod",
        out_shape=jax.ShapeDtypeStruct((depth, bp, n3), F32),
        grid=(depth, n3 // tn),
        in_specs=[pl.BlockSpec((bp, d), lambda l, j: (0, 0)),
                  pl.BlockSpec((1, d, tn), lambda l, j: (l, 0, j)),
                  pl.BlockSpec((1, 1, tn), lambda l, j: (l, 0, j))],
        out_specs=pl.BlockSpec((1, bp, tn), lambda l, j: (l, 0, j)),
        compiler_params=_params("parallel", "parallel"),
    )(cp, ada_w, ada_b.reshape(depth, 1, n3))


def _norm_proj_kernel(x_ref, g_ref, sh_ref, sc_ref, w_ref, cs_ref, *rest, has_f, emit_h, sub):
    rest = list(rest)
    wf_ref = rest.pop(0) if has_f else None
    o_ref = rest.pop(0)
    hout_ref = rest.pop(0) if emit_h else None
    f_ref = rest.pop(0) if has_f else None
    h_ref = rest.pop(0)

    @pl.when(pl.program_id(1) == 0)
    def _():
        g = g_ref[...]
        sc = 1.0 + sc_ref[0]
        sh = sh_ref[0]
        for r in range(x_ref.shape[0] // sub):
            x = x_ref[r * sub:(r + 1) * sub, :]
            y = x * lax.rsqrt(jnp.mean(x * x, axis=-1, keepdims=True) + EPS) * g
            h_ref[r * sub:(r + 1) * sub, :] = (y * sc + sh).astype(BF16)
        if emit_h:
            hout_ref[...] = h_ref[...]
        if has_f:
            f_ref[...] = jnp.dot(h_ref[...], wf_ref[...], preferred_element_type=F32)

    y = jnp.dot(h_ref[...], w_ref[...], preferred_element_type=F32)
    o_ref[...] = (y * cs_ref[...]).astype(o_ref.dtype)


def _norm_proj(x2, seq, g, mod3, w, col_blocks, col_scale, wf=None, emit_h=False, out_dtype=BF16,
               tm=512, tn=1024):
    t, d = x2.shape
    tm = min(tm, seq)
    nb = seq // tm
    nj = len(col_blocks)
    n = nj * tn
    first, skip_from, skip = col_blocks[0], None, 0
    for a, b2 in zip(col_blocks, col_blocks[1:]):
        if b2 != a + 1:
            skip_from, skip = a - first + 1, b2 - a - 1
    if skip_from is None:
        wmap = lambda i, j: (0, j + first)
    else:
        wmap = lambda i, j: (0, j + first + jnp.where(j >= skip_from, skip, 0))
    in_specs = [pl.BlockSpec((tm, d), lambda i, j: (i, 0)),
                pl.BlockSpec((1, d), lambda i, j: (0, 0)),
                pl.BlockSpec((1, 1, d), lambda i, j: (i // nb, 0, 0)),
                pl.BlockSpec((1, 1, d), lambda i, j: (i // nb, 0, 1)),
                pl.BlockSpec((d, tn), wmap),
                pl.BlockSpec((1, tn), lambda i, j: (0, j))]
    out_shape = [jax.ShapeDtypeStruct((t, n), out_dtype)]
    out_specs = [pl.BlockSpec((tm, tn), lambda i, j: (i, j))]
    args = [x2, g.reshape(1, d), mod3, mod3, w, col_scale.reshape(1, n).astype(F32)]
    if wf is not None:
        in_specs.append(pl.BlockSpec((d, LANES), lambda i, j: (0, 0)))
        args.append(wf)
    if emit_h:
        out_shape.append(jax.ShapeDtypeStruct((t, d), BF16))
        out_specs.append(pl.BlockSpec((tm, d), lambda i, j: (i, 0)))
    if wf is not None:
        out_shape.append(jax.ShapeDtypeStruct((t, LANES), F32))
        out_specs.append(pl.BlockSpec((tm, LANES), lambda i, j: (i, 0)))
    return pl.pallas_call(
        functools.partial(_norm_proj_kernel, has_f=wf is not None, emit_h=emit_h, sub=min(256, tm)),
        name="norm_proj",
        out_shape=out_shape,
        grid=(t // tm, nj),
        in_specs=in_specs,
        out_specs=out_specs,
        scratch_shapes=[pltpu.VMEM((tm, d), BF16)],
        compiler_params=_params("parallel", "arbitrary"),
    )(*args)


def _proj_t_kernel(w_ref, h_ref, o_ref):
    o_ref[...] = lax.dot_general(w_ref[...], h_ref[...], NT_DIMS,
                                 preferred_element_type=F32).astype(o_ref.dtype)


def _proj_t(w_t, h, tn=512, tm=1024):
    n, d = w_t.shape
    t = h.shape[0]
    tm = min(tm, t)
    return pl.pallas_call(
        _proj_t_kernel,
        name="proj_t",
        out_shape=jax.ShapeDtypeStruct((n, t), BF16),
        grid=(t // tm, n // tn),
        in_specs=[pl.BlockSpec((tn, d), lambda i, j: (j, 0)),
                  pl.BlockSpec((tm, d), lambda i, j: (i, 0))],
        out_specs=pl.BlockSpec((tn, tm), lambda i, j: (j, i)),
        compiler_params=_params("parallel", "arbitrary"),
    )(w_t, h)


def _out_proj_kernel(o_ref, w_ref, x_ref, gate_ref, g_ref, out_ref):
    y = jnp.dot(o_ref[...], w_ref[...], preferred_element_type=F32)
    yn = y * lax.rsqrt(jnp.mean(y * y, axis=-1, keepdims=True) + EPS) * g_ref[...]
    out_ref[...] = x_ref[...] + gate_ref[0] * yn


def _out_proj_residual(o2, w, x2, seq, mod3, g, tm=256):
    t, d = x2.shape
    tm = min(tm, seq)
    nb = seq // tm
    return pl.pallas_call(
        _out_proj_kernel,
        name="out_proj",
        out_shape=jax.ShapeDtypeStruct((t, d), F32),
        grid=(t // tm,),
        in_specs=[pl.BlockSpec((tm, d), lambda i: (i, 0)),
                  pl.BlockSpec((d, d), lambda i: (0, 0)),
                  pl.BlockSpec((tm, d), lambda i: (i, 0)),
                  pl.BlockSpec((1, 1, d), lambda i: (i // nb, 0, 2)),
                  pl.BlockSpec((1, d), lambda i: (0, 0))],
        out_specs=pl.BlockSpec((tm, d), lambda i: (i, 0)),
        compiler_params=_params("parallel"),
    )(o2, w, x2, mod3, g.reshape(1, d))


def _flash_step_t(s_t, v_t, m_ref, l_ref, acc_ref, idx):
    m_prev = m_ref[idx]
    m_new = jnp.maximum(m_prev, jnp.max(s_t, axis=0, keepdims=True))
    alpha = jnp.exp2(m_prev - m_new)
    p_t = jnp.exp2(s_t - m_new)
    l_ref[idx] = alpha * l_ref[idx] + jnp.sum(p_t, axis=0, keepdims=True)
    acc_ref[idx] = alpha * acc_ref[idx] + jnp.dot(v_t, p_t.astype(BF16), preferred_element_type=F32)
    m_ref[idx] = m_new


def _flash_init(m_ref, l_ref, acc_ref):
    m_ref[...] = jnp.full(m_ref.shape, NEG_INF, F32)
    l_ref[...] = jnp.zeros(l_ref.shape, F32)
    acc_ref[...] = jnp.zeros(acc_ref.shape, F32)


def _causal_sweep(qi, scores, update, s0_ref, s1_ref):
    scores(0, s0_ref)

    def pair(jj, carry):
        j = 2 * jj
        scores(j + 1, s1_ref)
        update(j, s0_ref, False)
        scores(j + 2, s0_ref)
        update(j + 1, s1_ref, False)
        return carry

    lax.fori_loop(0, qi // 2, pair, 0)

    @pl.when(qi % 2 == 1)
    def _():
        scores(qi, s1_ref)
        update(qi - 1, s0_ref, False)
        update(qi, s1_ref, True)

    @pl.when(qi % 2 == 0)
    def _():
        update(qi, s0_ref, True)


def _diff_attn_kernel(q_ref, k_ref, vt_ref, z_ref, lq1_ref, lk1_ref, lq2_ref, lk2_ref, g_ref,
                      o_ref, m_ref, l_ref, acc_ref, s0_ref, s1_ref, *, tq, lambda_init):
    qi = pl.program_id(2)
    _flash_init(m_ref, l_ref, acc_ref)
    q = q_ref[...]

    def scores(ki, s_ref):
        k = k_ref[pl.ds(pl.multiple_of(ki * tq, tq), tq), :]
        for m in range(2):
            sl = slice(m * DIFF_QK_DIM, (m + 1) * DIFF_QK_DIM)
            s_ref[m] = lax.dot_general(k[:, sl], q[:, sl], NT_DIMS, preferred_element_type=F32)

    def update(ki, s_ref, masked):
        v_t = vt_ref[:, pl.ds(pl.multiple_of(ki * tq, tq), tq)]
        for m in range(2):
            s_t = s_ref[m]
            if masked:
                kc = lax.broadcasted_iota(jnp.int32, s_t.shape, 0) // CHUNK
                qc = lax.broadcasted_iota(jnp.int32, s_t.shape, 1) // CHUNK
                s_t = jnp.where(kc <= qc, s_t, NEG_INF)
            _flash_step_t(s_t, v_t, m_ref, l_ref, acc_ref, m)

    _causal_sweep(qi, scores, update, s0_ref, s1_ref)

    lam = (jnp.exp(jnp.sum(lq1_ref[...] * lk1_ref[...], axis=-1, keepdims=True))
           - jnp.exp(jnp.sum(lq2_ref[...] * lk2_ref[...], axis=-1, keepdims=True)) + lambda_init)
    o_t = acc_ref[0] / l_ref[0] - lam * (acc_ref[1] / l_ref[1])
    o = o_t.T
    on = o * lax.rsqrt(jnp.mean(o * o, axis=-1, keepdims=True) + EPS) * g_ref[...]
    on = on * (1.0 - lambda_init)
    o_ref[...] = (on * _silu(z_ref[...].astype(F32))).astype(o_ref.dtype)


def _diff_attention(qkz, v_t, batch, seq, lq1, lk1, lq2, lk2, subln_g, lambda_init, tq=512):
    t = qkz.shape[0]
    tq = min(tq, seq)
    nq = seq // tq
    h, dv = DIFF_HEADS, DIFF_V_DIM
    vec = lambda a: a.reshape(1, -1).astype(F32)
    small = lambda n: pl.BlockSpec((1, n), lambda b, hh, i: (0, 0))
    return pl.pallas_call(
        functools.partial(_diff_attn_kernel, tq=tq, lambda_init=lambda_init),
        name="diff_attn",
        out_shape=jax.ShapeDtypeStruct((t, h * dv), BF16),
        grid=(batch, h, nq),
        in_specs=[pl.BlockSpec((tq, dv), lambda b, hh, i: (b * nq + i, hh)),
                  pl.BlockSpec((seq, dv), lambda b, hh, i: (b, h + hh)),
                  pl.BlockSpec((dv, seq), lambda b, hh, i: (hh, b)),
                  pl.BlockSpec((tq, dv), lambda b, hh, i: (b * nq + i, 2 * h + hh)),
                  small(DIFF_QK_DIM), small(DIFF_QK_DIM), small(DIFF_QK_DIM), small(DIFF_QK_DIM),
                  small(dv)],
        out_specs=pl.BlockSpec((tq, dv), lambda b, hh, i: (b * nq + i, hh)),
        scratch_shapes=[pltpu.VMEM((2, 1, tq), F32), pltpu.VMEM((2, 1, tq), F32),
                        pltpu.VMEM((2, dv, tq), F32),
                        pltpu.VMEM((2, tq, tq), F32), pltpu.VMEM((2, tq, tq), F32)],
        compiler_params=_params("parallel", "parallel", "arbitrary"),
    )(qkz, qkz, v_t, qkz, vec(lq1), vec(lk1), vec(lq2), vec(lk2), vec(subln_g))


def _fox_gate_kernel(f_ref, b_ref, fc_ref, fr_ref):
    x = f_ref[...] + b_ref[...]
    lf = jnp.minimum(x, 0.0) - jnp.log1p(jnp.exp(-jnp.abs(x)))
    n = lf.shape[0]
    row = lax.broadcasted_iota(jnp.int32, lf.shape, 0)
    d = 1
    while d < n:
        lf = lf + jnp.where(row >= d, pltpu.roll(lf, d, 0), 0.0)
        d *= 2
    lf = lf * LOG2E
    fc_ref[...] = lf
    fr_ref[0] = lf.T[:FOX_HEADS, :]


def _fox_gates(f_logit, bias, batch, seq):
    bpad = jnp.zeros((1, LANES), F32).at[0, :FOX_HEADS].set(bias.astype(F32))
    return pl.pallas_call(
        _fox_gate_kernel,
        name="fox_gates",
        out_shape=[jax.ShapeDtypeStruct((batch * seq, LANES), F32),
                   jax.ShapeDtypeStruct((batch, FOX_HEADS, seq), F32)],
        grid=(batch,),
        in_specs=[pl.BlockSpec((seq, LANES), lambda b: (b, 0)),
                  pl.BlockSpec((1, LANES), lambda b: (0, 0))],
        out_specs=[pl.BlockSpec((seq, LANES), lambda b: (b, 0)),
                   pl.BlockSpec((1, FOX_HEADS, seq), lambda b: (b, 0, 0))],
        compiler_params=_params("parallel"),
    )(f_logit, bpad)


def _fox_attn_kernel(q_ref, k_ref, vt_ref, z_ref, fc_ref, fr_ref, o_ref, m_ref, l_ref, acc_ref, fk_ref, s0_ref, s1_ref, *, tq):
    hh = pl.program_id(1)
    qi = pl.program_id(2)
    _flash_init(m_ref, l_ref, acc_ref)

    @pl.when(qi == 0)
    def _():
        fc = fc_ref[...]
        lane = lax.broadcasted_iota(jnp.int32, fc.shape, 1)
        col = jnp.sum(jnp.where(lane == hh, fc, 0.0), axis=-1, keepdims=True)
        fk_ref[...] = jnp.broadcast_to(col, fk_ref.shape)

    q = q_ref[...]
    fq = fr_ref[0, pl.ds(hh, 1), pl.ds(pl.multiple_of(qi * tq, tq), tq)]

    def scores(ki, s_ref):
        off = pl.multiple_of(ki * tq, tq)
        fk = fk_ref[pl.ds(off, tq), :]
        s_t = lax.dot_general(k_ref[pl.ds(off, tq), :], q, NT_DIMS, preferred_element_type=F32)
        s_ref[...] = s_t + fq - jnp.concatenate([fk] * (tq // LANES), axis=1)

    def update(ki, s_ref, masked):
        v_t = vt_ref[:, pl.ds(pl.multiple_of(ki * tq, tq), tq)]
        s_t = s_ref[...]
        if masked:
            kp = lax.broadcasted_iota(jnp.int32, s_t.shape, 0)
            qp = lax.broadcasted_iota(jnp.int32, s_t.shape, 1)
            s_t = jnp.where(kp <= qp, s_t, NEG_INF)
        _flash_step_t(s_t, v_t, m_ref, l_ref, acc_ref, 0)

    _causal_sweep(qi, scores, update, s0_ref, s1_ref)
    o = (acc_ref[0] / l_ref[0]).T
    o_ref[...] = (o * _silu(z_ref[...].astype(F32))).astype(o_ref.dtype)


def _fox_attention(qkz, v_t, f_cols, f_rows, batch, seq, tq=512):
    t = qkz.shape[0]
    tq = min(tq, seq)
    nq = seq // tq
    h, dh = FOX_HEADS, FOX_HEAD_DIM
    return pl.pallas_call(
        functools.partial(_fox_attn_kernel, tq=tq),
        name="fox_attn",
        out_shape=jax.ShapeDtypeStruct((t, h * dh), BF16),
        grid=(batch, h, nq),
        in_specs=[pl.BlockSpec((tq, dh), lambda b, hh, i: (b * nq + i, hh)),
                  pl.BlockSpec((seq, dh), lambda b, hh, i: (b, h + hh)),
                  pl.BlockSpec((dh, seq), lambda b, hh, i: (hh, b)),
                  pl.BlockSpec((tq, dh), lambda b, hh, i: (b * nq + i, 2 * h + hh)),
                  pl.BlockSpec((seq, LANES), lambda b, hh, i: (b, 0)),
                  pl.BlockSpec((1, h, seq), lambda b, hh, i: (b, 0, 0))],
        out_specs=pl.BlockSpec((tq, dh), lambda b, hh, i: (b * nq + i, hh)),
        scratch_shapes=[pltpu.VMEM((1, 1, tq), F32), pltpu.VMEM((1, 1, tq), F32),
                        pltpu.VMEM((1, dh, tq), F32), pltpu.VMEM((seq, LANES), F32),
                        pltpu.VMEM((tq, tq), F32), pltpu.VMEM((tq, tq), F32)],
        compiler_params=_params("parallel", "parallel", "arbitrary"),
    )(qkz, qkz, v_t, qkz, f_cols, f_rows)


BAND_TQ = 256
BAND_WIN = BAND_PAST + BAND_TQ
BAND_HPB = LANES // BAND_HEAD_DIM
BAND_REV = 1024


def _band_bias_kernel(rev_ref, o_ref):
    x = jnp.broadcast_to(rev_ref[0], (BAND_TQ, BAND_REV))
    x = pltpu.roll(x, BAND_REV - (BAND_TQ - 1), 1, stride=1, stride_axis=0)
    x = x[:, :BAND_WIN]
    qc = lax.broadcasted_iota(jnp.int32, x.shape, 0) // CHUNK
    kc = lax.broadcasted_iota(jnp.int32, x.shape, 1) // CHUNK
    band = (kc >= qc) & (kc <= qc + BAND_PAST // CHUNK)
    o_ref[0] = jnp.where(band, x, NEG_INF)


def _band_bias(rel_table):
    nh = rel_table.shape[0]
    t = rel_table.astype(F32)
    rev = jnp.concatenate([jnp.broadcast_to(t[:, 2 * REL_CLIP:], (nh, BAND_REV // 2)),
                           t[:, :2 * REL_CLIP][:, ::-1]], axis=1)
    return pl.pallas_call(
        _band_bias_kernel,
        name="band_bias",
        out_shape=jax.ShapeDtypeStruct((nh, BAND_TQ, BAND_WIN), F32),
        grid=(nh,),
        in_specs=[pl.BlockSpec((1, 1, BAND_REV), lambda h: (h, 0, 0))],
        out_specs=pl.BlockSpec((1, BAND_TQ, BAND_WIN), lambda h: (h, 0, 0)),
        compiler_params=_params("parallel"),
    )(rev.reshape(nh, 1, BAND_REV))


def _band_attn_kernel(q_ref, k_ref, v_ref, z_ref, bias_ref, o_ref, kp_ref, vp_ref):
    qb = pl.program_id(2)
    scale = BAND_HEAD_DIM ** -0.5

    @pl.when(qb == 0)
    def _():
        zeros = jnp.zeros((BAND_PAST, LANES), BF16)
        kp_ref[:BAND_PAST, :] = zeros
        vp_ref[:BAND_PAST, :] = zeros
        kp_ref[BAND_PAST:, :] = k_ref[...]
        vp_ref[BAND_PAST:, :] = v_ref[...]

    start = pl.multiple_of(qb * BAND_TQ, BAND_TQ)
    kw = kp_ref[pl.ds(start, BAND_WIN), :]
    vw = vp_ref[pl.ds(start, BAND_WIN), :]
    kpos = start - BAND_PAST + lax.broadcasted_iota(jnp.int32, (1, BAND_WIN), 1)
    valid = kpos >= 0
    q = q_ref[...]
    z = z_ref[...].astype(F32)
    outs = []
    for hh in range(BAND_HPB):
        sl = slice(hh * BAND_HEAD_DIM, (hh + 1) * BAND_HEAD_DIM)
        s = lax.dot_general(q[:, sl], kw[:, sl], NT_DIMS, preferred_element_type=F32) * scale
        s = jnp.where(valid, s + bias_ref[hh], NEG_INF)
        p = jnp.exp(s - jnp.max(s, axis=-1, keepdims=True))
        l = jnp.sum(p, axis=-1, keepdims=True)
        o = jnp.dot(p.astype(BF16), vw[:, sl], preferred_element_type=F32) / l
        outs.append(o * _silu(z[:, sl]))
    o_ref[...] = jnp.concatenate(outs, axis=-1).astype(o_ref.dtype)


def _band_attention(qkvz, bias, batch, seq):
    t = qkvz.shape[0]
    nq = seq // BAND_TQ
    hp = BAND_HEADS // BAND_HPB
    return pl.pallas_call(
        _band_attn_kernel,
        name="band_attn",
        out_shape=jax.ShapeDtypeStruct((t, BAND_HEADS * BAND_HEAD_DIM), BF16),
        grid=(batch, hp, nq),
        in_specs=[pl.BlockSpec((BAND_TQ, LANES), lambda b, g, i: (b * nq + i, g)),
                  pl.BlockSpec((seq, LANES), lambda b, g, i: (b, hp + g)),
                  pl.BlockSpec((seq, LANES), lambda b, g, i: (b, 2 * hp + g)),
                  pl.BlockSpec((BAND_TQ, LANES), lambda b, g, i: (b * nq + i, 3 * hp + g)),
                  pl.BlockSpec((BAND_HPB, BAND_TQ, BAND_WIN), lambda b, g, i: (g, 0, 0))],
        out_specs=pl.BlockSpec((BAND_TQ, LANES), lambda b, g, i: (b * nq + i, g)),
        scratch_shapes=[pltpu.VMEM((seq + BAND_PAST, LANES), BF16),
                        pltpu.VMEM((seq + BAND_PAST, LANES), BF16)],
        compiler_params=_params("parallel", "parallel", "arbitrary"),
    )(qkvz, qkvz, qkvz, qkvz, bias)


def _pool_kernel(u_ref, up_ref, z_ref, w_ref, ps_ref, o_ref, *, tm, nb):
    i = pl.program_id(0)
    t0 = (i % nb) * tm
    has_history = t0 > 0
    t = t0 + lax.broadcasted_iota(jnp.int32, (tm, 1), 0)
    for g, win in enumerate(POOL_WINDOWS):
        sl = slice(g * POOL_GROUP, (g + 1) * POOL_GROUP)
        u = u_ref[:, sl]
        acc = jnp.concatenate([jnp.where(has_history, up_ref[:, sl], 0.0), u], axis=0)
        d = 1
        while d < win:
            acc = acc + pltpu.roll(acc, d, 0)
            d *= 2
        wsum = acc[POOL_HALO:, :]
        cnt = jnp.minimum(t + 1, win).astype(F32)
        delta = (wsum / cnt - u).astype(BF16)
        y = jnp.dot(delta, w_ref[g], preferred_element_type=F32) * ps_ref[:, sl]
        o_ref[:, sl] = (y * _silu(z_ref[:, sl])).astype(o_ref.dtype)


def _pool_mix(uz, seq, pool_w, pool_scale, tm=256):
    t = uz.shape[0]
    d = D_MODEL
    tm = min(tm, seq)
    nb = seq // tm
    hb = tm // POOL_HALO
    return pl.pallas_call(
        functools.partial(_pool_kernel, tm=tm, nb=nb),
        name="pool_mix",
        out_shape=jax.ShapeDtypeStruct((t, d), BF16),
        grid=(t // tm,),
        in_specs=[pl.BlockSpec((tm, d), lambda i: (i, 0)),
                  pl.BlockSpec((POOL_HALO, d), lambda i: (jnp.maximum(i * hb - 1, 0), 0)),
                  pl.BlockSpec((tm, d), lambda i: (i, 1)),
                  pl.BlockSpec((len(POOL_WINDOWS), POOL_GROUP, POOL_GROUP), lambda i: (0, 0, 0)),
                  pl.BlockSpec((1, d), lambda i: (0, 0))],
        out_specs=pl.BlockSpec((tm, d), lambda i: (i, 0)),
        compiler_params=_params("parallel"),
    )(uz, uz, uz, pool_w, pool_scale.reshape(1, d).astype(F32))


PROJ_TN = 1024


def _qkz_scale(qk_dim, branch):
    return jnp.concatenate([jnp.full((branch,), qk_dim ** -0.5 * LOG2E, F32), jnp.ones((2 * branch,), F32)])


def kernel(x, c, ada_w, ada_b, norm_pre, norm_post, diff_w_in, diff_w_out, diff_lambda_q1, diff_lambda_k1, diff_lambda_q2, diff_lambda_k2, diff_subln, band_w_in, band_w_out, band_rel_bias, fox_w_in, fox_w_out, fox_forget_bias, pool_w_in, pool_w_out, pool_group_w, pool_scale):
    batch, seq, d = x.shape
    depth = ada_w.shape[0]
    branch = d
    nblk = branch // PROJ_TN
    qkz_blocks = list(range(2 * nblk)) + list(range(3 * nblk, 4 * nblk))
    mod = _ada_mod(c, ada_w, ada_b)
    x2 = x.reshape(batch * seq, d)
    for i in range(depth):
        mod3 = mod[i].reshape(mod.shape[1], 1, 3 * d)
        kind = i % 4
        if kind == 0:
            lambda_init = 0.8 - 0.6 * math.exp(-0.3 * i)
            qkz, h = _norm_proj(x2, seq, norm_pre[i], mod3, diff_w_in.astype(BF16), qkz_blocks,
                                _qkz_scale(DIFF_QK_DIM, branch), emit_h=True)
            v_t = _proj_t(diff_w_in[:, 2 * branch:3 * branch].T.astype(BF16), h)
            o = _diff_attention(qkz, v_t, batch, seq, diff_lambda_q1, diff_lambda_k1, diff_lambda_q2,
                                diff_lambda_k2, diff_subln, lambda_init)
            w_out = diff_w_out
        elif kind == 1:
            qkvz, = _norm_proj(x2, seq, norm_pre[i], mod3, band_w_in.astype(BF16), list(range(4 * nblk)),
                               jnp.ones((4 * branch,), F32))
            o = _band_attention(qkvz, _band_bias(band_rel_bias), batch, seq)
            w_out = band_w_out
        elif kind == 2:
            wf = jnp.pad(fox_w_in[:, 4 * branch:], ((0, 0), (0, LANES - FOX_HEADS))).astype(BF16)
            qkz, h, f_logit = _norm_proj(x2, seq, norm_pre[i], mod3, fox_w_in.astype(BF16), qkz_blocks,
                                         _qkz_scale(FOX_HEAD_DIM, branch), wf=wf, emit_h=True)
            v_t = _proj_t(fox_w_in[:, 2 * branch:3 * branch].T.astype(BF16), h)
            f_cols, f_rows = _fox_gates(f_logit, fox_forget_bias, batch, seq)
            o = _fox_attention(qkz, v_t, f_cols, f_rows, batch, seq)
            w_out = fox_w_out
        else:
            uz, = _norm_proj(x2, seq, norm_pre[i], mod3, pool_w_in.astype(BF16), list(range(2 * nblk)),
                             jnp.ones((2 * branch,), F32), out_dtype=F32)
            o = _pool_mix(uz, seq, pool_group_w.astype(BF16), pool_scale)
            w_out = pool_w_out
        x2 = _out_proj_residual(o, w_out.astype(BF16), x2, seq, mod3, norm_post[i])
    return x2.reshape(batch, seq, d)
```

```python
import functools
import math

import jax
import jax.numpy as jnp
from jax import lax
from jax.experimental import pallas as pl
from jax.experimental.pallas import tpu as pltpu

F32 = jnp.float32
BF16 = jnp.bfloat16

D_MODEL = 2048
EPS = 1e-6
NEG_INF = -1e30
LOG2E = math.log2(math.e)
CHUNK = 64
DIFF_HEADS, DIFF_V_DIM, DIFF_QK_DIM = 8, 256, 128
BAND_HEADS, BAND_HEAD_DIM, BAND_PAST, REL_CLIP = 32, 64, 512, 256
FOX_HEADS, FOX_HEAD_DIM = 16, 128
POOL_WINDOWS = (2, 4, 8, 16)
POOL_GROUP = D_MODEL // len(POOL_WINDOWS)
POOL_HALO = 16

LANES = 128
VMEM_LIMIT = 52 * 1024 * 1024

NT_DIMS = (((1,), (1,)), ((), ()))


def _silu(x):
    return x / (1.0 + jnp.exp(-x))


def _params(*sem):
    return pltpu.CompilerParams(dimension_semantics=sem, vmem_limit_bytes=VMEM_LIMIT)


def _ada_kernel(c_ref, w_ref, b_ref, o_ref):
    ca = _silu(c_ref[...]).astype(BF16)
    o_ref[0] = jnp.dot(ca, w_ref[0].astype(BF16), preferred_element_type=F32) + b_ref[0]


def _ada_mod(c, ada_w, ada_b):
    b, d = c.shape
    bp = -(-b // 8) * 8
    depth, _, n3 = ada_w.shape
    tn = 768
    cp = jnp.pad(c, ((0, bp - b), (0, 0)))
    return pl.pallas_call(
        _ada_kernel,
        name="ada_mod",
        out_shape=jax.ShapeDtypeStruct((depth, bp, n3), F32),
        grid=(depth, n3 // tn),
        in_specs=[pl.BlockSpec((bp, d), lambda l, j: (0, 0)),
                  pl.BlockSpec((1, d, tn), lambda l, j: (l, 0, j)),
                  pl.BlockSpec((1, 1, tn), lambda l, j: (l, 0, j))],
        out_specs=pl.BlockSpec((1, bp, tn), lambda l, j: (l, 0, j)),
        compiler_params=_params("parallel", "parallel"),
    )(cp, ada_w, ada_b.reshape(depth, 1, n3))


def _norm_proj_kernel(x_ref, g_ref, sh_ref, sc_ref, w_ref, cs_ref, *rest, has_f, emit_h, sub):
    rest = list(rest)
    wf_ref = rest.pop(0) if has_f else None
    o_ref = rest.pop(0)
    hout_ref = rest.pop(0) if emit_h else None
    f_ref = rest.pop(0) if has_f else None
    h_ref = rest.pop(0)

    @pl.when(pl.program_id(1) == 0)
    def _():
        g = g_ref[...]
        sc = 1.0 + sc_ref[0]
        sh = sh_ref[0]
        for r in range(x_ref.shape[0] // sub):
            x = x_ref[r * sub:(r + 1) * sub, :]
            y = x * lax.rsqrt(jnp.mean(x * x, axis=-1, keepdims=True) + EPS) * g
            h_ref[r * sub:(r + 1) * sub, :] = (y * sc + sh).astype(BF16)
        if emit_h:
            hout_ref[...] = h_ref[...]
        if has_f:
            f_ref[...] = jnp.dot(h_ref[...], wf_ref[...], preferred_element_type=F32)

    y = jnp.dot(h_ref[...], w_ref[...], preferred_element_type=F32)
    o_ref[...] = (y * cs_ref[...]).astype(o_ref.dtype)


def _norm_proj(x2, seq, g, mod3, w, col_blocks, col_scale, wf=None, emit_h=False, out_dtype=BF16,
               tm=512, tn=1024):
    t, d = x2.shape
    tm = min(tm, seq)
    nb = seq // tm
    nj = len(col_blocks)
    n = nj * tn
    first, skip_from, skip = col_blocks[0], None, 0
    for a, b2 in zip(col_blocks, col_blocks[1:]):
        if b2 != a + 1:
            skip_from, skip = a - first + 1, b2 - a - 1
    if skip_from is None:
        wmap = lambda i, j: (0, j + first)
    else:
        wmap = lambda i, j: (0, j + first + jnp.where(j >= skip_from, skip, 0))
    in_specs = [pl.BlockSpec((tm, d), lambda i, j: (i, 0)),
                pl.BlockSpec((1, d), lambda i, j: (0, 0)),
                pl.BlockSpec((1, 1, d), lambda i, j: (i // nb, 0, 0)),
                pl.BlockSpec((1, 1, d), lambda i, j: (i // nb, 0, 1)),
                pl.BlockSpec((d, tn), wmap),
                pl.BlockSpec((1, tn), lambda i, j: (0, j))]
    out_shape = [jax.ShapeDtypeStruct((t, n), out_dtype)]
    out_specs = [pl.BlockSpec((tm, tn), lambda i, j: (i, j))]
    args = [x2, g.reshape(1, d), mod3, mod3, w, col_scale.reshape(1, n).astype(F32)]
    if wf is not None:
        in_specs.append(pl.BlockSpec((d, LANES), lambda i, j: (0, 0)))
        args.append(wf)
    if emit_h:
        out_shape.append(jax.ShapeDtypeStruct((t, d), BF16))
        out_specs.append(pl.BlockSpec((tm, d), lambda i, j: (i, 0)))
    if wf is not None:
        out_shape.append(jax.ShapeDtypeStruct((t, LANES), F32))
        out_specs.append(pl.BlockSpec((tm, LANES), lambda i, j: (i, 0)))
    return pl.pallas_call(
        functools.partial(_norm_proj_kernel, has_f=wf is not None, emit_h=emit_h, sub=min(256, tm)),
        name="norm_proj",
        out_shape=out_shape,
        grid=(t // tm, nj),
        in_specs=in_specs,
        out_specs=out_specs,
        scratch_shapes=[pltpu.VMEM((tm, d), BF16)],
        compiler_params=_params("parallel", "arbitrary"),
    )(*args)


def _proj_t_kernel(w_ref, h_ref, o_ref):
    o_ref[...] = lax.dot_general(w_ref[...], h_ref[...], NT_DIMS,
                                 preferred_element_type=F32).astype(o_ref.dtype)


def _proj_t(w_t, h, tn=512, tm=1024):
    n, d = w_t.shape
    t = h.shape[0]
    tm = min(tm, t)
    return pl.pallas_call(
        _proj_t_kernel,
        name="proj_t",
        out_shape=jax.ShapeDtypeStruct((n, t), BF16),
        grid=(t // tm, n // tn),
        in_specs=[pl.BlockSpec((tn, d), lambda i, j: (j, 0)),
                  pl.BlockSpec((tm, d), lambda i, j: (i, 0))],
        out_specs=pl.BlockSpec((tn, tm), lambda i, j: (j, i)),
        compiler_params=_params("parallel", "arbitrary"),
    )(w_t, h)


def _out_proj_kernel(o_ref, w_ref, x_ref, gate_ref, g_ref, out_ref):
    y = jnp.dot(o_ref[...], w_ref[...], preferred_element_type=F32)
    yn = y * lax.rsqrt(jnp.mean(y * y, axis=-1, keepdims=True) + EPS) * g_ref[...]
    out_ref[...] = x_ref[...] + gate_ref[0] * yn


def _out_proj_residual(o2, w, x2, seq, mod3, g, tm=256):
    t, d = x2.shape
    tm = min(tm, seq)
    nb = seq // tm
    return pl.pallas_call(
        _out_proj_kernel,
        name="out_proj",
        out_shape=jax.ShapeDtypeStruct((t, d), F32),
        grid=(t // tm,),
        in_specs=[pl.BlockSpec((tm, d), lambda i: (i, 0)),
                  pl.BlockSpec((d, d), lambda i: (0, 0)),
                  pl.BlockSpec((tm, d), lambda i: (i, 0)),
                  pl.BlockSpec((1, 1, d), lambda i: (i // nb, 0, 2)),
                  pl.BlockSpec((1, d), lambda i: (0, 0))],
        out_specs=pl.BlockSpec((tm, d), lambda i: (i, 0)),
        compiler_params=_params("parallel"),
    )(o2, w, x2, mod3, g.reshape(1, d))


def _flash_step_t(s_t, v_t, m_ref, l_ref, acc_ref, idx):
    m_prev = m_ref[idx]
    m_new = jnp.maximum(m_prev, jnp.max(s_t, axis=0, keepdims=True))
    alpha = jnp.exp2(m_prev - m_new)
    p_t = jnp.exp2(s_t - m_new)
    l_ref[idx] = alpha * l_ref[idx] + jnp.sum(p_t, axis=0, keepdims=True)
    acc_ref[idx] = alpha * acc_ref[idx] + jnp.dot(v_t, p_t.astype(BF16), preferred_element_type=F32)
    m_ref[idx] = m_new


def _flash_init(m_ref, l_ref, acc_ref):
    m_ref[...] = jnp.full(m_ref.shape, NEG_INF, F32)
    l_ref[...] = jnp.zeros(l_ref.shape, F32)
    acc_ref[...] = jnp.zeros(acc_ref.shape, F32)


def _causal_sweep(qi, scores, update, s0_ref, s1_ref):
    scores(0, s0_ref)

    def pair(jj, carry):
        j = 2 * jj
        scores(j + 1, s1_ref)
        update(j, s0_ref, False)
        scores(j + 2, s0_ref)
        update(j + 1, s1_ref, False)
        return carry

    lax.fori_loop(0, qi // 2, pair, 0)

    @pl.when(qi % 2 == 1)
    def _():
        scores(qi, s1_ref)
        update(qi - 1, s0_ref, False)
        update(qi, s1_ref, True)

    @pl.when(qi % 2 == 0)
    def _():
        update(qi, s0_ref, True)


def _diff_attn_kernel(q_ref, k_ref, vt_ref, z_ref, lq1_ref, lk1_ref, lq2_ref, lk2_ref, g_ref,
                      o_ref, m_ref, l_ref, acc_ref, s0_ref, s1_ref, *, tq, lambda_init):
    qi = pl.program_id(2)
    _flash_init(m_ref, l_ref, acc_ref)
    q = q_ref[...]

    def scores(ki, s_ref):
        k = k_ref[pl.ds(pl.multiple_of(ki * tq, tq), tq), :]
        for m in range(2):
            sl = slice(m * DIFF_QK_DIM, (m + 1) * DIFF_QK_DIM)
            s_ref[m] = lax.dot_general(k[:, sl], q[:, sl], NT_DIMS, preferred_element_type=F32)

    def update(ki, s_ref, masked):
        v_t = vt_ref[:, pl.ds(pl.multiple_of(ki * tq, tq), tq)]
        for m in range(2):
            s_t = s_ref[m]
            if masked:
                kc = lax.broadcasted_iota(jnp.int32, s_t.shape, 0) // CHUNK
                qc = lax.broadcasted_iota(jnp.int32, s_t.shape, 1) // CHUNK
                s_t = jnp.where(kc <= qc, s_t, NEG_INF)
            _flash_step_t(s_t, v_t, m_ref, l_ref, acc_ref, m)

    _causal_sweep(qi, scores, update, s0_ref, s1_ref)

    lam = (jnp.exp(jnp.sum(lq1_ref[...] * lk1_ref[...], axis=-1, keepdims=True))
           - jnp.exp(jnp.sum(lq2_ref[...] * lk2_ref[...], axis=-1, keepdims=True)) + lambda_init)
    o_t = acc_ref[0] / l_ref[0] - lam * (acc_ref[1] / l_ref[1])
    o = o_t.T
    on = o * lax.rsqrt(jnp.mean(o * o, axis=-1, keepdims=True) + EPS) * g_ref[...]
    on = on * (1.0 - lambda_init)
    o_ref[...] = (on * _silu(z_ref[...].astype(F32))).astype(o_ref.dtype)


def _diff_attention(qkz, v_t, batch, seq, lq1, lk1, lq2, lk2, subln_g, lambda_init, tq=512):
    t = qkz.shape[0]
    tq = min(tq, seq)
    nq = seq // tq
    h, dv = DIFF_HEADS, DIFF_V_DIM
    vec = lambda a: a.reshape(1, -1).astype(F32)
    small = lambda n: pl.BlockSpec((1, n), lambda b, hh, i: (0, 0))
    return pl.pallas_call(
        functools.partial(_diff_attn_kernel, tq=tq, lambda_init=lambda_init),
        name="diff_attn",
        out_shape=jax.ShapeDtypeStruct((t, h * dv), BF16),
        grid=(batch, h, nq),
        in_specs=[pl.BlockSpec((tq, dv), lambda b, hh, i: (b * nq + i, hh)),
                  pl.BlockSpec((seq, dv), lambda b, hh, i: (b, h + hh)),
                  pl.BlockSpec((dv, seq), lambda b, hh, i: (hh, b)),
                  pl.BlockSpec((tq, dv), lambda b, hh, i: (b * nq + i, 2 * h + hh)),
                  small(DIFF_QK_DIM), small(DIFF_QK_DIM), small(DIFF_QK_DIM), small(DIFF_QK_DIM),
                  small(dv)],
        out_specs=pl.BlockSpec((tq, dv), lambda b, hh, i: (b * nq + i, hh)),
        scratch_shapes=[pltpu.VMEM((2, 1, tq), F32), pltpu.VMEM((2, 1, tq), F32),
                        pltpu.VMEM((2, dv, tq), F32),
                        pltpu.VMEM((2, tq, tq), F32), pltpu.VMEM((2, tq, tq), F32)],
        compiler_params=_params("parallel", "parallel", "arbitrary"),
    )(qkz, qkz, v_t, qkz, vec(lq1), vec(lk1), vec(lq2), vec(lk2), vec(subln_g))


def _fox_gate_kernel(f_ref, b_ref, fc_ref, fr_ref):
    x = f_ref[...] + b_ref[...]
    lf = jnp.minimum(x, 0.0) - jnp.log1p(jnp.exp(-jnp.abs(x)))
    n = lf.shape[0]
    row = lax.broadcasted_iota(jnp.int32, lf.shape, 0)
    d = 1
    while d < n:
        lf = lf + jnp.where(row >= d, pltpu.roll(lf, d, 0), 0.0)
        d *= 2
    lf = lf * LOG2E
    fc_ref[...] = lf
    fr_ref[0] = lf.T[:FOX_HEADS, :]


def _fox_gates(f_logit, bias, batch, seq):
    bpad = jnp.zeros((1, LANES), F32).at[0, :FOX_HEADS].set(bias.astype(F32))
    return pl.pallas_call(
        _fox_gate_kernel,
        name="fox_gates",
        out_shape=[jax.ShapeDtypeStruct((batch * seq, LANES), F32),
                   jax.ShapeDtypeStruct((batch, FOX_HEADS, seq), F32)],
        grid=(batch,),
        in_specs=[pl.BlockSpec((seq, LANES), lambda b: (b, 0)),
                  pl.BlockSpec((1, LANES), lambda b: (0, 0))],
        out_specs=[pl.BlockSpec((seq, LANES), lambda b: (b, 0)),
                   pl.BlockSpec((1, FOX_HEADS, seq), lambda b: (b, 0, 0))],
        compiler_params=_params("parallel"),
    )(f_logit, bpad)


def _fox_attn_kernel(q_ref, k_ref, vt_ref, z_ref, fc_ref, fr_ref, o_ref, m_ref, l_ref, acc_ref, fk_ref, s0_ref, s1_ref, *, tq):
    hh = pl.program_id(1)
    qi = pl.program_id(2)
    _flash_init(m_ref, l_ref, acc_ref)

    @pl.when(qi == 0)
    def _():
        fc = fc_ref[...]
        lane = lax.broadcasted_iota(jnp.int32, fc.shape, 1)
        col = jnp.sum(jnp.where(lane == hh, fc, 0.0), axis=-1, keepdims=True)
        fk_ref[...] = jnp.broadcast_to(col, fk_ref.shape)

    q = q_ref[...]
    fq = fr_ref[0, pl.ds(hh, 1), pl.ds(pl.multiple_of(qi * tq, tq), tq)]

    def scores(ki, s_ref):
        off = pl.multiple_of(ki * tq, tq)
        fk = fk_ref[pl.ds(off, tq), :]
        s_t = lax.dot_general(k_ref[pl.ds(off, tq), :], q, NT_DIMS, preferred_element_type=F32)
        s_ref[...] = s_t + fq - jnp.concatenate([fk] * (tq // LANES), axis=1)

    def update(ki, s_ref, masked):
        v_t = vt_ref[:, pl.ds(pl.multiple_of(ki * tq, tq), tq)]
        s_t = s_ref[...]
        if masked:
            kp = lax.broadcasted_iota(jnp.int32, s_t.shape, 0)
            qp = lax.broadcasted_iota(jnp.int32, s_t.shape, 1)
            s_t = jnp.where(kp <= qp, s_t, NEG_INF)
        _flash_step_t(s_t, v_t, m_ref, l_ref, acc_ref, 0)

    _causal_sweep(qi, scores, update, s0_ref, s1_ref)
    o = (acc_ref[0] / l_ref[0]).T
    o_ref[...] = (o * _silu(z_ref[...].astype(F32))).astype(o_ref.dtype)


def _fox_attention(qkz, v_t, f_cols, f_rows, batch, seq, tq=512):
    t = qkz.shape[0]
    tq = min(tq, seq)
    nq = seq // tq
    h, dh = FOX_HEADS, FOX_HEAD_DIM
    return pl.pallas_call(
        functools.partial(_fox_attn_kernel, tq=tq),
        name="fox_attn",
        out_shape=jax.ShapeDtypeStruct((t, h * dh), BF16),
        grid=(batch, h, nq),
        in_specs=[pl.BlockSpec((tq, dh), lambda b, hh, i: (b * nq + i, hh)),
                  pl.BlockSpec((seq, dh), lambda b, hh, i: (b, h + hh)),
                  pl.BlockSpec((dh, seq), lambda b, hh, i: (hh, b)),
                  pl.BlockSpec((tq, dh), lambda b, hh, i: (b * nq + i, 2 * h + hh)),
                  pl.BlockSpec((seq, LANES), lambda b, hh, i: (b, 0)),
                  pl.BlockSpec((1, h, seq), lambda b, hh, i: (b, 0, 0))],
        out_specs=pl.BlockSpec((tq, dh), lambda b, hh, i: (b * nq + i, hh)),
        scratch_shapes=[pltpu.VMEM((1, 1, tq), F32), pltpu.VMEM((1, 1, tq), F32),
                        pltpu.VMEM((1, dh, tq), F32), pltpu.VMEM((seq, LANES), F32),
                        pltpu.VMEM((tq, tq), F32), pltpu.VMEM((tq, tq), F32)],
        compiler_params=_params("parallel", "parallel", "arbitrary"),
    )(qkz, qkz, v_t, qkz, f_cols, f_rows)


BAND_TQ = 256
BAND_WIN = BAND_PAST + BAND_TQ
BAND_HPB = LANES // BAND_HEAD_DIM
BAND_REV = 1024


def _band_bias_kernel(row_ref, o_ref):
    x = jnp.broadcast_to(row_ref[0], (BAND_WIN, BAND_REV))
    x = pltpu.roll(x, BAND_REV - BAND_WIN, 1, stride=1, stride_axis=0)
    x = x[:, :BAND_TQ]
    kc = lax.broadcasted_iota(jnp.int32, x.shape, 0) // CHUNK
    qc = lax.broadcasted_iota(jnp.int32, x.shape, 1) // CHUNK
    band = (kc >= qc) & (kc <= qc + BAND_PAST // CHUNK)
    o_ref[0] = jnp.where(band, x * LOG2E, NEG_INF)


def _band_bias(rel_table):
    nh = rel_table.shape[0]
    t = rel_table.astype(F32)
    row = jnp.concatenate([t, jnp.broadcast_to(t[:, 2 * REL_CLIP:], (nh, BAND_REV - t.shape[1]))], axis=1)
    return pl.pallas_call(
        _band_bias_kernel,
        name="band_bias",
        out_shape=jax.ShapeDtypeStruct((nh, BAND_WIN, BAND_TQ), F32),
        grid=(nh,),
        in_specs=[pl.BlockSpec((1, 1, BAND_REV), lambda h: (h, 0, 0))],
        out_specs=pl.BlockSpec((1, BAND_WIN, BAND_TQ), lambda h: (h, 0, 0)),
        compiler_params=_params("parallel"),
    )(row.reshape(nh, 1, BAND_REV))


def _band_attn_kernel(q_ref, k_ref, vt_ref, z_ref, bias_ref, o_ref, kp_ref, vtp_ref, s0_ref, s1_ref, *, nq):
    hd = BAND_HEAD_DIM
    for hh in range(BAND_HPB):
        kp_ref[hh, :BAND_PAST, :] = jnp.zeros((BAND_PAST, hd), BF16)
        kp_ref[hh, BAND_PAST:, :] = k_ref[:, hh * hd:(hh + 1) * hd]
    vtp_ref[:, :BAND_PAST] = jnp.zeros((LANES, BAND_PAST), BF16)
    vtp_ref[:, BAND_PAST:] = vt_ref[...]

    def scores(j, s_ref):
        start = pl.multiple_of(j * BAND_TQ, BAND_TQ)
        q = q_ref[pl.ds(start, BAND_TQ), :]
        for hh in range(BAND_HPB):
            s_ref[hh] = lax.dot_general(kp_ref[hh, pl.ds(start, BAND_WIN), :], q[:, hh * hd:(hh + 1) * hd],
                                        NT_DIMS, preferred_element_type=F32)

    def update(j, s_ref, masked):
        start = pl.multiple_of(j * BAND_TQ, BAND_TQ)
        outs = []
        for hh in range(BAND_HPB):
            s_t = s_ref[hh] + bias_ref[hh]
            if masked:
                kpos = start - BAND_PAST + lax.broadcasted_iota(jnp.int32, (BAND_WIN, 1), 0)
                s_t = jnp.where(kpos >= 0, s_t, NEG_INF)
            p_t = jnp.exp2(s_t - jnp.max(s_t, axis=0, keepdims=True))
            l = jnp.sum(p_t, axis=0, keepdims=True)
            v_t = vtp_ref[hh * hd:(hh + 1) * hd, pl.ds(start, BAND_WIN)]
            outs.append(jnp.dot(v_t, p_t.astype(BF16), preferred_element_type=F32) / l)
        o = jnp.concatenate(outs, axis=0).T
        z = z_ref[pl.ds(start, BAND_TQ), :].astype(F32)
        o_ref[pl.ds(start, BAND_TQ), :] = (o * _silu(z)).astype(o_ref.dtype)

    scores(0, s0_ref)
    scores(1, s1_ref)
    update(0, s0_ref, True)
    scores(2, s0_ref)
    update(1, s1_ref, True)

    def pair(jj, carry):
        j = 2 + 2 * jj
        scores(j + 1, s1_ref)
        update(j, s0_ref, False)
        scores(j + 2, s0_ref)
        update(j + 1, s1_ref, False)
        return carry

    lax.fori_loop(0, (nq - 4) // 2, pair, 0)
    scores(nq - 1, s1_ref)
    update(nq - 2, s0_ref, False)
    update(nq - 1, s1_ref, False)


def _band_attention(qkz, v_t, bias_t, batch, seq):
    t = qkz.shape[0]
    nq = seq // BAND_TQ
    assert nq >= 4 and nq % 2 == 0
    hp = BAND_HEADS // BAND_HPB
    return pl.pallas_call(
        functools.partial(_band_attn_kernel, nq=nq),
        name="band_attn",
        out_shape=jax.ShapeDtypeStruct((t, BAND_HEADS * BAND_HEAD_DIM), BF16),
        grid=(batch, hp),
        in_specs=[pl.BlockSpec((seq, LANES), lambda b, g: (b, g)),
                  pl.BlockSpec((seq, LANES), lambda b, g: (b, hp + g)),
                  pl.BlockSpec((LANES, seq), lambda b, g: (g, b)),
                  pl.BlockSpec((seq, LANES), lambda b, g: (b, 2 * hp + g)),
                  pl.BlockSpec((BAND_HPB, BAND_WIN, BAND_TQ), lambda b, g: (g, 0, 0))],
        out_specs=pl.BlockSpec((seq, LANES), lambda b, g: (b, g)),
        scratch_shapes=[pltpu.VMEM((BAND_HPB, seq + BAND_PAST, BAND_HEAD_DIM), BF16),
                        pltpu.VMEM((LANES, seq + BAND_PAST), BF16),
                        pltpu.VMEM((BAND_HPB, BAND_WIN, BAND_TQ), F32),
                        pltpu.VMEM((BAND_HPB, BAND_WIN, BAND_TQ), F32)],
        compiler_params=_params("parallel", "parallel"),
    )(qkz, qkz, v_t, qkz, bias_t)


def _pool_kernel(u_ref, up_ref, z_ref, w_ref, ps_ref, o_ref, *, tm, nb):
    i = pl.program_id(0)
    t0 = (i % nb) * tm
    has_history = t0 > 0
    t = t0 + lax.broadcasted_iota(jnp.int32, (tm, 1), 0)
    for g, win in enumerate(POOL_WINDOWS):
        sl = slice(g * POOL_GROUP, (g + 1) * POOL_GROUP)
        u = u_ref[:, sl]
        acc = jnp.concatenate([jnp.where(has_history, up_ref[:, sl], 0.0), u], axis=0)
        d = 1
        while d < win:
            acc = acc + pltpu.roll(acc, d, 0)
            d *= 2
        wsum = acc[POOL_HALO:, :]
        cnt = jnp.minimum(t + 1, win).astype(F32)
        delta = (wsum / cnt - u).astype(BF16)
        y = jnp.dot(delta, w_ref[g], preferred_element_type=F32) * ps_ref[:, sl]
        o_ref[:, sl] = (y * _silu(z_ref[:, sl])).astype(o_ref.dtype)


def _pool_mix(uz, seq, pool_w, pool_scale, tm=256):
    t = uz.shape[0]
    d = D_MODEL
    tm = min(tm, seq)
    nb = seq // tm
    hb = tm // POOL_HALO
    return pl.pallas_call(
        functools.partial(_pool_kernel, tm=tm, nb=nb),
        name="pool_mix",
        out_shape=jax.ShapeDtypeStruct((t, d), BF16),
        grid=(t // tm,),
        in_specs=[pl.BlockSpec((tm, d), lambda i: (i, 0)),
                  pl.BlockSpec((POOL_HALO, d), lambda i: (jnp.maximum(i * hb - 1, 0), 0)),
                  pl.BlockSpec((tm, d), lambda i: (i, 1)),
                  pl.BlockSpec((len(POOL_WINDOWS), POOL_GROUP, POOL_GROUP), lambda i: (0, 0, 0)),
                  pl.BlockSpec((1, d), lambda i: (0, 0))],
        out_specs=pl.BlockSpec((tm, d), lambda i: (i, 0)),
        compiler_params=_params("parallel"),
    )(uz, uz, uz, pool_w, pool_scale.reshape(1, d).astype(F32))


PROJ_TN = 1024


def _qkz_scale(qk_dim, branch):
    return jnp.concatenate([jnp.full((branch,), qk_dim ** -0.5 * LOG2E, F32), jnp.ones((2 * branch,), F32)])


def kernel(x, c, ada_w, ada_b, norm_pre, norm_post, diff_w_in, diff_w_out, diff_lambda_q1, diff_lambda_k1, diff_lambda_q2, diff_lambda_k2, diff_subln, band_w_in, band_w_out, band_rel_bias, fox_w_in, fox_w_out, fox_forget_bias, pool_w_in, pool_w_out, pool_group_w, pool_scale):
    batch, seq, d = x.shape
    depth = ada_w.shape[0]
    branch = d
    nblk = branch // PROJ_TN
    qkz_blocks = list(range(2 * nblk)) + list(range(3 * nblk, 4 * nblk))
    mod = _ada_mod(c, ada_w, ada_b)
    x2 = x.reshape(batch * seq, d)
    for i in range(depth):
        mod3 = mod[i].reshape(mod.shape[1], 1, 3 * d)
        kind = i % 4
        if kind == 0:
            lambda_init = 0.8 - 0.6 * math.exp(-0.3 * i)
            qkz, h = _norm_proj(x2, seq, norm_pre[i], mod3, diff_w_in.astype(BF16), qkz_blocks,
                                _qkz_scale(DIFF_QK_DIM, branch), emit_h=True)
            v_t = _proj_t(diff_w_in[:, 2 * branch:3 * branch].T.astype(BF16), h)
            o = _diff_attention(qkz, v_t, batch, seq, diff_lambda_q1, diff_lambda_k1, diff_lambda_q2,
                                diff_lambda_k2, diff_subln, lambda_init)
            w_out = diff_w_out
        elif kind == 1:
            qkz, h = _norm_proj(x2, seq, norm_pre[i], mod3, band_w_in.astype(BF16), qkz_blocks,
                                _qkz_scale(BAND_HEAD_DIM, branch), emit_h=True)
            v_t = _proj_t(band_w_in[:, 2 * branch:3 * branch].T.astype(BF16), h)
            o = _band_attention(qkz, v_t, _band_bias(band_rel_bias), batch, seq)
            w_out = band_w_out
        elif kind == 2:
            wf = jnp.pad(fox_w_in[:, 4 * branch:], ((0, 0), (0, LANES - FOX_HEADS))).astype(BF16)
            qkz, h, f_logit = _norm_proj(x2, seq, norm_pre[i], mod3, fox_w_in.astype(BF16), qkz_blocks,
                                         _qkz_scale(FOX_HEAD_DIM, branch), wf=wf, emit_h=True)
            v_t = _proj_t(fox_w_in[:, 2 * branch:3 * branch].T.astype(BF16), h)
            f_cols, f_rows = _fox_gates(f_logit, fox_forget_bias, batch, seq)
            o = _fox_attention(qkz, v_t, f_cols, f_rows, batch, seq)
            w_out = fox_w_out
        else:
            uz, = _norm_proj(x2, seq, norm_pre[i], mod3, pool_w_in.astype(BF16), list(range(2 * nblk)),
                             jnp.ones((2 * branch,), F32), out_dtype=F32)
            o = _pool_mix(uz, seq, pool_group_w.astype(BF16), pool_scale)
            w_out = pool_w_out
        x2 = _out_proj_residual(o, w_out.astype(BF16), x2, seq, mod3, norm_post[i])
    return x2.reshape(batch, seq, d)
```

```python
import functools
import math

import jax
import jax.numpy as jnp
from jax import lax
from jax.experimental import pallas as pl
from jax.experimental.pallas import tpu as pltpu

F32 = jnp.float32
BF16 = jnp.bfloat16

D_MODEL = 2048
EPS = 1e-6
NEG_INF = -1e30
LOG2E = math.log2(math.e)
CHUNK = 64
DIFF_HEADS, DIFF_V_DIM, DIFF_QK_DIM = 8, 256, 128
BAND_HEADS, BAND_HEAD_DIM, BAND_PAST, REL_CLIP = 32, 64, 512, 256
FOX_HEADS, FOX_HEAD_DIM = 16, 128
POOL_WINDOWS = (2, 4, 8, 16)
POOL_GROUP = D_MODEL // len(POOL_WINDOWS)
POOL_HALO = 16

LANES = 128
VMEM_LIMIT = 52 * 1024 * 1024

NT_DIMS = (((1,), (1,)), ((), ()))


def _silu(x):
    return x / (1.0 + jnp.exp(-x))


def _params(*sem):
    return pltpu.CompilerParams(dimension_semantics=sem, vmem_limit_bytes=VMEM_LIMIT)


def _ada_kernel(c_ref, w_ref, b_ref, o_ref):
    ca = _silu(c_ref[...]).astype(BF16)
    o_ref[0] = jnp.dot(ca, w_ref[0].astype(BF16), preferred_element_type=F32) + b_ref[0]


def _ada_mod(c, ada_w, ada_b):
    b, d = c.shape
    bp = -(-b // 8) * 8
    depth, _, n3 = ada_w.shape
    tn = 768
    cp = jnp.pad(c, ((0, bp - b), (0, 0)))
    return pl.pallas_call(
        _ada_kernel,
        name="ada_mod",
        out_shape=jax.ShapeDtypeStruct((depth, bp, n3), F32),
        grid=(depth, n3 // tn),
        in_specs=[pl.BlockSpec((bp, d), lambda l, j: (0, 0)),
                  pl.BlockSpec((1, d, tn), lambda l, j: (l, 0, j)),
                  pl.BlockSpec((1, 1, tn), lambda l, j: (l, 0, j))],
        out_specs=pl.BlockSpec((1, bp, tn), lambda l, j: (l, 0, j)),
        compiler_params=_params("parallel", "parallel"),
    )(cp, ada_w, ada_b.reshape(depth, 1, n3))


def _norm_proj_kernel(x_ref, g_ref, sh_ref, sc_ref, w_ref, cs_ref, *rest, has_f, emit_h, sub):
    rest = list(rest)
    wf_ref = rest.pop(0) if has_f else None
    o_ref = rest.pop(0)
    hout_ref = rest.pop(0) if emit_h else None
    f_ref = rest.pop(0) if has_f else None
    h_ref = rest.pop(0)

    @pl.when(pl.program_id(1) == 0)
    def _():
        g = g_ref[...]
        sc = 1.0 + sc_ref[0]
        sh = sh_ref[0]
        for r in range(x_ref.shape[0] // sub):
            x = x_ref[r * sub:(r + 1) * sub, :]
            y = x * lax.rsqrt(jnp.mean(x * x, axis=-1, keepdims=True) + EPS) * g
            h_ref[r * sub:(r + 1) * sub, :] = (y * sc + sh).astype(BF16)
        if emit_h:
            hout_ref[...] = h_ref[...]
        if has_f:
            f_ref[...] = jnp.dot(h_ref[...], wf_ref[...], preferred_element_type=F32)

    y = jnp.dot(h_ref[...], w_ref[...], preferred_element_type=F32)
    o_ref[...] = (y * cs_ref[...]).astype(o_ref.dtype)


def _norm_proj(x2, seq, g, mod3, w, col_blocks, col_scale, wf=None, emit_h=False, out_dtype=BF16,
               tm=512, tn=1024):
    t, d = x2.shape
    tm = min(tm, seq)
    nb = seq // tm
    nj = len(col_blocks)
    n = nj * tn
    first, skip_from, skip = col_blocks[0], None, 0
    for a, b2 in zip(col_blocks, col_blocks[1:]):
        if b2 != a + 1:
            skip_from, skip = a - first + 1, b2 - a - 1
    if skip_from is None:
        wmap = lambda i, j: (0, j + first)
    else:
        wmap = lambda i, j: (0, j + first + jnp.where(j >= skip_from, skip, 0))
    in_specs = [pl.BlockSpec((tm, d), lambda i, j: (i, 0)),
                pl.BlockSpec((1, d), lambda i, j: (0, 0)),
                pl.BlockSpec((1, 1, d), lambda i, j: (i // nb, 0, 0)),
                pl.BlockSpec((1, 1, d), lambda i, j: (i // nb, 0, 1)),
                pl.BlockSpec((d, tn), wmap),
                pl.BlockSpec((1, tn), lambda i, j: (0, j))]
    out_shape = [jax.ShapeDtypeStruct((t, n), out_dtype)]
    out_specs = [pl.BlockSpec((tm, tn), lambda i, j: (i, j))]
    args = [x2, g.reshape(1, d), mod3, mod3, w, col_scale.reshape(1, n).astype(F32)]
    if wf is not None:
        in_specs.append(pl.BlockSpec((d, LANES), lambda i, j: (0, 0)))
        args.append(wf)
    if emit_h:
        out_shape.append(jax.ShapeDtypeStruct((t, d), BF16))
        out_specs.append(pl.BlockSpec((tm, d), lambda i, j: (i, 0)))
    if wf is not None:
        out_shape.append(jax.ShapeDtypeStruct((t, LANES), F32))
        out_specs.append(pl.BlockSpec((tm, LANES), lambda i, j: (i, 0)))
    return pl.pallas_call(
        functools.partial(_norm_proj_kernel, has_f=wf is not None, emit_h=emit_h, sub=min(256, tm)),
        name="norm_proj",
        out_shape=out_shape,
        grid=(t // tm, nj),
        in_specs=in_specs,
        out_specs=out_specs,
        scratch_shapes=[pltpu.VMEM((tm, d), BF16)],
        compiler_params=_params("parallel", "arbitrary"),
    )(*args)


def _proj_t_kernel(w_ref, h_ref, o_ref):
    o_ref[...] = lax.dot_general(w_ref[...], h_ref[...], NT_DIMS,
                                 preferred_element_type=F32).astype(o_ref.dtype)


def _proj_t(w_t, h, tn=512, tm=1024):
    n, d = w_t.shape
    t = h.shape[0]
    tm = min(tm, t)
    return pl.pallas_call(
        _proj_t_kernel,
        name="proj_t",
        out_shape=jax.ShapeDtypeStruct((n, t), BF16),
        grid=(t // tm, n // tn),
        in_specs=[pl.BlockSpec((tn, d), lambda i, j: (j, 0)),
                  pl.BlockSpec((tm, d), lambda i, j: (i, 0))],
        out_specs=pl.BlockSpec((tn, tm), lambda i, j: (j, i)),
        compiler_params=_params("parallel", "arbitrary"),
    )(w_t, h)


def _out_proj_kernel(o_ref, w_ref, x_ref, gate_ref, g_ref, out_ref):
    y = jnp.dot(o_ref[...], w_ref[...], preferred_element_type=F32)
    yn = y * lax.rsqrt(jnp.mean(y * y, axis=-1, keepdims=True) + EPS) * g_ref[...]
    out_ref[...] = x_ref[...] + gate_ref[0] * yn


def _out_proj_residual(o2, w, x2, seq, mod3, g, tm=256):
    t, d = x2.shape
    tm = min(tm, seq)
    nb = seq // tm
    return pl.pallas_call(
        _out_proj_kernel,
        name="out_proj",
        out_shape=jax.ShapeDtypeStruct((t, d), F32),
        grid=(t // tm,),
        in_specs=[pl.BlockSpec((tm, d), lambda i: (i, 0)),
                  pl.BlockSpec((d, d), lambda i: (0, 0)),
                  pl.BlockSpec((tm, d), lambda i: (i, 0)),
                  pl.BlockSpec((1, 1, d), lambda i: (i // nb, 0, 2)),
                  pl.BlockSpec((1, d), lambda i: (0, 0))],
        out_specs=pl.BlockSpec((tm, d), lambda i: (i, 0)),
        compiler_params=_params("parallel"),
    )(o2, w, x2, mod3, g.reshape(1, d))


def _flash_step_t(s_t, s_max, v_t, m_ref, l_ref, acc_ref, idx, cols):
    m_prev = m_ref[idx, :, cols]
    m_new = jnp.maximum(m_prev, s_max)
    alpha = jnp.exp2(m_prev - m_new)
    p_t = jnp.exp2(s_t - m_new)
    l_ref[idx, :, cols] = alpha * l_ref[idx, :, cols] + jnp.sum(p_t, axis=0, keepdims=True)
    acc_ref[idx, :, cols] = (alpha * acc_ref[idx, :, cols]
                             + jnp.dot(v_t, p_t.astype(BF16), preferred_element_type=F32))
    m_ref[idx, :, cols] = m_new


def _flash_init(m_ref, l_ref, acc_ref):
    m_ref[...] = jnp.full(m_ref.shape, NEG_INF, F32)
    l_ref[...] = jnp.zeros(l_ref.shape, F32)
    acc_ref[...] = jnp.zeros(acc_ref.shape, F32)


ATTN_TK = 512


def _causal_sweep(qi, scores, update, buf0, buf1):
    every, late = slice(None), slice(ATTN_TK, 2 * ATTN_TK)
    scores(0, buf0, every)

    def pair(jj, carry):
        j = 2 * jj
        scores(j + 1, buf1, every)
        update(j, buf0, False, every)
        scores(j + 2, buf0, every)
        update(j + 1, buf1, False, every)
        return carry

    lax.fori_loop(0, qi, pair, 0)
    scores(2 * qi + 1, buf1, late)
    update(2 * qi, buf0, True, every)
    update(2 * qi + 1, buf1, True, late)


def _diff_attn_kernel(q_ref, k_ref, vt_ref, z_ref, lq1_ref, lk1_ref, lq2_ref, lk2_ref, g_ref,
                      o_ref, m_ref, l_ref, acc_ref, s0_ref, s1_ref, mx0_ref, mx1_ref, *, lambda_init):
    qi = pl.program_id(2)
    _flash_init(m_ref, l_ref, acc_ref)
    q = q_ref[...]

    def scores(ki, buf, cols):
        s_ref, mx_ref = buf
        k = k_ref[pl.ds(pl.multiple_of(ki * ATTN_TK, ATTN_TK), ATTN_TK), :]
        for m in range(2):
            sl = slice(m * DIFF_QK_DIM, (m + 1) * DIFF_QK_DIM)
            s_t = lax.dot_general(k[:, sl], q[cols, sl], NT_DIMS, preferred_element_type=F32)
            s_ref[m, :, cols] = s_t
            mx_ref[m, :, cols] = jnp.max(s_t, axis=0, keepdims=True)

    def update(ki, buf, masked, cols):
        s_ref, mx_ref = buf
        v_t = vt_ref[:, pl.ds(pl.multiple_of(ki * ATTN_TK, ATTN_TK), ATTN_TK)]
        for m in range(2):
            s_t = s_ref[m, :, cols]
            s_max = mx_ref[m, :, cols]
            if masked:
                kc = lax.broadcasted_iota(jnp.int32, s_t.shape, 0) // CHUNK
                qc = lax.broadcasted_iota(jnp.int32, s_t.shape, 1) // CHUNK
                s_t = jnp.where(kc <= qc, s_t, NEG_INF)
                s_max = jnp.max(s_t, axis=0, keepdims=True)
            _flash_step_t(s_t, s_max, v_t, m_ref, l_ref, acc_ref, m, cols)

    _causal_sweep(qi, scores, update, (s0_ref, mx0_ref), (s1_ref, mx1_ref))

    lam = (jnp.exp(jnp.sum(lq1_ref[...] * lk1_ref[...], axis=-1, keepdims=True))
           - jnp.exp(jnp.sum(lq2_ref[...] * lk2_ref[...], axis=-1, keepdims=True)) + lambda_init)
    o_t = acc_ref[0] / l_ref[0] - lam * (acc_ref[1] / l_ref[1])
    o = o_t.T
    on = o * lax.rsqrt(jnp.mean(o * o, axis=-1, keepdims=True) + EPS) * g_ref[...]
    on = on * (1.0 - lambda_init)
    o_ref[...] = (on * _silu(z_ref[...].astype(F32))).astype(o_ref.dtype)


def _diff_attention(qkz, v_t, batch, seq, lq1, lk1, lq2, lk2, subln_g, lambda_init):
    t = qkz.shape[0]
    tq = 2 * ATTN_TK
    nq = seq // tq
    h, dv = DIFF_HEADS, DIFF_V_DIM
    vec = lambda a: a.reshape(1, -1).astype(F32)
    small = lambda n: pl.BlockSpec((1, n), lambda b, hh, i: (0, 0))
    return pl.pallas_call(
        functools.partial(_diff_attn_kernel, lambda_init=lambda_init),
        name="diff_attn",
        out_shape=jax.ShapeDtypeStruct((t, h * dv), BF16),
        grid=(batch, h, nq),
        in_specs=[pl.BlockSpec((tq, dv), lambda b, hh, i: (b * nq + i, hh)),
                  pl.BlockSpec((seq, dv), lambda b, hh, i: (b, h + hh)),
                  pl.BlockSpec((dv, seq), lambda b, hh, i: (hh, b)),
                  pl.BlockSpec((tq, dv), lambda b, hh, i: (b * nq + i, 2 * h + hh)),
                  small(DIFF_QK_DIM), small(DIFF_QK_DIM), small(DIFF_QK_DIM), small(DIFF_QK_DIM),
                  small(dv)],
        out_specs=pl.BlockSpec((tq, dv), lambda b, hh, i: (b * nq + i, hh)),
        scratch_shapes=[pltpu.VMEM((2, 1, tq), F32), pltpu.VMEM((2, 1, tq), F32),
                        pltpu.VMEM((2, dv, tq), F32),
                        pltpu.VMEM((2, ATTN_TK, tq), F32), pltpu.VMEM((2, ATTN_TK, tq), F32),
                        pltpu.VMEM((2, 1, tq), F32), pltpu.VMEM((2, 1, tq), F32)],
        compiler_params=_params("parallel", "parallel", "arbitrary"),
    )(qkz, qkz, v_t, qkz, vec(lq1), vec(lk1), vec(lq2), vec(lk2), vec(subln_g))


FOX_PIECES = 3


def _fox_gate_kernel(f_ref, b_ref, o_ref):
    x = f_ref[...] + b_ref[...]
    lf = jnp.minimum(x, 0.0) - jnp.log1p(jnp.exp(-jnp.abs(x)))
    n = lf.shape[0]
    row = lax.broadcasted_iota(jnp.int32, lf.shape, 0)
    d = 1
    while d < n:
        lf = lf + jnp.where(row >= d, pltpu.roll(lf, d, 0), 0.0)
        d *= 2
    rest = lf * LOG2E
    lane = lax.broadcasted_iota(jnp.int32, lf.shape, 1)
    out = jnp.zeros(lf.shape, F32)
    for p in range(FOX_PIECES):
        piece = rest.astype(BF16).astype(F32)
        rest = rest - piece
        moved = piece if p == 0 else pltpu.roll(piece, p * FOX_HEADS, 1)
        out = jnp.where((lane >= p * FOX_HEADS) & (lane < (p + 1) * FOX_HEADS), moved, out)
    o_ref[...] = out.astype(BF16)


def _fox_gates(f_logit, bias, batch, seq):
    bpad = jnp.zeros((1, LANES), F32).at[0, :FOX_HEADS].set(bias.astype(F32))
    return pl.pallas_call(
        _fox_gate_kernel,
        name="fox_gates",
        out_shape=jax.ShapeDtypeStruct((batch * seq, LANES), BF16),
        grid=(batch,),
        in_specs=[pl.BlockSpec((seq, LANES), lambda b: (b, 0)),
                  pl.BlockSpec((1, LANES), lambda b: (0, 0))],
        out_specs=pl.BlockSpec((seq, LANES), lambda b: (b, 0)),
        compiler_params=_params("parallel"),
    )(f_logit, bpad)


def _fox_place_kernel(f_ref, pk_ref, pq_ref, ck_ref, cq_ref, kx_ref, qx_ref):
    f = f_ref[...]
    kx_ref[...] = (jnp.dot(f, pk_ref[...], preferred_element_type=F32) + ck_ref[...]).astype(BF16)
    qx_ref[...] = (jnp.dot(f, pq_ref[...], preferred_element_type=F32) + cq_ref[...]).astype(BF16)


def _fox_extend(pieces, tm=1024):
    t = pieces.shape[0]
    tm = min(tm, t)
    n = FOX_HEADS * LANES
    src = jnp.arange(LANES)[:, None]
    dst = jnp.arange(n)[None, :]
    p_src, h_src = src // FOX_HEADS, src % FOX_HEADS
    h_dst, c_dst = dst // LANES, dst % LANES
    live = (p_src < FOX_PIECES) & (h_src == h_dst)
    pk = jnp.where(live & (c_dst == p_src), -1.0, 0.0).astype(BF16)
    pq = jnp.where(live & (c_dst == p_src + FOX_PIECES), 1.0, 0.0).astype(BF16)
    ck = ((c_dst >= FOX_PIECES) & (c_dst < 2 * FOX_PIECES)).astype(F32)
    cq = (c_dst < FOX_PIECES).astype(F32)
    mat = pl.BlockSpec((LANES, n), lambda i: (0, 0))
    vec = pl.BlockSpec((1, n), lambda i: (0, 0))
    return pl.pallas_call(
        _fox_place_kernel,
        name="fox_extend",
        out_shape=[jax.ShapeDtypeStruct((t, n), BF16), jax.ShapeDtypeStruct((t, n), BF16)],
        grid=(t // tm,),
        in_specs=[pl.BlockSpec((tm, LANES), lambda i: (i, 0)), mat, mat, vec, vec],
        out_specs=[pl.BlockSpec((tm, n), lambda i: (i, 0)), pl.BlockSpec((tm, n), lambda i: (i, 0))],
        compiler_params=_params("parallel"),
    )(pieces, pk, pq, ck, cq)


def _fox_attn_kernel(q_ref, qx_ref, k_ref, kx_ref, vt_ref, z_ref, o_ref, m_ref, l_ref, acc_ref,
                     s0_ref, s1_ref, mx0_ref, mx1_ref):
    qi = pl.program_id(2)
    _flash_init(m_ref, l_ref, acc_ref)
    q = jnp.concatenate([q_ref[...], qx_ref[...]], axis=1)

    def scores(ki, buf, cols):
        s_ref, mx_ref = buf
        off = pl.multiple_of(ki * ATTN_TK, ATTN_TK)
        k = jnp.concatenate([k_ref[pl.ds(off, ATTN_TK), :], kx_ref[pl.ds(off, ATTN_TK), :]], axis=1)
        s_t = lax.dot_general(k, q[cols], NT_DIMS, preferred_element_type=F32)
        s_ref[:, cols] = s_t
        mx_ref[:, cols] = jnp.max(s_t, axis=0, keepdims=True)

    def update(ki, buf, masked, cols):
        s_ref, mx_ref = buf
        v_t = vt_ref[:, pl.ds(pl.multiple_of(ki * ATTN_TK, ATTN_TK), ATTN_TK)]
        s_t = s_ref[:, cols]
        s_max = mx_ref[:, cols]
        if masked:
            kp = lax.broadcasted_iota(jnp.int32, s_t.shape, 0)
            qp = lax.broadcasted_iota(jnp.int32, s_t.shape, 1)
            s_t = jnp.where(kp <= qp, s_t, NEG_INF)
            s_max = jnp.max(s_t, axis=0, keepdims=True)
        _flash_step_t(s_t, s_max, v_t, m_ref, l_ref, acc_ref, 0, cols)

    _causal_sweep(qi, scores, update, (s0_ref, mx0_ref), (s1_ref, mx1_ref))
    o = (acc_ref[0] / l_ref[0]).T
    o_ref[...] = (o * _silu(z_ref[...].astype(F32))).astype(o_ref.dtype)


def _fox_attention(qkz, v_t, kx, qx, batch, seq):
    t = qkz.shape[0]
    tq = 2 * ATTN_TK
    nq = seq // tq
    h, dh = FOX_HEADS, FOX_HEAD_DIM
    return pl.pallas_call(
        _fox_attn_kernel,
        name="fox_attn",
        out_shape=jax.ShapeDtypeStruct((t, h * dh), BF16),
        grid=(batch, h, nq),
        in_specs=[pl.BlockSpec((tq, dh), lambda b, hh, i: (b * nq + i, hh)),
                  pl.BlockSpec((tq, LANES), lambda b, hh, i: (b * nq + i, hh)),
                  pl.BlockSpec((seq, dh), lambda b, hh, i: (b, h + hh)),
                  pl.BlockSpec((seq, LANES), lambda b, hh, i: (b, hh)),
                  pl.BlockSpec((dh, seq), lambda b, hh, i: (hh, b)),
                  pl.BlockSpec((tq, dh), lambda b, hh, i: (b * nq + i, 2 * h + hh))],
        out_specs=pl.BlockSpec((tq, dh), lambda b, hh, i: (b * nq + i, hh)),
        scratch_shapes=[pltpu.VMEM((1, 1, tq), F32), pltpu.VMEM((1, 1, tq), F32),
                        pltpu.VMEM((1, dh, tq), F32),
                        pltpu.VMEM((ATTN_TK, tq), F32), pltpu.VMEM((ATTN_TK, tq), F32),
                        pltpu.VMEM((1, tq), F32), pltpu.VMEM((1, tq), F32)],
        compiler_params=_params("parallel", "parallel", "arbitrary"),
    )(qkz, qx, qkz, kx, v_t, qkz)


BAND_TQ = 256
BAND_WIN = BAND_PAST + BAND_TQ
BAND_HPB = LANES // BAND_HEAD_DIM
BAND_REV = 1024


def _band_bias_kernel(row_ref, o_ref):
    x = jnp.broadcast_to(row_ref[0], (BAND_WIN, BAND_REV))
    x = pltpu.roll(x, BAND_REV - BAND_WIN, 1, stride=1, stride_axis=0)
    x = x[:, :BAND_TQ]
    kc = lax.broadcasted_iota(jnp.int32, x.shape, 0) // CHUNK
    qc = lax.broadcasted_iota(jnp.int32, x.shape, 1) // CHUNK
    band = (kc >= qc) & (kc <= qc + BAND_PAST // CHUNK)
    o_ref[0] = jnp.where(band, x * LOG2E, NEG_INF)


def _band_bias(rel_table):
    nh = rel_table.shape[0]
    t = rel_table.astype(F32)
    row = jnp.concatenate([t, jnp.broadcast_to(t[:, 2 * REL_CLIP:], (nh, BAND_REV - t.shape[1]))], axis=1)
    return pl.pallas_call(
        _band_bias_kernel,
        name="band_bias",
        out_shape=jax.ShapeDtypeStruct((nh, BAND_WIN, BAND_TQ), F32),
        grid=(nh,),
        in_specs=[pl.BlockSpec((1, 1, BAND_REV), lambda h: (h, 0, 0))],
        out_specs=pl.BlockSpec((1, BAND_WIN, BAND_TQ), lambda h: (h, 0, 0)),
        compiler_params=_params("parallel"),
    )(row.reshape(nh, 1, BAND_REV))


def _band_attn_kernel(q_ref, k_ref, vt_ref, z_ref, bias_ref, o_ref, kp_ref, vtp_ref, s0_ref, s1_ref,
                      mx0_ref, mx1_ref, *, nq):
    hd = BAND_HEAD_DIM
    for hh in range(BAND_HPB):
        kp_ref[hh, :BAND_PAST, :] = jnp.zeros((BAND_PAST, hd), BF16)
        kp_ref[hh, BAND_PAST:, :] = k_ref[:, hh * hd:(hh + 1) * hd]
    vtp_ref[:, :BAND_PAST] = jnp.zeros((LANES, BAND_PAST), BF16)
    vtp_ref[:, BAND_PAST:] = vt_ref[...]

    def scores(j, buf):
        s_ref, mx_ref = buf
        start = pl.multiple_of(j * BAND_TQ, BAND_TQ)
        q = q_ref[pl.ds(start, BAND_TQ), :]
        for hh in range(BAND_HPB):
            s_t = lax.dot_general(kp_ref[hh, pl.ds(start, BAND_WIN), :], q[:, hh * hd:(hh + 1) * hd],
                                  NT_DIMS, preferred_element_type=F32) + bias_ref[hh]
            s_ref[hh] = s_t
            mx_ref[hh] = jnp.max(s_t, axis=0, keepdims=True)

    def update(j, buf, masked):
        s_ref, mx_ref = buf
        start = pl.multiple_of(j * BAND_TQ, BAND_TQ)
        outs = []
        for hh in range(BAND_HPB):
            s_t = s_ref[hh]
            s_max = mx_ref[hh]
            if masked:
                kpos = start - BAND_PAST + lax.broadcasted_iota(jnp.int32, (BAND_WIN, 1), 0)
                s_t = jnp.where(kpos >= 0, s_t, NEG_INF)
                s_max = jnp.max(s_t, axis=0, keepdims=True)
            p_t = jnp.exp2(s_t - s_max)
            l = jnp.sum(p_t, axis=0, keepdims=True)
            v_t = vtp_ref[hh * hd:(hh + 1) * hd, pl.ds(start, BAND_WIN)]
            outs.append(jnp.dot(v_t, p_t.astype(BF16), preferred_element_type=F32) / l)
        o = jnp.concatenate(outs, axis=0).T
        z = z_ref[pl.ds(start, BAND_TQ), :].astype(F32)
        o_ref[pl.ds(start, BAND_TQ), :] = (o * _silu(z)).astype(o_ref.dtype)

    buf0, buf1 = (s0_ref, mx0_ref), (s1_ref, mx1_ref)
    scores(0, buf0)
    scores(1, buf1)
    update(0, buf0, True)
    scores(2, buf0)
    update(1, buf1, True)

    def pair(jj, carry):
        j = 2 + 2 * jj
        scores(j + 1, buf1)
        update(j, buf0, False)
        scores(j + 2, buf0)
        update(j + 1, buf1, False)
        return carry

    lax.fori_loop(0, (nq - 4) // 2, pair, 0)
    scores(nq - 1, buf1)
    update(nq - 2, buf0, False)
    update(nq - 1, buf1, False)


def _band_attention(qkz, v_t, bias_t, batch, seq):
    t = qkz.shape[0]
    nq = seq // BAND_TQ
    assert nq >= 4 and nq % 2 == 0
    hp = BAND_HEADS // BAND_HPB
    return pl.pallas_call(
        functools.partial(_band_attn_kernel, nq=nq),
        name="band_attn",
        out_shape=jax.ShapeDtypeStruct((t, BAND_HEADS * BAND_HEAD_DIM), BF16),
        grid=(batch, hp),
        in_specs=[pl.BlockSpec((seq, LANES), lambda b, g: (b, g)),
                  pl.BlockSpec((seq, LANES), lambda b, g: (b, hp + g)),
                  pl.BlockSpec((LANES, seq), lambda b, g: (g, b)),
                  pl.BlockSpec((seq, LANES), lambda b, g: (b, 2 * hp + g)),
                  pl.BlockSpec((BAND_HPB, BAND_WIN, BAND_TQ), lambda b, g: (g, 0, 0))],
        out_specs=pl.BlockSpec((seq, LANES), lambda b, g: (b, g)),
        scratch_shapes=[pltpu.VMEM((BAND_HPB, seq + BAND_PAST, BAND_HEAD_DIM), BF16),
                        pltpu.VMEM((LANES, seq + BAND_PAST), BF16),
                        pltpu.VMEM((BAND_HPB, BAND_WIN, BAND_TQ), F32),
                        pltpu.VMEM((BAND_HPB, BAND_WIN, BAND_TQ), F32),
                        pltpu.VMEM((BAND_HPB, 1, BAND_TQ), F32),
                        pltpu.VMEM((BAND_HPB, 1, BAND_TQ), F32)],
        compiler_params=_params("parallel", "parallel"),
    )(qkz, qkz, v_t, qkz, bias_t)


def _pool_kernel(u_ref, up_ref, z_ref, w_ref, ps_ref, o_ref, *, tm, nb):
    i = pl.program_id(0)
    t0 = (i % nb) * tm
    has_history = t0 > 0
    t = t0 + lax.broadcasted_iota(jnp.int32, (tm, 1), 0)
    for g, win in enumerate(POOL_WINDOWS):
        sl = slice(g * POOL_GROUP, (g + 1) * POOL_GROUP)
        u = u_ref[:, sl]
        acc = jnp.concatenate([jnp.where(has_history, up_ref[:, sl], 0.0), u], axis=0)
        d = 1
        while d < win:
            acc = acc + pltpu.roll(acc, d, 0)
            d *= 2
        wsum = acc[POOL_HALO:, :]
        cnt = jnp.minimum(t + 1, win).astype(F32)
        delta = (wsum / cnt - u).astype(BF16)
        y = jnp.dot(delta, w_ref[g], preferred_element_type=F32) * ps_ref[:, sl]
        o_ref[:, sl] = (y * _silu(z_ref[:, sl])).astype(o_ref.dtype)


def _pool_mix(uz, seq, pool_w, pool_scale, tm=256):
    t = uz.shape[0]
    d = D_MODEL
    tm = min(tm, seq)
    nb = seq // tm
    hb = tm // POOL_HALO
    return pl.pallas_call(
        functools.partial(_pool_kernel, tm=tm, nb=nb),
        name="pool_mix",
        out_shape=jax.ShapeDtypeStruct((t, d), BF16),
        grid=(t // tm,),
        in_specs=[pl.BlockSpec((tm, d), lambda i: (i, 0)),
                  pl.BlockSpec((POOL_HALO, d), lambda i: (jnp.maximum(i * hb - 1, 0), 0)),
                  pl.BlockSpec((tm, d), lambda i: (i, 1)),
                  pl.BlockSpec((len(POOL_WINDOWS), POOL_GROUP, POOL_GROUP), lambda i: (0, 0, 0)),
                  pl.BlockSpec((1, d), lambda i: (0, 0))],
        out_specs=pl.BlockSpec((tm, d), lambda i: (i, 0)),
        compiler_params=_params("parallel"),
    )(uz, uz, uz, pool_w, pool_scale.reshape(1, d).astype(F32))


PROJ_TN = 1024


def _qkz_scale(qk_dim, branch):
    return jnp.concatenate([jnp.full((branch,), qk_dim ** -0.5 * LOG2E, F32), jnp.ones((2 * branch,), F32)])


def kernel(x, c, ada_w, ada_b, norm_pre, norm_post, diff_w_in, diff_w_out, diff_lambda_q1, diff_lambda_k1, diff_lambda_q2, diff_lambda_k2, diff_subln, band_w_in, band_w_out, band_rel_bias, fox_w_in, fox_w_out, fox_forget_bias, pool_w_in, pool_w_out, pool_group_w, pool_scale):
    batch, seq, d = x.shape
    depth = ada_w.shape[0]
    branch = d
    nblk = branch // PROJ_TN
    qkz_blocks = list(range(2 * nblk)) + list(range(3 * nblk, 4 * nblk))
    mod = _ada_mod(c, ada_w, ada_b)
    x2 = x.reshape(batch * seq, d)
    for i in range(depth):
        mod3 = mod[i].reshape(mod.shape[1], 1, 3 * d)
        kind = i % 4
        if kind == 0:
            lambda_init = 0.8 - 0.6 * math.exp(-0.3 * i)
            qkz, h = _norm_proj(x2, seq, norm_pre[i], mod3, diff_w_in.astype(BF16), qkz_blocks,
                                _qkz_scale(DIFF_QK_DIM, branch), emit_h=True)
            v_t = _proj_t(diff_w_in[:, 2 * branch:3 * branch].T.astype(BF16), h)
            o = _diff_attention(qkz, v_t, batch, seq, diff_lambda_q1, diff_lambda_k1, diff_lambda_q2,
                                diff_lambda_k2, diff_subln, lambda_init)
            w_out = diff_w_out
        elif kind == 1:
            qkz, h = _norm_proj(x2, seq, norm_pre[i], mod3, band_w_in.astype(BF16), qkz_blocks,
                                _qkz_scale(BAND_HEAD_DIM, branch), emit_h=True)
            v_t = _proj_t(band_w_in[:, 2 * branch:3 * branch].T.astype(BF16), h)
            o = _band_attention(qkz, v_t, _band_bias(band_rel_bias), batch, seq)
            w_out = band_w_out
        elif kind == 2:
            wf = jnp.pad(fox_w_in[:, 4 * branch:], ((0, 0), (0, LANES - FOX_HEADS))).astype(BF16)
            qkz, h, f_logit = _norm_proj(x2, seq, norm_pre[i], mod3, fox_w_in.astype(BF16), qkz_blocks,
                                         _qkz_scale(FOX_HEAD_DIM, branch), wf=wf, emit_h=True)
            v_t = _proj_t(fox_w_in[:, 2 * branch:3 * branch].T.astype(BF16), h)
            kx, qx = _fox_extend(_fox_gates(f_logit, fox_forget_bias, batch, seq))
            o = _fox_attention(qkz, v_t, kx, qx, batch, seq)
            w_out = fox_w_out
        else:
            uz, = _norm_proj(x2, seq, norm_pre[i], mod3, pool_w_in.astype(BF16), list(range(2 * nblk)),
                             jnp.ones((2 * branch,), F32), out_dtype=F32)
            o = _pool_mix(uz, seq, pool_group_w.astype(BF16), pool_scale)
            w_out = pool_w_out
        x2 = _out_proj_residual(o, w_out.astype(BF16), x2, seq, mod3, norm_post[i])
    return x2.reshape(batch, seq, d)
```

```python
import functools
import math

import jax
import jax.numpy as jnp
from jax import lax
from jax.experimental import pallas as pl
from jax.experimental.pallas import tpu as pltpu

F32 = jnp.float32
BF16 = jnp.bfloat16

D_MODEL = 2048
EPS = 1e-6
NEG_INF = -1e30
LOG2E = math.log2(math.e)
CHUNK = 64
DIFF_HEADS, DIFF_V_DIM, DIFF_QK_DIM = 8, 256, 128
BAND_HEADS, BAND_HEAD_DIM, BAND_PAST, REL_CLIP = 32, 64, 512, 256
FOX_HEADS, FOX_HEAD_DIM = 16, 128
POOL_WINDOWS = (2, 4, 8, 16)
POOL_GROUP = D_MODEL // len(POOL_WINDOWS)
POOL_HALO = 16

LANES = 128
VMEM_LIMIT = 52 * 1024 * 1024

NT_DIMS = (((1,), (1,)), ((), ()))


def _silu(x):
    return x / (1.0 + jnp.exp(-x))


def _params(*sem):
    return pltpu.CompilerParams(dimension_semantics=sem, vmem_limit_bytes=VMEM_LIMIT)


def _ada_kernel(c_ref, w_ref, b_ref, o_ref):
    ca = _silu(c_ref[...]).astype(BF16)
    o_ref[0] = jnp.dot(ca, w_ref[0].astype(BF16), preferred_element_type=F32) + b_ref[0]


def _ada_mod(c, ada_w, ada_b):
    b, d = c.shape
    bp = -(-b // 8) * 8
    depth, _, n3 = ada_w.shape
    tn = 768
    cp = jnp.pad(c, ((0, bp - b), (0, 0)))
    return pl.pallas_call(
        _ada_kernel,
        name="ada_mod",
        out_shape=jax.ShapeDtypeStruct((depth, bp, n3), F32),
        grid=(depth, n3 // tn),
        in_specs=[pl.BlockSpec((bp, d), lambda l, j: (0, 0)),
                  pl.BlockSpec((1, d, tn), lambda l, j: (l, 0, j)),
                  pl.BlockSpec((1, 1, tn), lambda l, j: (l, 0, j))],
        out_specs=pl.BlockSpec((1, bp, tn), lambda l, j: (l, 0, j)),
        compiler_params=_params("parallel", "parallel"),
    )(cp, ada_w, ada_b.reshape(depth, 1, n3))


def _norm_proj_kernel(x_ref, g_ref, sh_ref, sc_ref, w_ref, cs_ref, *rest, has_f, emit_h, sub):
    rest = list(rest)
    wf_ref = rest.pop(0) if has_f else None
    o_ref = rest.pop(0)
    hout_ref = rest.pop(0) if emit_h else None
    f_ref = rest.pop(0) if has_f else None
    h_ref = rest.pop(0)

    @pl.when(pl.program_id(1) == 0)
    def _():
        g = g_ref[...]
        sc = 1.0 + sc_ref[0]
        sh = sh_ref[0]
        for r in range(x_ref.shape[0] // sub):
            rows = slice(r * sub, (r + 1) * sub)
            x = x_ref[rows, :]
            y = x * lax.rsqrt(jnp.mean(x * x, axis=-1, keepdims=True) + EPS) * g
            h = (y * sc + sh).astype(BF16)
            h_ref[rows, :] = h
            if emit_h:
                hout_ref[rows, :] = h
            if has_f:
                f_ref[rows, :] = jnp.dot(h, wf_ref[...], preferred_element_type=F32)
            y = jnp.dot(h, w_ref[...], preferred_element_type=F32)
            o_ref[rows, :] = (y * cs_ref[...]).astype(o_ref.dtype)

    @pl.when(pl.program_id(1) > 0)
    def _():
        y = jnp.dot(h_ref[...], w_ref[...], preferred_element_type=F32)
        o_ref[...] = (y * cs_ref[...]).astype(o_ref.dtype)


def _norm_proj(x2, seq, g, mod3, w, col_blocks, col_scale, wf=None, emit_h=False, out_dtype=BF16,
               tm=512, tn=1024):
    t, d = x2.shape
    tm = min(tm, seq)
    nb = seq // tm
    nj = len(col_blocks)
    n = nj * tn
    first, skip_from, skip = col_blocks[0], None, 0
    for a, b2 in zip(col_blocks, col_blocks[1:]):
        if b2 != a + 1:
            skip_from, skip = a - first + 1, b2 - a - 1
    if skip_from is None:
        wmap = lambda i, j: (0, j + first)
    else:
        wmap = lambda i, j: (0, j + first + jnp.where(j >= skip_from, skip, 0))
    in_specs = [pl.BlockSpec((tm, d), lambda i, j: (i, 0)),
                pl.BlockSpec((1, d), lambda i, j: (0, 0)),
                pl.BlockSpec((1, 1, d), lambda i, j: (i // nb, 0, 0)),
                pl.BlockSpec((1, 1, d), lambda i, j: (i // nb, 0, 1)),
                pl.BlockSpec((d, tn), wmap),
                pl.BlockSpec((1, tn), lambda i, j: (0, j))]
    out_shape = [jax.ShapeDtypeStruct((t, n), out_dtype)]
    out_specs = [pl.BlockSpec((tm, tn), lambda i, j: (i, j))]
    args = [x2, g.reshape(1, d), mod3, mod3, w, col_scale.reshape(1, n).astype(F32)]
    if wf is not None:
        in_specs.append(pl.BlockSpec((d, LANES), lambda i, j: (0, 0)))
        args.append(wf)
    if emit_h:
        out_shape.append(jax.ShapeDtypeStruct((t, d), BF16))
        out_specs.append(pl.BlockSpec((tm, d), lambda i, j: (i, 0)))
    if wf is not None:
        out_shape.append(jax.ShapeDtypeStruct((t, LANES), F32))
        out_specs.append(pl.BlockSpec((tm, LANES), lambda i, j: (i, 0)))
    return pl.pallas_call(
        functools.partial(_norm_proj_kernel, has_f=wf is not None, emit_h=emit_h, sub=min(256, tm)),
        name="norm_proj",
        out_shape=out_shape,
        grid=(t // tm, nj),
        in_specs=in_specs,
        out_specs=out_specs,
        scratch_shapes=[pltpu.VMEM((tm, d), BF16)],
        compiler_params=_params("parallel", "arbitrary"),
    )(*args)


def _proj_t_kernel(w_ref, h_ref, o_ref):
    o_ref[...] = lax.dot_general(w_ref[...], h_ref[...], NT_DIMS,
                                 preferred_element_type=F32).astype(o_ref.dtype)


def _proj_t(w_t, h, tn=512, tm=1024):
    n, d = w_t.shape
    t = h.shape[0]
    tm = min(tm, t)
    return pl.pallas_call(
        _proj_t_kernel,
        name="proj_t",
        out_shape=jax.ShapeDtypeStruct((n, t), BF16),
        grid=(t // tm, n // tn),
        in_specs=[pl.BlockSpec((tn, d), lambda i, j: (j, 0)),
                  pl.BlockSpec((tm, d), lambda i, j: (i, 0))],
        out_specs=pl.BlockSpec((tn, tm), lambda i, j: (j, i)),
        compiler_params=_params("parallel", "arbitrary"),
    )(w_t, h)


def _out_proj_kernel(o_ref, w_ref, x_ref, gate_ref, g_ref, out_ref, *, sub):
    for r in range(o_ref.shape[0] // sub):
        rows = slice(r * sub, (r + 1) * sub)
        y = jnp.dot(o_ref[rows, :], w_ref[...], preferred_element_type=F32)
        yn = y * lax.rsqrt(jnp.mean(y * y, axis=-1, keepdims=True) + EPS) * g_ref[...]
        out_ref[rows, :] = x_ref[rows, :] + gate_ref[0] * yn


def _out_proj_residual(o2, w, x2, seq, mod3, g, tm=512):
    t, d = x2.shape
    tm = min(tm, seq)
    nb = seq // tm
    return pl.pallas_call(
        functools.partial(_out_proj_kernel, sub=min(256, tm)),
        name="out_proj",
        out_shape=jax.ShapeDtypeStruct((t, d), F32),
        grid=(t // tm,),
        in_specs=[pl.BlockSpec((tm, d), lambda i: (i, 0)),
                  pl.BlockSpec((d, d), lambda i: (0, 0), pipeline_mode=pl.Buffered(1)),
                  pl.BlockSpec((tm, d), lambda i: (i, 0)),
                  pl.BlockSpec((1, 1, d), lambda i: (i // nb, 0, 2)),
                  pl.BlockSpec((1, d), lambda i: (0, 0))],
        out_specs=pl.BlockSpec((tm, d), lambda i: (i, 0)),
        compiler_params=_params("parallel"),
    )(o2, w, x2, mod3, g.reshape(1, d))


def _flash_step_t(s_t, s_max, v_t, m_ref, l_ref, acc_ref, idx, cols):
    m_prev = m_ref[idx, :, cols]
    m_new = jnp.maximum(m_prev, s_max)
    alpha = jnp.exp2(m_prev - m_new)
    p_t = jnp.exp2(s_t - m_new)
    l_ref[idx, :, cols] = alpha * l_ref[idx, :, cols] + jnp.sum(p_t, axis=0, keepdims=True)
    acc_ref[idx, :, cols] = (alpha * acc_ref[idx, :, cols]
                             + jnp.dot(v_t, p_t.astype(BF16), preferred_element_type=F32))
    m_ref[idx, :, cols] = m_new


def _flash_init(m_ref, l_ref, acc_ref):
    m_ref[...] = jnp.full(m_ref.shape, NEG_INF, F32)
    l_ref[...] = jnp.zeros(l_ref.shape, F32)
    acc_ref[...] = jnp.zeros(acc_ref.shape, F32)


ATTN_TK = 512


def _diag_masked(s_t, granule):
    kg = lax.broadcasted_iota(jnp.int32, s_t.shape, 0) // granule
    qg = lax.broadcasted_iota(jnp.int32, s_t.shape, 1) // granule
    return jnp.where(kg <= qg, s_t, NEG_INF)


def _causal_sweep(qi, scores, update, buf0, buf1):
    every, late = slice(None), slice(ATTN_TK, 2 * ATTN_TK)
    scores(0, buf0, every)

    def pair(jj, carry):
        j = 2 * jj
        scores(j + 1, buf1, every)
        update(j, buf0, False, every)
        scores(j + 2, buf0, every)
        update(j + 1, buf1, False, every)
        return carry

    lax.fori_loop(0, qi, pair, 0)
    scores(2 * qi + 1, buf1, late)
    update(2 * qi, buf0, True, every)
    update(2 * qi + 1, buf1, True, late)


def _diff_attn_kernel(q_ref, k_ref, vt_ref, z_ref, lq1_ref, lk1_ref, lq2_ref, lk2_ref, g_ref,
                      o_ref, m_ref, l_ref, acc_ref, s0_ref, s1_ref, mx0_ref, mx1_ref, *, lambda_init):
    qi = pl.program_id(2)
    _flash_init(m_ref, l_ref, acc_ref)
    q = q_ref[...]

    def scores(ki, buf, cols):
        s_ref, mx_ref = buf
        k = k_ref[pl.ds(pl.multiple_of(ki * ATTN_TK, ATTN_TK), ATTN_TK), :]
        for m in range(2):
            sl = slice(m * DIFF_QK_DIM, (m + 1) * DIFF_QK_DIM)
            s_t = lax.dot_general(k[:, sl], q[cols, sl], NT_DIMS, preferred_element_type=F32)
            s_ref[m, :, cols] = s_t
            mx_ref[m, :, cols] = jnp.max(s_t, axis=0, keepdims=True)

    def update(ki, buf, masked, cols):
        s_ref, mx_ref = buf
        v_t = vt_ref[:, pl.ds(pl.multiple_of(ki * ATTN_TK, ATTN_TK), ATTN_TK)]
        for m in range(2):
            s_t = s_ref[m, :, cols]
            s_max = mx_ref[m, :, cols]
            if masked:
                s_t = _diag_masked(s_t, CHUNK)
                s_max = jnp.max(s_t, axis=0, keepdims=True)
            _flash_step_t(s_t, s_max, v_t, m_ref, l_ref, acc_ref, m, cols)

    _causal_sweep(qi, scores, update, (s0_ref, mx0_ref), (s1_ref, mx1_ref))

    lam = (jnp.exp(jnp.sum(lq1_ref[...] * lk1_ref[...], axis=-1, keepdims=True))
           - jnp.exp(jnp.sum(lq2_ref[...] * lk2_ref[...], axis=-1, keepdims=True)) + lambda_init)
    o_t = acc_ref[0] * (1.0 / l_ref[0]) - acc_ref[1] * (lam / l_ref[1])
    o = o_t.T
    on = o * lax.rsqrt(jnp.mean(o * o, axis=-1, keepdims=True) + EPS) * g_ref[...]
    on = on * (1.0 - lambda_init)
    o_ref[...] = (on * _silu(z_ref[...].astype(F32))).astype(o_ref.dtype)


def _diff_attention(qkz, v_t, batch, seq, lq1, lk1, lq2, lk2, subln_g, lambda_init):
    t = qkz.shape[0]
    tq = 2 * ATTN_TK
    nq = seq // tq
    h, dv = DIFF_HEADS, DIFF_V_DIM
    vec = lambda a: a.reshape(1, -1).astype(F32)
    small = lambda n: pl.BlockSpec((1, n), lambda b, hh, i: (0, 0))
    return pl.pallas_call(
        functools.partial(_diff_attn_kernel, lambda_init=lambda_init),
        name="diff_attn",
        out_shape=jax.ShapeDtypeStruct((t, h * dv), BF16),
        grid=(batch, h, nq),
        in_specs=[pl.BlockSpec((tq, dv), lambda b, hh, i: (b * nq + i, hh)),
                  pl.BlockSpec((seq, dv), lambda b, hh, i: (b, h + hh)),
                  pl.BlockSpec((dv, seq), lambda b, hh, i: (hh, b)),
                  pl.BlockSpec((tq, dv), lambda b, hh, i: (b * nq + i, 2 * h + hh)),
                  small(DIFF_QK_DIM), small(DIFF_QK_DIM), small(DIFF_QK_DIM), small(DIFF_QK_DIM),
                  small(dv)],
        out_specs=pl.BlockSpec((tq, dv), lambda b, hh, i: (b * nq + i, hh)),
        scratch_shapes=[pltpu.VMEM((2, 1, tq), F32), pltpu.VMEM((2, 1, tq), F32),
                        pltpu.VMEM((2, dv, tq), F32),
                        pltpu.VMEM((2, ATTN_TK, tq), F32), pltpu.VMEM((2, ATTN_TK, tq), F32),
                        pltpu.VMEM((2, 1, tq), F32), pltpu.VMEM((2, 1, tq), F32)],
        compiler_params=_params("parallel", "parallel", "arbitrary"),
    )(qkz, qkz, v_t, qkz, vec(lq1), vec(lk1), vec(lq2), vec(lk2), vec(subln_g))


FOX_PIECES = 3


def _fox_gate_kernel(f_ref, b_ref, o_ref):
    x = f_ref[...] + b_ref[...]
    lf = jnp.minimum(x, 0.0) - jnp.log1p(jnp.exp(-jnp.abs(x)))
    n = lf.shape[0]
    row = lax.broadcasted_iota(jnp.int32, lf.shape, 0)
    d = 1
    while d < n:
        lf = lf + jnp.where(row >= d, pltpu.roll(lf, d, 0), 0.0)
        d *= 2
    rest = lf * LOG2E
    lane = lax.broadcasted_iota(jnp.int32, lf.shape, 1)
    out = jnp.zeros(lf.shape, F32)
    for p in range(FOX_PIECES):
        piece = rest.astype(BF16).astype(F32)
        rest = rest - piece
        moved = piece if p == 0 else pltpu.roll(piece, p * FOX_HEADS, 1)
        out = jnp.where((lane >= p * FOX_HEADS) & (lane < (p + 1) * FOX_HEADS), moved, out)
    o_ref[...] = out.astype(BF16)


def _fox_gates(f_logit, bias, batch, seq):
    bpad = jnp.zeros((1, LANES), F32).at[0, :FOX_HEADS].set(bias.astype(F32))
    return pl.pallas_call(
        _fox_gate_kernel,
        name="fox_gates",
        out_shape=jax.ShapeDtypeStruct((batch * seq, LANES), BF16),
        grid=(batch,),
        in_specs=[pl.BlockSpec((seq, LANES), lambda b: (b, 0)),
                  pl.BlockSpec((1, LANES), lambda b: (0, 0))],
        out_specs=pl.BlockSpec((seq, LANES), lambda b: (b, 0)),
        compiler_params=_params("parallel"),
    )(f_logit, bpad)


def _fox_place_kernel(f_ref, pk_ref, pq_ref, ck_ref, cq_ref, kx_ref, qx_ref):
    f = f_ref[...]
    kx_ref[...] = (jnp.dot(f, pk_ref[...], preferred_element_type=F32) + ck_ref[...]).astype(BF16)
    qx_ref[...] = (jnp.dot(f, pq_ref[...], preferred_element_type=F32) + cq_ref[...]).astype(BF16)


def _fox_extend(pieces, tm=1024):
    t = pieces.shape[0]
    tm = min(tm, t)
    n = FOX_HEADS * LANES
    src = jnp.arange(LANES)[:, None]
    dst = jnp.arange(n)[None, :]
    p_src, h_src = src // FOX_HEADS, src % FOX_HEADS
    h_dst, c_dst = dst // LANES, dst % LANES
    live = (p_src < FOX_PIECES) & (h_src == h_dst)
    pk = jnp.where(live & (c_dst == p_src), -1.0, 0.0).astype(BF16)
    pq = jnp.where(live & (c_dst == p_src + FOX_PIECES), 1.0, 0.0).astype(BF16)
    ck = ((c_dst >= FOX_PIECES) & (c_dst < 2 * FOX_PIECES)).astype(F32)
    cq = (c_dst < FOX_PIECES).astype(F32)
    mat = pl.BlockSpec((LANES, n), lambda i: (0, 0))
    vec = pl.BlockSpec((1, n), lambda i: (0, 0))
    return pl.pallas_call(
        _fox_place_kernel,
        name="fox_extend",
        out_shape=[jax.ShapeDtypeStruct((t, n), BF16), jax.ShapeDtypeStruct((t, n), BF16)],
        grid=(t // tm,),
        in_specs=[pl.BlockSpec((tm, LANES), lambda i: (i, 0)), mat, mat, vec, vec],
        out_specs=[pl.BlockSpec((tm, n), lambda i: (i, 0)), pl.BlockSpec((tm, n), lambda i: (i, 0))],
        compiler_params=_params("parallel"),
    )(pieces, pk, pq, ck, cq)


def _fox_attn_kernel(q_ref, qx_ref, k_ref, kx_ref, vt_ref, z_ref, o_ref, m_ref, l_ref, acc_ref,
                     s0_ref, s1_ref, mx0_ref, mx1_ref):
    qi = pl.program_id(2)
    _flash_init(m_ref, l_ref, acc_ref)
    q_t = jnp.concatenate([q_ref[...].T, qx_ref[...].T], axis=0)

    def scores(ki, buf, cols):
        s_ref, mx_ref = buf
        off = pl.multiple_of(ki * ATTN_TK, ATTN_TK)
        k = jnp.concatenate([k_ref[pl.ds(off, ATTN_TK), :], kx_ref[pl.ds(off, ATTN_TK), :]], axis=1)
        s_t = jnp.dot(k, q_t[:, cols], preferred_element_type=F32)
        s_ref[:, cols] = s_t
        mx_ref[:, cols] = jnp.max(s_t, axis=0, keepdims=True)

    def update(ki, buf, masked, cols):
        s_ref, mx_ref = buf
        v_t = vt_ref[:, pl.ds(pl.multiple_of(ki * ATTN_TK, ATTN_TK), ATTN_TK)]
        s_t = s_ref[:, cols]
        s_max = mx_ref[:, cols]
        if masked:
            s_t = _diag_masked(s_t, 1)
            s_max = jnp.max(s_t, axis=0, keepdims=True)
        _flash_step_t(s_t, s_max, v_t, m_ref, l_ref, acc_ref, 0, cols)

    _causal_sweep(qi, scores, update, (s0_ref, mx0_ref), (s1_ref, mx1_ref))
    o = (acc_ref[0] * (1.0 / l_ref[0])).T
    o_ref[...] = (o * _silu(z_ref[...].astype(F32))).astype(o_ref.dtype)


def _fox_attention(qkz, v_t, kx, qx, batch, seq):
    t = qkz.shape[0]
    tq = 2 * ATTN_TK
    nq = seq // tq
    h, dh = FOX_HEADS, FOX_HEAD_DIM
    return pl.pallas_call(
        _fox_attn_kernel,
        name="fox_attn",
        out_shape=jax.ShapeDtypeStruct((t, h * dh), BF16),
        grid=(batch, h, nq),
        in_specs=[pl.BlockSpec((tq, dh), lambda b, hh, i: (b * nq + i, hh)),
                  pl.BlockSpec((tq, LANES), lambda b, hh, i: (b * nq + i, hh)),
                  pl.BlockSpec((seq, dh), lambda b, hh, i: (b, h + hh)),
                  pl.BlockSpec((seq, LANES), lambda b, hh, i: (b, hh)),
                  pl.BlockSpec((dh, seq), lambda b, hh, i: (hh, b)),
                  pl.BlockSpec((tq, dh), lambda b, hh, i: (b * nq + i, 2 * h + hh))],
        out_specs=pl.BlockSpec((tq, dh), lambda b, hh, i: (b * nq + i, hh)),
        scratch_shapes=[pltpu.VMEM((1, 1, tq), F32), pltpu.VMEM((1, 1, tq), F32),
                        pltpu.VMEM((1, dh, tq), F32),
                        pltpu.VMEM((ATTN_TK, tq), F32), pltpu.VMEM((ATTN_TK, tq), F32),
                        pltpu.VMEM((1, tq), F32), pltpu.VMEM((1, tq), F32)],
        compiler_params=_params("parallel", "parallel", "arbitrary"),
    )(qkz, qx, qkz, kx, v_t, qkz)


BAND_TQ = 256
BAND_WIN = BAND_PAST + BAND_TQ
BAND_HPB = LANES // BAND_HEAD_DIM
BAND_REV = 1024


def _band_bias_kernel(row_ref, o_ref):
    x = jnp.broadcast_to(row_ref[0], (BAND_WIN, BAND_REV))
    x = pltpu.roll(x, BAND_REV - BAND_WIN, 1, stride=1, stride_axis=0)
    x = x[:, :BAND_TQ]
    kc = lax.broadcasted_iota(jnp.int32, x.shape, 0) // CHUNK
    qc = lax.broadcasted_iota(jnp.int32, x.shape, 1) // CHUNK
    band = (kc >= qc) & (kc <= qc + BAND_PAST // CHUNK)
    o_ref[0] = jnp.where(band, x * LOG2E, NEG_INF)


def _band_bias(rel_table):
    nh = rel_table.shape[0]
    t = rel_table.astype(F32)
    row = jnp.concatenate([t, jnp.broadcast_to(t[:, 2 * REL_CLIP:], (nh, BAND_REV - t.shape[1]))], axis=1)
    return pl.pallas_call(
        _band_bias_kernel,
        name="band_bias",
        out_shape=jax.ShapeDtypeStruct((nh, BAND_WIN, BAND_TQ), F32),
        grid=(nh,),
        in_specs=[pl.BlockSpec((1, 1, BAND_REV), lambda h: (h, 0, 0))],
        out_specs=pl.BlockSpec((1, BAND_WIN, BAND_TQ), lambda h: (h, 0, 0)),
        compiler_params=_params("parallel"),
    )(row.reshape(nh, 1, BAND_REV))


def _band_attn_kernel(q_ref, k_ref, vt_ref, z_ref, bias_ref, o_ref, kp_ref, vtp_ref, s0_ref, s1_ref,
                      mx0_ref, mx1_ref, *, nq):
    hd = BAND_HEAD_DIM
    for hh in range(BAND_HPB):
        kp_ref[hh, :BAND_PAST, :] = jnp.zeros((BAND_PAST, hd), BF16)
        kp_ref[hh, BAND_PAST:, :] = k_ref[:, hh * hd:(hh + 1) * hd]
    vtp_ref[:, :BAND_PAST] = jnp.zeros((LANES, BAND_PAST), BF16)
    vtp_ref[:, BAND_PAST:] = vt_ref[...]

    def scores(j, buf):
        s_ref, mx_ref = buf
        start = pl.multiple_of(j * BAND_TQ, BAND_TQ)
        q = q_ref[pl.ds(start, BAND_TQ), :]
        for hh in range(BAND_HPB):
            s_t = lax.dot_general(kp_ref[hh, pl.ds(start, BAND_WIN), :], q[:, hh * hd:(hh + 1) * hd],
                                  NT_DIMS, preferred_element_type=F32) + bias_ref[hh]
            s_ref[hh] = s_t
            mx_ref[hh] = jnp.max(s_t, axis=0, keepdims=True)

    def update(j, buf, masked):
        s_ref, mx_ref = buf
        start = pl.multiple_of(j * BAND_TQ, BAND_TQ)
        outs = []
        for hh in range(BAND_HPB):
            s_t = s_ref[hh]
            s_max = mx_ref[hh]
            if masked:
                kpos = start - BAND_PAST + lax.broadcasted_iota(jnp.int32, (BAND_WIN, 1), 0)
                s_t = jnp.where(kpos >= 0, s_t, NEG_INF)
                s_max = jnp.max(s_t, axis=0, keepdims=True)
            p_t = jnp.exp2(s_t - s_max)
            l = jnp.sum(p_t, axis=0, keepdims=True)
            v_t = vtp_ref[hh * hd:(hh + 1) * hd, pl.ds(start, BAND_WIN)]
            outs.append(jnp.dot(v_t, p_t.astype(BF16), preferred_element_type=F32) * (1.0 / l))
        o = jnp.concatenate(outs, axis=0).T
        z = z_ref[pl.ds(start, BAND_TQ), :].astype(F32)
        o_ref[pl.ds(start, BAND_TQ), :] = (o * _silu(z)).astype(o_ref.dtype)

    buf0, buf1 = (s0_ref, mx0_ref), (s1_ref, mx1_ref)
    scores(0, buf0)
    scores(1, buf1)
    update(0, buf0, True)
    scores(2, buf0)
    update(1, buf1, True)

    def pair(jj, carry):
        j = 2 + 2 * jj
        scores(j + 1, buf1)
        update(j, buf0, False)
        scores(j + 2, buf0)
        update(j + 1, buf1, False)
        return carry

    lax.fori_loop(0, (nq - 4) // 2, pair, 0)
    scores(nq - 1, buf1)
    update(nq - 2, buf0, False)
    update(nq - 1, buf1, False)


def _band_attention(qkz, v_t, bias_t, batch, seq):
    t = qkz.shape[0]
    nq = seq // BAND_TQ
    assert nq >= 4 and nq % 2 == 0
    hp = BAND_HEADS // BAND_HPB
    return pl.pallas_call(
        functools.partial(_band_attn_kernel, nq=nq),
        name="band_attn",
        out_shape=jax.ShapeDtypeStruct((t, BAND_HEADS * BAND_HEAD_DIM), BF16),
        grid=(batch, hp),
        in_specs=[pl.BlockSpec((seq, LANES), lambda b, g: (b, g)),
                  pl.BlockSpec((seq, LANES), lambda b, g: (b, hp + g)),
                  pl.BlockSpec((LANES, seq), lambda b, g: (g, b)),
                  pl.BlockSpec((seq, LANES), lambda b, g: (b, 2 * hp + g)),
                  pl.BlockSpec((BAND_HPB, BAND_WIN, BAND_TQ), lambda b, g: (g, 0, 0))],
        out_specs=pl.BlockSpec((seq, LANES), lambda b, g: (b, g)),
        scratch_shapes=[pltpu.VMEM((BAND_HPB, seq + BAND_PAST, BAND_HEAD_DIM), BF16),
                        pltpu.VMEM((LANES, seq + BAND_PAST), BF16),
                        pltpu.VMEM((BAND_HPB, BAND_WIN, BAND_TQ), F32),
                        pltpu.VMEM((BAND_HPB, BAND_WIN, BAND_TQ), F32),
                        pltpu.VMEM((BAND_HPB, 1, BAND_TQ), F32),
                        pltpu.VMEM((BAND_HPB, 1, BAND_TQ), F32)],
        compiler_params=_params("parallel", "parallel"),
    )(qkz, qkz, v_t, qkz, bias_t)


def _pool_kernel(u_ref, up_ref, z_ref, w_ref, ps_ref, o_ref, *, tm, nb):
    i = pl.program_id(0)
    t0 = (i % nb) * tm
    has_history = t0 > 0
    t = t0 + lax.broadcasted_iota(jnp.int32, (tm, 1), 0)
    for g, win in enumerate(POOL_WINDOWS):
        sl = slice(g * POOL_GROUP, (g + 1) * POOL_GROUP)
        u = u_ref[:, sl]
        acc = jnp.concatenate([jnp.where(has_history, up_ref[:, sl], 0.0), u], axis=0)
        d = 1
        while d < win:
            acc = acc + pltpu.roll(acc, d, 0)
            d *= 2
        wsum = acc[POOL_HALO:, :]
        cnt = jnp.minimum(t + 1, win).astype(F32)
        delta = (wsum / cnt - u).astype(BF16)
        y = jnp.dot(delta, w_ref[g], preferred_element_type=F32) * ps_ref[:, sl]
        o_ref[:, sl] = (y * _silu(z_ref[:, sl])).astype(o_ref.dtype)


def _pool_mix(uz, seq, pool_w, pool_scale, tm=256):
    t = uz.shape[0]
    d = D_MODEL
    tm = min(tm, seq)
    nb = seq // tm
    hb = tm // POOL_HALO
    return pl.pallas_call(
        functools.partial(_pool_kernel, tm=tm, nb=nb),
        name="pool_mix",
        out_shape=jax.ShapeDtypeStruct((t, d), BF16),
        grid=(t // tm,),
        in_specs=[pl.BlockSpec((tm, d), lambda i: (i, 0)),
                  pl.BlockSpec((POOL_HALO, d), lambda i: (jnp.maximum(i * hb - 1, 0), 0)),
                  pl.BlockSpec((tm, d), lambda i: (i, 1)),
                  pl.BlockSpec((len(POOL_WINDOWS), POOL_GROUP, POOL_GROUP), lambda i: (0, 0, 0)),
                  pl.BlockSpec((1, d), lambda i: (0, 0))],
        out_specs=pl.BlockSpec((tm, d), lambda i: (i, 0)),
        compiler_params=_params("parallel"),
    )(uz, uz, uz, pool_w, pool_scale.reshape(1, d).astype(F32))


PROJ_TN = 1024


def _qkz_scale(qk_dim, branch):
    return jnp.concatenate([jnp.full((branch,), qk_dim ** -0.5 * LOG2E, F32), jnp.ones((2 * branch,), F32)])


def kernel(x, c, ada_w, ada_b, norm_pre, norm_post, diff_w_in, diff_w_out, diff_lambda_q1, diff_lambda_k1, diff_lambda_q2, diff_lambda_k2, diff_subln, band_w_in, band_w_out, band_rel_bias, fox_w_in, fox_w_out, fox_forget_bias, pool_w_in, pool_w_out, pool_group_w, pool_scale):
    batch, seq, d = x.shape
    depth = ada_w.shape[0]
    branch = d
    nblk = branch // PROJ_TN
    qkz_blocks = list(range(2 * nblk)) + list(range(3 * nblk, 4 * nblk))
    mod = _ada_mod(c, ada_w, ada_b)
    x2 = x.reshape(batch * seq, d)
    for i in range(depth):
        mod3 = mod[i].reshape(mod.shape[1], 1, 3 * d)
        kind = i % 4
        if kind == 0:
            lambda_init = 0.8 - 0.6 * math.exp(-0.3 * i)
            qkz, h = _norm_proj(x2, seq, norm_pre[i], mod3, diff_w_in.astype(BF16), qkz_blocks,
                                _qkz_scale(DIFF_QK_DIM, branch), emit_h=True)
            v_t = _proj_t(diff_w_in[:, 2 * branch:3 * branch].T.astype(BF16), h)
            o = _diff_attention(qkz, v_t, batch, seq, diff_lambda_q1, diff_lambda_k1, diff_lambda_q2,
                                diff_lambda_k2, diff_subln, lambda_init)
            w_out = diff_w_out
        elif kind == 1:
            qkz, h = _norm_proj(x2, seq, norm_pre[i], mod3, band_w_in.astype(BF16), qkz_blocks,
                                _qkz_scale(BAND_HEAD_DIM, branch), emit_h=True)
            v_t = _proj_t(band_w_in[:, 2 * branch:3 * branch].T.astype(BF16), h)
            o = _band_attention(qkz, v_t, _band_bias(band_rel_bias), batch, seq)
            w_out = band_w_out
        elif kind == 2:
            wf = jnp.pad(fox_w_in[:, 4 * branch:], ((0, 0), (0, LANES - FOX_HEADS))).astype(BF16)
            qkz, h, f_logit = _norm_proj(x2, seq, norm_pre[i], mod3, fox_w_in.astype(BF16), qkz_blocks,
                                         _qkz_scale(FOX_HEAD_DIM, branch), wf=wf, emit_h=True)
            v_t = _proj_t(fox_w_in[:, 2 * branch:3 * branch].T.astype(BF16), h)
            kx, qx = _fox_extend(_fox_gates(f_logit, fox_forget_bias, batch, seq))
            o = _fox_attention(qkz, v_t, kx, qx, batch, seq)
            w_out = fox_w_out
        else:
            uz, = _norm_proj(x2, seq, norm_pre[i], mod3, pool_w_in.astype(BF16), list(range(2 * nblk)),
                             jnp.ones((2 * branch,), F32), out_dtype=F32)
            o = _pool_mix(uz, seq, pool_group_w.astype(BF16), pool_scale)
            w_out = pool_w_out
        x2 = _out_proj_residual(o, w_out.astype(BF16), x2, seq, mod3, norm_post[i])
    return x2.reshape(batch, seq, d)
```

```python
import functools
import math

import jax
import jax.numpy as jnp
from jax import lax
from jax.experimental import pallas as pl
from jax.experimental.pallas import tpu as pltpu

F32 = jnp.float32
BF16 = jnp.bfloat16

D_MODEL = 2048
EPS = 1e-6
NEG_INF = -1e30
LOG2E = math.log2(math.e)
CHUNK = 64
DIFF_HEADS, DIFF_V_DIM, DIFF_QK_DIM = 8, 256, 128
BAND_HEADS, BAND_HEAD_DIM, BAND_PAST, REL_CLIP = 32, 64, 512, 256
FOX_HEADS, FOX_HEAD_DIM = 16, 128
POOL_WINDOWS = (2, 4, 8, 16)
POOL_GROUP = D_MODEL // len(POOL_WINDOWS)
POOL_HALO = 16

LANES = 128
VMEM_LIMIT = 52 * 1024 * 1024

NT_DIMS = (((1,), (1,)), ((), ()))


def _silu(x):
    return x / (1.0 + jnp.exp(-x))


def _params(*sem):
    return pltpu.CompilerParams(dimension_semantics=sem, vmem_limit_bytes=VMEM_LIMIT)


def _ada_kernel(c_ref, w_ref, b_ref, o_ref):
    ca = _silu(c_ref[...]).astype(BF16)
    o_ref[0] = jnp.dot(ca, w_ref[0].astype(BF16), preferred_element_type=F32) + b_ref[0]


def _ada_mod(c, ada_w, ada_b):
    b, d = c.shape
    bp = -(-b // 8) * 8
    depth, _, n3 = ada_w.shape
    tn = 768
    cp = jnp.pad(c, ((0, bp - b), (0, 0)))
    return pl.pallas_call(
        _ada_kernel,
        name="ada_mod",
        out_shape=jax.ShapeDtypeStruct((depth, bp, n3), F32),
        grid=(depth, n3 // tn),
        in_specs=[pl.BlockSpec((bp, d), lambda l, j: (0, 0)),
                  pl.BlockSpec((1, d, tn), lambda l, j: (l, 0, j)),
                  pl.BlockSpec((1, 1, tn), lambda l, j: (l, 0, j))],
        out_specs=pl.BlockSpec((1, bp, tn), lambda l, j: (l, 0, j)),
        compiler_params=_params("parallel", "parallel"),
    )(cp, ada_w, ada_b.reshape(depth, 1, n3))


def _norm_proj_kernel(x_ref, g_ref, sh_ref, sc_ref, w_ref, cs_ref, *rest, has_f, emit_h, sub):
    rest = list(rest)
    wf_ref = rest.pop(0) if has_f else None
    o_ref = rest.pop(0)
    hout_ref = rest.pop(0) if emit_h else None
    f_ref = rest.pop(0) if has_f else None
    h_ref = rest.pop(0)

    @pl.when(pl.program_id(1) == 0)
    def _():
        g = g_ref[...]
        sc = 1.0 + sc_ref[0]
        sh = sh_ref[0]
        for r in range(x_ref.shape[0] // sub):
            rows = slice(r * sub, (r + 1) * sub)
            x = x_ref[rows, :]
            y = x * lax.rsqrt(jnp.mean(x * x, axis=-1, keepdims=True) + EPS) * g
            h = (y * sc + sh).astype(BF16)
            h_ref[rows, :] = h
            if emit_h:
                hout_ref[rows, :] = h
            if has_f:
                f_ref[rows, :] = jnp.dot(h, wf_ref[...], preferred_element_type=F32)
            y = jnp.dot(h, w_ref[...], preferred_element_type=F32)
            o_ref[rows, :] = (y * cs_ref[...]).astype(o_ref.dtype)

    @pl.when(pl.program_id(1) > 0)
    def _():
        y = jnp.dot(h_ref[...], w_ref[...], preferred_element_type=F32)
        o_ref[...] = (y * cs_ref[...]).astype(o_ref.dtype)


def _norm_proj(x2, seq, g, mod3, w, col_blocks, col_scale, wf=None, emit_h=False, out_dtype=BF16,
               tm=512, tn=1024):
    t, d = x2.shape
    tm = min(tm, seq)
    nb = seq // tm
    nj = len(col_blocks)
    n = nj * tn
    first, skip_from, skip = col_blocks[0], None, 0
    for a, b2 in zip(col_blocks, col_blocks[1:]):
        if b2 != a + 1:
            skip_from, skip = a - first + 1, b2 - a - 1
    if skip_from is None:
        wmap = lambda i, j: (0, j + first)
    else:
        wmap = lambda i, j: (0, j + first + jnp.where(j >= skip_from, skip, 0))
    in_specs = [pl.BlockSpec((tm, d), lambda i, j: (i, 0)),
                pl.BlockSpec((1, d), lambda i, j: (0, 0)),
                pl.BlockSpec((1, 1, d), lambda i, j: (i // nb, 0, 0)),
                pl.BlockSpec((1, 1, d), lambda i, j: (i // nb, 0, 1)),
                pl.BlockSpec((d, tn), wmap),
                pl.BlockSpec((1, tn), lambda i, j: (0, j))]
    out_shape = [jax.ShapeDtypeStruct((t, n), out_dtype)]
    out_specs = [pl.BlockSpec((tm, tn), lambda i, j: (i, j))]
    args = [x2, g.reshape(1, d), mod3, mod3, w, col_scale.reshape(1, n).astype(F32)]
    if wf is not None:
        in_specs.append(pl.BlockSpec((d, LANES), lambda i, j: (0, 0)))
        args.append(wf)
    if emit_h:
        out_shape.append(jax.ShapeDtypeStruct((t, d), BF16))
        out_specs.append(pl.BlockSpec((tm, d), lambda i, j: (i, 0)))
    if wf is not None:
        out_shape.append(jax.ShapeDtypeStruct((t, LANES), F32))
        out_specs.append(pl.BlockSpec((tm, LANES), lambda i, j: (i, 0)))
    return pl.pallas_call(
        functools.partial(_norm_proj_kernel, has_f=wf is not None, emit_h=emit_h, sub=min(256, tm)),
        name="norm_proj",
        out_shape=out_shape,
        grid=(t // tm, nj),
        in_specs=in_specs,
        out_specs=out_specs,
        scratch_shapes=[pltpu.VMEM((tm, d), BF16)],
        compiler_params=_params("parallel", "arbitrary"),
    )(*args)


def _proj_t_kernel(w_ref, h_ref, o_ref):
    o_ref[...] = lax.dot_general(w_ref[...], h_ref[...], NT_DIMS,
                                 preferred_element_type=F32).astype(o_ref.dtype)


def _proj_t(w_t, h, tn=512, tm=1024):
    n, d = w_t.shape
    t = h.shape[0]
    tm = min(tm, t)
    return pl.pallas_call(
        _proj_t_kernel,
        name="proj_t",
        out_shape=jax.ShapeDtypeStruct((n, t), BF16),
        grid=(t // tm, n // tn),
        in_specs=[pl.BlockSpec((tn, d), lambda i, j: (j, 0)),
                  pl.BlockSpec((tm, d), lambda i, j: (i, 0))],
        out_specs=pl.BlockSpec((tn, tm), lambda i, j: (j, i)),
        compiler_params=_params("parallel", "arbitrary"),
    )(w_t, h)


def _out_proj_kernel(o_ref, w_ref, x_ref, gate_ref, g_ref, out_ref, *, sub):
    for r in range(o_ref.shape[0] // sub):
        rows = slice(r * sub, (r + 1) * sub)
        y = jnp.dot(o_ref[rows, :], w_ref[...], preferred_element_type=F32)
        yn = y * lax.rsqrt(jnp.mean(y * y, axis=-1, keepdims=True) + EPS) * g_ref[...]
        out_ref[rows, :] = x_ref[rows, :] + gate_ref[0] * yn


def _out_proj_residual(o2, w, x2, seq, mod3, g, tm=512):
    t, d = x2.shape
    tm = min(tm, seq)
    nb = seq // tm
    return pl.pallas_call(
        functools.partial(_out_proj_kernel, sub=min(256, tm)),
        name="out_proj",
        out_shape=jax.ShapeDtypeStruct((t, d), F32),
        grid=(t // tm,),
        in_specs=[pl.BlockSpec((tm, d), lambda i: (i, 0)),
                  pl.BlockSpec((d, d), lambda i: (0, 0), pipeline_mode=pl.Buffered(1)),
                  pl.BlockSpec((tm, d), lambda i: (i, 0)),
                  pl.BlockSpec((1, 1, d), lambda i: (i // nb, 0, 2)),
                  pl.BlockSpec((1, d), lambda i: (0, 0))],
        out_specs=pl.BlockSpec((tm, d), lambda i: (i, 0)),
        compiler_params=_params("parallel"),
    )(o2, w, x2, mod3, g.reshape(1, d))


def _flash_step_t(s_t, s_max, v_t, m_ref, l_ref, acc_ref, idx, cols):
    m_prev = m_ref[idx, :, cols]
    m_new = jnp.maximum(m_prev, s_max)
    alpha = jnp.exp2(m_prev - m_new)
    p_t = jnp.exp2(s_t - m_new)
    l_ref[idx, :, cols] = alpha * l_ref[idx, :, cols] + jnp.sum(p_t, axis=0, keepdims=True)
    acc_ref[idx, :, cols] = (alpha * acc_ref[idx, :, cols]
                             + jnp.dot(v_t, p_t.astype(BF16), preferred_element_type=F32))
    m_ref[idx, :, cols] = m_new


def _flash_init(m_ref, l_ref, acc_ref):
    m_ref[...] = jnp.full(m_ref.shape, NEG_INF, F32)
    l_ref[...] = jnp.zeros(l_ref.shape, F32)
    acc_ref[...] = jnp.zeros(acc_ref.shape, F32)


ATTN_TK = 512


def _diag_masked(s_t, granule):
    kg = lax.broadcasted_iota(jnp.int32, s_t.shape, 0) // granule
    qg = lax.broadcasted_iota(jnp.int32, s_t.shape, 1) // granule
    return jnp.where(kg <= qg, s_t, NEG_INF)


def _causal_sweep(qi, scores, update, buf0, buf1):
    every, late = slice(None), slice(ATTN_TK, 2 * ATTN_TK)
    scores(0, buf0, every)

    def pair(jj, carry):
        j = 2 * jj
        scores(j + 1, buf1, every)
        update(j, buf0, False, every)
        scores(j + 2, buf0, every)
        update(j + 1, buf1, False, every)
        return carry

    lax.fori_loop(0, qi, pair, 0)
    scores(2 * qi + 1, buf1, late)
    update(2 * qi, buf0, True, every)
    update(2 * qi + 1, buf1, True, late)


def _diff_attn_kernel(q_ref, k_ref, vt_ref, z_ref, lq1_ref, lk1_ref, lq2_ref, lk2_ref, g_ref,
                      o_ref, m_ref, l_ref, acc_ref, s0_ref, s1_ref, mx0_ref, mx1_ref, *, lambda_init):
    qi = pl.program_id(2)
    _flash_init(m_ref, l_ref, acc_ref)
    q = q_ref[...]

    def scores(ki, buf, cols):
        s_ref, mx_ref = buf
        k = k_ref[pl.ds(pl.multiple_of(ki * ATTN_TK, ATTN_TK), ATTN_TK), :]
        for m in range(2):
            sl = slice(m * DIFF_QK_DIM, (m + 1) * DIFF_QK_DIM)
            s_t = lax.dot_general(k[:, sl], q[cols, sl], NT_DIMS, preferred_element_type=F32)
            s_ref[m, :, cols] = s_t
            mx_ref[m, :, cols] = jnp.max(s_t, axis=0, keepdims=True)

    def update(ki, buf, masked, cols):
        s_ref, mx_ref = buf
        v_t = vt_ref[:, pl.ds(pl.multiple_of(ki * ATTN_TK, ATTN_TK), ATTN_TK)]
        for m in range(2):
            s_t = s_ref[m, :, cols]
            s_max = mx_ref[m, :, cols]
            if masked:
                s_t = _diag_masked(s_t, CHUNK)
                s_max = jnp.max(s_t, axis=0, keepdims=True)
            _flash_step_t(s_t, s_max, v_t, m_ref, l_ref, acc_ref, m, cols)

    _causal_sweep(qi, scores, update, (s0_ref, mx0_ref), (s1_ref, mx1_ref))

    lam = (jnp.exp(jnp.sum(lq1_ref[...] * lk1_ref[...], axis=-1, keepdims=True))
           - jnp.exp(jnp.sum(lq2_ref[...] * lk2_ref[...], axis=-1, keepdims=True)) + lambda_init)
    o_t = acc_ref[0] * (1.0 / l_ref[0]) - acc_ref[1] * (lam / l_ref[1])
    o = o_t.T
    on = o * lax.rsqrt(jnp.mean(o * o, axis=-1, keepdims=True) + EPS) * g_ref[...]
    on = on * (1.0 - lambda_init)
    o_ref[...] = (on * _silu(z_ref[...].astype(F32))).astype(o_ref.dtype)


def _diff_attention(qkz, v_t, batch, seq, lq1, lk1, lq2, lk2, subln_g, lambda_init):
    t = qkz.shape[0]
    tq = 2 * ATTN_TK
    nq = seq // tq
    h, dv = DIFF_HEADS, DIFF_V_DIM
    vec = lambda a: a.reshape(1, -1).astype(F32)
    small = lambda n: pl.BlockSpec((1, n), lambda b, hh, i: (0, 0))
    return pl.pallas_call(
        functools.partial(_diff_attn_kernel, lambda_init=lambda_init),
        name="diff_attn",
        out_shape=jax.ShapeDtypeStruct((t, h * dv), BF16),
        grid=(batch, h, nq),
        in_specs=[pl.BlockSpec((tq, dv), lambda b, hh, i: (b * nq + i, hh)),
                  pl.BlockSpec((seq, dv), lambda b, hh, i: (b, h + hh)),
                  pl.BlockSpec((dv, seq), lambda b, hh, i: (hh, b)),
                  pl.BlockSpec((tq, dv), lambda b, hh, i: (b * nq + i, 2 * h + hh)),
                  small(DIFF_QK_DIM), small(DIFF_QK_DIM), small(DIFF_QK_DIM), small(DIFF_QK_DIM),
                  small(dv)],
        out_specs=pl.BlockSpec((tq, dv), lambda b, hh, i: (b * nq + i, hh)),
        scratch_shapes=[pltpu.VMEM((2, 1, tq), F32), pltpu.VMEM((2, 1, tq), F32),
                        pltpu.VMEM((2, dv, tq), F32),
                        pltpu.VMEM((2, ATTN_TK, tq), F32), pltpu.VMEM((2, ATTN_TK, tq), F32),
                        pltpu.VMEM((2, 1, tq), F32), pltpu.VMEM((2, 1, tq), F32)],
        compiler_params=_params("parallel", "parallel", "arbitrary"),
    )(qkz, qkz, v_t, qkz, vec(lq1), vec(lk1), vec(lq2), vec(lk2), vec(subln_g))


FOX_PIECES = 3


def _fox_gate_kernel(f_ref, b_ref, o_ref):
    x = f_ref[...] + b_ref[...]
    lf = jnp.minimum(x, 0.0) - jnp.log1p(jnp.exp(-jnp.abs(x)))
    n = lf.shape[0]
    row = lax.broadcasted_iota(jnp.int32, lf.shape, 0)
    d = 1
    while d < n:
        lf = lf + jnp.where(row >= d, pltpu.roll(lf, d, 0), 0.0)
        d *= 2
    rest = lf * LOG2E
    lane = lax.broadcasted_iota(jnp.int32, lf.shape, 1)
    out = jnp.zeros(lf.shape, F32)
    for p in range(FOX_PIECES):
        piece = rest.astype(BF16).astype(F32)
        rest = rest - piece
        moved = piece if p == 0 else pltpu.roll(piece, p * FOX_HEADS, 1)
        out = jnp.where((lane >= p * FOX_HEADS) & (lane < (p + 1) * FOX_HEADS), moved, out)
    o_ref[...] = out.astype(BF16)


def _fox_gates(f_logit, bias, batch, seq):
    bpad = jnp.zeros((1, LANES), F32).at[0, :FOX_HEADS].set(bias.astype(F32))
    return pl.pallas_call(
        _fox_gate_kernel,
        name="fox_gates",
        out_shape=jax.ShapeDtypeStruct((batch * seq, LANES), BF16),
        grid=(batch,),
        in_specs=[pl.BlockSpec((seq, LANES), lambda b: (b, 0)),
                  pl.BlockSpec((1, LANES), lambda b: (0, 0))],
        out_specs=pl.BlockSpec((seq, LANES), lambda b: (b, 0)),
        compiler_params=_params("parallel"),
    )(f_logit, bpad)


def _fox_place_kernel(f_ref, pk_ref, pq_ref, ck_ref, cq_ref, kx_ref, qx_ref):
    f = f_ref[...]
    kx_ref[...] = (jnp.dot(f, pk_ref[...], preferred_element_type=F32) + ck_ref[...]).astype(BF16)
    qx_ref[...] = (jnp.dot(f, pq_ref[...], preferred_element_type=F32) + cq_ref[...]).astype(BF16)


def _fox_extend(pieces, tm=1024):
    t = pieces.shape[0]
    tm = min(tm, t)
    n = FOX_HEADS * LANES
    src = jnp.arange(LANES)[:, None]
    dst = jnp.arange(n)[None, :]
    p_src, h_src = src // FOX_HEADS, src % FOX_HEADS
    h_dst, c_dst = dst // LANES, dst % LANES
    live = (p_src < FOX_PIECES) & (h_src == h_dst)
    pk = jnp.where(live & (c_dst == p_src), -1.0, 0.0).astype(BF16)
    pq = jnp.where(live & (c_dst == p_src + FOX_PIECES), 1.0, 0.0).astype(BF16)
    ck = ((c_dst >= FOX_PIECES) & (c_dst < 2 * FOX_PIECES)).astype(F32)
    cq = (c_dst < FOX_PIECES).astype(F32)
    mat = pl.BlockSpec((LANES, n), lambda i: (0, 0))
    vec = pl.BlockSpec((1, n), lambda i: (0, 0))
    return pl.pallas_call(
        _fox_place_kernel,
        name="fox_extend",
        out_shape=[jax.ShapeDtypeStruct((t, n), BF16), jax.ShapeDtypeStruct((t, n), BF16)],
        grid=(t // tm,),
        in_specs=[pl.BlockSpec((tm, LANES), lambda i: (i, 0)), mat, mat, vec, vec],
        out_specs=[pl.BlockSpec((tm, n), lambda i: (i, 0)), pl.BlockSpec((tm, n), lambda i: (i, 0))],
        compiler_params=_params("parallel"),
    )(pieces, pk, pq, ck, cq)


def _causal_blocks(nq, begin, scores, update, finish, buf0, buf1):
    every, late = slice(None), slice(ATTN_TK, 2 * ATTN_TK)
    begin(0)
    scores(0, buf0, every)

    def qblock(qi, carry):
        def pair(jj, c):
            j = 2 * jj
            scores(j + 1, buf1, every)
            update(j, buf0, False, every)
            scores(j + 2, buf0, every)
            update(j + 1, buf1, False, every)
            return c

        lax.fori_loop(0, qi, pair, 0)
        scores(2 * qi + 1, buf1, late)
        update(2 * qi, buf0, True, every)
        update(2 * qi + 1, buf1, True, late)
        finish(qi)
        begin(jnp.minimum(qi + 1, nq - 1))
        scores(0, buf0, every)
        return carry

    lax.fori_loop(0, nq, qblock, 0)


def _fox_attn_kernel(q_ref, qx_ref, k_ref, kx_ref, vt_ref, z_ref, o_ref, m_ref, l_ref, acc_ref,
                     s0_ref, s1_ref, mx0_ref, mx1_ref, qt_ref, *, nq):
    tq = 2 * ATTN_TK
    dh = FOX_HEAD_DIM

    def begin(qi):
        rows = pl.ds(pl.multiple_of(qi * tq, tq), tq)
        qt_ref[:dh, :] = q_ref[rows, :].T
        qt_ref[dh:, :] = qx_ref[rows, :].T
        _flash_init(m_ref, l_ref, acc_ref)

    def finish(qi):
        rows = pl.ds(pl.multiple_of(qi * tq, tq), tq)
        o = (acc_ref[0] * (1.0 / l_ref[0])).T
        o_ref[rows, :] = (o * _silu(z_ref[rows, :].astype(F32))).astype(o_ref.dtype)

    def scores(ki, buf, cols):
        s_ref, mx_ref = buf
        off = pl.multiple_of(ki * ATTN_TK, ATTN_TK)
        k = jnp.concatenate([k_ref[pl.ds(off, ATTN_TK), :], kx_ref[pl.ds(off, ATTN_TK), :]], axis=1)
        s_t = jnp.dot(k, qt_ref[:, cols], preferred_element_type=F32)
        s_ref[:, cols] = s_t
        mx_ref[:, cols] = jnp.max(s_t, axis=0, keepdims=True)

    def update(ki, buf, masked, cols):
        s_ref, mx_ref = buf
        v_t = vt_ref[:, pl.ds(pl.multiple_of(ki * ATTN_TK, ATTN_TK), ATTN_TK)]
        s_t = s_ref[:, cols]
        s_max = mx_ref[:, cols]
        if masked:
            s_t = _diag_masked(s_t, 1)
            s_max = jnp.max(s_t, axis=0, keepdims=True)
        _flash_step_t(s_t, s_max, v_t, m_ref, l_ref, acc_ref, 0, cols)

    _causal_blocks(nq, begin, scores, update, finish, (s0_ref, mx0_ref), (s1_ref, mx1_ref))


def _fox_attention(qkz, v_t, kx, qx, batch, seq):
    t = qkz.shape[0]
    tq = 2 * ATTN_TK
    nq = seq // tq
    h, dh = FOX_HEADS, FOX_HEAD_DIM
    return pl.pallas_call(
        functools.partial(_fox_attn_kernel, nq=nq),
        name="fox_attn",
        out_shape=jax.ShapeDtypeStruct((t, h * dh), BF16),
        grid=(batch, h),
        in_specs=[pl.BlockSpec((seq, dh), lambda b, hh: (b, hh)),
                  pl.BlockSpec((seq, LANES), lambda b, hh: (b, hh)),
                  pl.BlockSpec((seq, dh), lambda b, hh: (b, h + hh)),
                  pl.BlockSpec((seq, LANES), lambda b, hh: (b, hh)),
                  pl.BlockSpec((dh, seq), lambda b, hh: (hh, b)),
                  pl.BlockSpec((seq, dh), lambda b, hh: (b, 2 * h + hh))],
        out_specs=pl.BlockSpec((seq, dh), lambda b, hh: (b, hh)),
        scratch_shapes=[pltpu.VMEM((1, 1, tq), F32), pltpu.VMEM((1, 1, tq), F32),
                        pltpu.VMEM((1, dh, tq), F32),
                        pltpu.VMEM((ATTN_TK, tq), F32), pltpu.VMEM((ATTN_TK, tq), F32),
                        pltpu.VMEM((1, tq), F32), pltpu.VMEM((1, tq), F32),
                        pltpu.VMEM((dh + LANES, tq), BF16)],
        compiler_params=_params("parallel", "parallel"),
    )(qkz, qx, qkz, kx, v_t, qkz)


BAND_TQ = 256
BAND_WIN = BAND_PAST + BAND_TQ
BAND_HPB = LANES // BAND_HEAD_DIM
BAND_REV = 1024


def _band_bias_kernel(row_ref, o_ref):
    x = jnp.broadcast_to(row_ref[0], (BAND_WIN, BAND_REV))
    x = pltpu.roll(x, BAND_REV - BAND_WIN, 1, stride=1, stride_axis=0)
    x = x[:, :BAND_TQ]
    kc = lax.broadcasted_iota(jnp.int32, x.shape, 0) // CHUNK
    qc = lax.broadcasted_iota(jnp.int32, x.shape, 1) // CHUNK
    band = (kc >= qc) & (kc <= qc + BAND_PAST // CHUNK)
    o_ref[0] = jnp.where(band, x * LOG2E, NEG_INF)


def _band_bias(rel_table):
    nh = rel_table.shape[0]
    t = rel_table.astype(F32)
    row = jnp.concatenate([t, jnp.broadcast_to(t[:, 2 * REL_CLIP:], (nh, BAND_REV - t.shape[1]))], axis=1)
    return pl.pallas_call(
        _band_bias_kernel,
        name="band_bias",
        out_shape=jax.ShapeDtypeStruct((nh, BAND_WIN, BAND_TQ), F32),
        grid=(nh,),
        in_specs=[pl.BlockSpec((1, 1, BAND_REV), lambda h: (h, 0, 0))],
        out_specs=pl.BlockSpec((1, BAND_WIN, BAND_TQ), lambda h: (h, 0, 0)),
        compiler_params=_params("parallel"),
    )(row.reshape(nh, 1, BAND_REV))


def _band_attn_kernel(q_ref, k_ref, vt_ref, z_ref, bias_ref, o_ref, kp_ref, vtp_ref, s0_ref, s1_ref,
                      mx0_ref, mx1_ref, *, nq):
    hd = BAND_HEAD_DIM
    for hh in range(BAND_HPB):
        kp_ref[hh, :BAND_PAST, :] = jnp.zeros((BAND_PAST, hd), BF16)
        kp_ref[hh, BAND_PAST:, :] = k_ref[:, hh * hd:(hh + 1) * hd]
    vtp_ref[:, :BAND_PAST] = jnp.zeros((LANES, BAND_PAST), BF16)
    vtp_ref[:, BAND_PAST:] = vt_ref[...]

    def scores(j, buf):
        s_ref, mx_ref = buf
        start = pl.multiple_of(j * BAND_TQ, BAND_TQ)
        q = q_ref[pl.ds(start, BAND_TQ), :]
        for hh in range(BAND_HPB):
            s_t = lax.dot_general(kp_ref[hh, pl.ds(start, BAND_WIN), :], q[:, hh * hd:(hh + 1) * hd],
                                  NT_DIMS, preferred_element_type=F32) + bias_ref[hh]
            s_ref[hh] = s_t
            mx_ref[hh] = jnp.max(s_t, axis=0, keepdims=True)

    def update(j, buf, masked):
        s_ref, mx_ref = buf
        start = pl.multiple_of(j * BAND_TQ, BAND_TQ)
        outs = []
        for hh in range(BAND_HPB):
            s_t = s_ref[hh]
            s_max = mx_ref[hh]
            if masked:
                kpos = start - BAND_PAST + lax.broadcasted_iota(jnp.int32, (BAND_WIN, 1), 0)
                s_t = jnp.where(kpos >= 0, s_t, NEG_INF)
                s_max = jnp.max(s_t, axis=0, keepdims=True)
            p_t = jnp.exp2(s_t - s_max)
            l = jnp.sum(p_t, axis=0, keepdims=True)
            v_t = vtp_ref[hh * hd:(hh + 1) * hd, pl.ds(start, BAND_WIN)]
            outs.append(jnp.dot(v_t, p_t.astype(BF16), preferred_element_type=F32) * (1.0 / l))
        o = jnp.concatenate(outs, axis=0).T
        z = z_ref[pl.ds(start, BAND_TQ), :].astype(F32)
        o_ref[pl.ds(start, BAND_TQ), :] = (o * _silu(z)).astype(o_ref.dtype)

    buf0, buf1 = (s0_ref, mx0_ref), (s1_ref, mx1_ref)
    scores(0, buf0)
    scores(1, buf1)
    update(0, buf0, True)
    scores(2, buf0)
    update(1, buf1, True)

    def pair(jj, carry):
        j = 2 + 2 * jj
        scores(j + 1, buf1)
        update(j, buf0, False)
        scores(j + 2, buf0)
        update(j + 1, buf1, False)
        return carry

    lax.fori_loop(0, (nq - 4) // 2, pair, 0)
    scores(nq - 1, buf1)
    update(nq - 2, buf0, False)
    update(nq - 1, buf1, False)


def _band_attention(qkz, v_t, bias_t, batch, seq):
    t = qkz.shape[0]
    nq = seq // BAND_TQ
    assert nq >= 4 and nq % 2 == 0
    hp = BAND_HEADS // BAND_HPB
    return pl.pallas_call(
        functools.partial(_band_attn_kernel, nq=nq),
        name="band_attn",
        out_shape=jax.ShapeDtypeStruct((t, BAND_HEADS * BAND_HEAD_DIM), BF16),
        grid=(batch, hp),
        in_specs=[pl.BlockSpec((seq, LANES), lambda b, g: (b, g)),
                  pl.BlockSpec((seq, LANES), lambda b, g: (b, hp + g)),
                  pl.BlockSpec((LANES, seq), lambda b, g: (g, b)),
                  pl.BlockSpec((seq, LANES), lambda b, g: (b, 2 * hp + g)),
                  pl.BlockSpec((BAND_HPB, BAND_WIN, BAND_TQ), lambda b, g: (g, 0, 0))],
        out_specs=pl.BlockSpec((seq, LANES), lambda b, g: (b, g)),
        scratch_shapes=[pltpu.VMEM((BAND_HPB, seq + BAND_PAST, BAND_HEAD_DIM), BF16),
                        pltpu.VMEM((LANES, seq + BAND_PAST), BF16),
                        pltpu.VMEM((BAND_HPB, BAND_WIN, BAND_TQ), F32),
                        pltpu.VMEM((BAND_HPB, BAND_WIN, BAND_TQ), F32),
                        pltpu.VMEM((BAND_HPB, 1, BAND_TQ), F32),
                        pltpu.VMEM((BAND_HPB, 1, BAND_TQ), F32)],
        compiler_params=_params("parallel", "parallel"),
    )(qkz, qkz, v_t, qkz, bias_t)


def _pool_kernel(u_ref, up_ref, z_ref, w_ref, ps_ref, o_ref, *, tm, nb):
    i = pl.program_id(0)
    t0 = (i % nb) * tm
    has_history = t0 > 0
    t = t0 + lax.broadcasted_iota(jnp.int32, (tm, 1), 0)
    for g, win in enumerate(POOL_WINDOWS):
        sl = slice(g * POOL_GROUP, (g + 1) * POOL_GROUP)
        u = u_ref[:, sl]
        acc = jnp.concatenate([jnp.where(has_history, up_ref[:, sl], 0.0), u], axis=0)
        d = 1
        while d < win:
            acc = acc + pltpu.roll(acc, d, 0)
            d *= 2
        wsum = acc[POOL_HALO:, :]
        cnt = jnp.minimum(t + 1, win).astype(F32)
        delta = (wsum / cnt - u).astype(BF16)
        y = jnp.dot(delta, w_ref[g], preferred_element_type=F32) * ps_ref[:, sl]
        o_ref[:, sl] = (y * _silu(z_ref[:, sl])).astype(o_ref.dtype)


def _pool_mix(uz, seq, pool_w, pool_scale, tm=256):
    t = uz.shape[0]
    d = D_MODEL
    tm = min(tm, seq)
    nb = seq // tm
    hb = tm // POOL_HALO
    return pl.pallas_call(
        functools.partial(_pool_kernel, tm=tm, nb=nb),
        name="pool_mix",
        out_shape=jax.ShapeDtypeStruct((t, d), BF16),
        grid=(t // tm,),
        in_specs=[pl.BlockSpec((tm, d), lambda i: (i, 0)),
                  pl.BlockSpec((POOL_HALO, d), lambda i: (jnp.maximum(i * hb - 1, 0), 0)),
                  pl.BlockSpec((tm, d), lambda i: (i, 1)),
                  pl.BlockSpec((len(POOL_WINDOWS), POOL_GROUP, POOL_GROUP), lambda i: (0, 0, 0)),
                  pl.BlockSpec((1, d), lambda i: (0, 0))],
        out_specs=pl.BlockSpec((tm, d), lambda i: (i, 0)),
        compiler_params=_params("parallel"),
    )(uz, uz, uz, pool_w, pool_scale.reshape(1, d).astype(F32))


PROJ_TN = 1024


def _qkz_scale(qk_dim, branch):
    return jnp.concatenate([jnp.full((branch,), qk_dim ** -0.5 * LOG2E, F32), jnp.ones((2 * branch,), F32)])


def kernel(x, c, ada_w, ada_b, norm_pre, norm_post, diff_w_in, diff_w_out, diff_lambda_q1, diff_lambda_k1, diff_lambda_q2, diff_lambda_k2, diff_subln, band_w_in, band_w_out, band_rel_bias, fox_w_in, fox_w_out, fox_forget_bias, pool_w_in, pool_w_out, pool_group_w, pool_scale):
    batch, seq, d = x.shape
    depth = ada_w.shape[0]
    branch = d
    nblk = branch // PROJ_TN
    qkz_blocks = list(range(2 * nblk)) + list(range(3 * nblk, 4 * nblk))
    mod = _ada_mod(c, ada_w, ada_b)
    x2 = x.reshape(batch * seq, d)
    for i in range(depth):
        mod3 = mod[i].reshape(mod.shape[1], 1, 3 * d)
        kind = i % 4
        if kind == 0:
            lambda_init = 0.8 - 0.6 * math.exp(-0.3 * i)
            qkz, h = _norm_proj(x2, seq, norm_pre[i], mod3, diff_w_in.astype(BF16), qkz_blocks,
                                _qkz_scale(DIFF_QK_DIM, branch), emit_h=True)
            v_t = _proj_t(diff_w_in[:, 2 * branch:3 * branch].T.astype(BF16), h)
            o = _diff_attention(qkz, v_t, batch, seq, diff_lambda_q1, diff_lambda_k1, diff_lambda_q2,
                                diff_lambda_k2, diff_subln, lambda_init)
            w_out = diff_w_out
        elif kind == 1:
            qkz, h = _norm_proj(x2, seq, norm_pre[i], mod3, band_w_in.astype(BF16), qkz_blocks,
                                _qkz_scale(BAND_HEAD_DIM, branch), emit_h=True)
            v_t = _proj_t(band_w_in[:, 2 * branch:3 * branch].T.astype(BF16), h)
            o = _band_attention(qkz, v_t, _band_bias(band_rel_bias), batch, seq)
            w_out = band_w_out
        elif kind == 2:
            wf = jnp.pad(fox_w_in[:, 4 * branch:], ((0, 0), (0, LANES - FOX_HEADS))).astype(BF16)
            qkz, h, f_logit = _norm_proj(x2, seq, norm_pre[i], mod3, fox_w_in.astype(BF16), qkz_blocks,
                                         _qkz_scale(FOX_HEAD_DIM, branch), wf=wf, emit_h=True)
            v_t = _proj_t(fox_w_in[:, 2 * branch:3 * branch].T.astype(BF16), h)
            kx, qx = _fox_extend(_fox_gates(f_logit, fox_forget_bias, batch, seq))
            o = _fox_attention(qkz, v_t, kx, qx, batch, seq)
            w_out = fox_w_out
        else:
            uz, = _norm_proj(x2, seq, norm_pre[i], mod3, pool_w_in.astype(BF16), list(range(2 * nblk)),
                             jnp.ones((2 * branch,), F32), out_dtype=F32)
            o = _pool_mix(uz, seq, pool_group_w.astype(BF16), pool_scale)
            w_out = pool_w_out
        x2 = _out_proj_residual(o, w_out.astype(BF16), x2, seq, mod3, norm_post[i])
    return x2.reshape(batch, seq, d)
```

```python
import functools
import math

import jax
import jax.numpy as jnp
from jax import lax
from jax.experimental import pallas as pl
from jax.experimental.pallas import tpu as pltpu

F32 = jnp.float32
BF16 = jnp.bfloat16

D_MODEL = 2048
EPS = 1e-6
NEG_INF = -1e30
LOG2E = math.log2(math.e)
CHUNK = 64
DIFF_HEADS, DIFF_V_DIM, DIFF_QK_DIM = 8, 256, 128
BAND_HEADS, BAND_HEAD_DIM, BAND_PAST, REL_CLIP = 32, 64, 512, 256
FOX_HEADS, FOX_HEAD_DIM = 16, 128
POOL_WINDOWS = (2, 4, 8, 16)
POOL_GROUP = D_MODEL // len(POOL_WINDOWS)
POOL_HALO = 16

LANES = 128
VMEM_LIMIT = 52 * 1024 * 1024

NT_DIMS = (((1,), (1,)), ((), ()))


def _silu(x):
    return x / (1.0 + jnp.exp(-x))


def _params(*sem):
    return pltpu.CompilerParams(dimension_semantics=sem, vmem_limit_bytes=VMEM_LIMIT)


def _ada_kernel(c_ref, w_ref, b_ref, o_ref):
    ca = _silu(c_ref[...]).astype(BF16)
    o_ref[0] = jnp.dot(ca, w_ref[0].astype(BF16), preferred_element_type=F32) + b_ref[0]


def _ada_mod(c, ada_w, ada_b):
    b, d = c.shape
    bp = -(-b // 8) * 8
    depth, _, n3 = ada_w.shape
    tn = 768
    cp = jnp.pad(c, ((0, bp - b), (0, 0)))
    return pl.pallas_call(
        _ada_kernel,
        name="ada_mod",
        out_shape=jax.ShapeDtypeStruct((depth, bp, n3), F32),
        grid=(depth, n3 // tn),
        in_specs=[pl.BlockSpec((bp, d), lambda l, j: (0, 0)),
                  pl.BlockSpec((1, d, tn), lambda l, j: (l, 0, j)),
                  pl.BlockSpec((1, 1, tn), lambda l, j: (l, 0, j))],
        out_specs=pl.BlockSpec((1, bp, tn), lambda l, j: (l, 0, j)),
        compiler_params=_params("parallel", "parallel"),
    )(cp, ada_w, ada_b.reshape(depth, 1, n3))


def _norm_proj_kernel(x_ref, g_ref, sh_ref, sc_ref, w_ref, cs_ref, *rest, has_f, emit_h, sub):
    rest = list(rest)
    wf_ref = rest.pop(0) if has_f else None
    o_ref = rest.pop(0)
    hout_ref = rest.pop(0) if emit_h else None
    f_ref = rest.pop(0) if has_f else None
    h_ref = rest.pop(0)

    @pl.when(pl.program_id(1) == 0)
    def _():
        g = g_ref[...]
        sc = 1.0 + sc_ref[0]
        sh = sh_ref[0]
        for r in range(x_ref.shape[0] // sub):
            rows = slice(r * sub, (r + 1) * sub)
            x = x_ref[rows, :]
            y = x * lax.rsqrt(jnp.mean(x * x, axis=-1, keepdims=True) + EPS) * g
            h = (y * sc + sh).astype(BF16)
            h_ref[rows, :] = h
            if emit_h:
                hout_ref[rows, :] = h
            if has_f:
                f_ref[rows, :] = jnp.dot(h, wf_ref[...], preferred_element_type=F32)
            y = jnp.dot(h, w_ref[...], preferred_element_type=F32)
            o_ref[rows, :] = (y * cs_ref[...]).astype(o_ref.dtype)

    @pl.when(pl.program_id(1) > 0)
    def _():
        y = jnp.dot(h_ref[...], w_ref[...], preferred_element_type=F32)
        o_ref[...] = (y * cs_ref[...]).astype(o_ref.dtype)


def _norm_proj(x2, seq, g, mod3, w, col_blocks, col_scale, wf=None, emit_h=False, out_dtype=BF16,
               tm=512, tn=1024):
    t, d = x2.shape
    tm = min(tm, seq)
    nb = seq // tm
    nj = len(col_blocks)
    n = nj * tn
    first, skip_from, skip = col_blocks[0], None, 0
    for a, b2 in zip(col_blocks, col_blocks[1:]):
        if b2 != a + 1:
            skip_from, skip = a - first + 1, b2 - a - 1
    if skip_from is None:
        wmap = lambda i, j: (0, j + first)
    else:
        wmap = lambda i, j: (0, j + first + jnp.where(j >= skip_from, skip, 0))
    in_specs = [pl.BlockSpec((tm, d), lambda i, j: (i, 0)),
                pl.BlockSpec((1, d), lambda i, j: (0, 0)),
                pl.BlockSpec((1, 1, d), lambda i, j: (i // nb, 0, 0)),
                pl.BlockSpec((1, 1, d), lambda i, j: (i // nb, 0, 1)),
                pl.BlockSpec((d, tn), wmap),
                pl.BlockSpec((1, tn), lambda i, j: (0, j))]
    out_shape = [jax.ShapeDtypeStruct((t, n), out_dtype)]
    out_specs = [pl.BlockSpec((tm, tn), lambda i, j: (i, j))]
    args = [x2, g.reshape(1, d), mod3, mod3, w, col_scale.reshape(1, n).astype(F32)]
    if wf is not None:
        in_specs.append(pl.BlockSpec((d, LANES), lambda i, j: (0, 0)))
        args.append(wf)
    if emit_h:
        out_shape.append(jax.ShapeDtypeStruct((t, d), BF16))
        out_specs.append(pl.BlockSpec((tm, d), lambda i, j: (i, 0)))
    if wf is not None:
        out_shape.append(jax.ShapeDtypeStruct((t, LANES), F32))
        out_specs.append(pl.BlockSpec((tm, LANES), lambda i, j: (i, 0)))
    return pl.pallas_call(
        functools.partial(_norm_proj_kernel, has_f=wf is not None, emit_h=emit_h, sub=min(256, tm)),
        name="norm_proj",
        out_shape=out_shape,
        grid=(t // tm, nj),
        in_specs=in_specs,
        out_specs=out_specs,
        scratch_shapes=[pltpu.VMEM((tm, d), BF16)],
        compiler_params=_params("parallel", "arbitrary"),
    )(*args)


def _proj_t_kernel(w_ref, h_ref, o_ref):
    o_ref[...] = lax.dot_general(w_ref[...], h_ref[...], NT_DIMS,
                                 preferred_element_type=F32).astype(o_ref.dtype)


def _proj_t(w_t, h, tn=512, tm=1024):
    n, d = w_t.shape
    t = h.shape[0]
    tm = min(tm, t)
    return pl.pallas_call(
        _proj_t_kernel,
        name="proj_t",
        out_shape=jax.ShapeDtypeStruct((n, t), BF16),
        grid=(t // tm, n // tn),
        in_specs=[pl.BlockSpec((tn, d), lambda i, j: (j, 0)),
                  pl.BlockSpec((tm, d), lambda i, j: (i, 0))],
        out_specs=pl.BlockSpec((tn, tm), lambda i, j: (j, i)),
        compiler_params=_params("parallel", "arbitrary"),
    )(w_t, h)


def _out_proj_kernel(o_ref, w_ref, x_ref, gate_ref, g_ref, out_ref, *, sub):
    for r in range(o_ref.shape[0] // sub):
        rows = slice(r * sub, (r + 1) * sub)
        y = jnp.dot(o_ref[rows, :], w_ref[...], preferred_element_type=F32)
        yn = y * lax.rsqrt(jnp.mean(y * y, axis=-1, keepdims=True) + EPS) * g_ref[...]
        out_ref[rows, :] = x_ref[rows, :] + gate_ref[0] * yn


def _out_proj_residual(o2, w, x2, seq, mod3, g, tm=512):
    t, d = x2.shape
    tm = min(tm, seq)
    nb = seq // tm
    return pl.pallas_call(
        functools.partial(_out_proj_kernel, sub=min(256, tm)),
        name="out_proj",
        out_shape=jax.ShapeDtypeStruct((t, d), F32),
        grid=(t // tm,),
        in_specs=[pl.BlockSpec((tm, d), lambda i: (i, 0)),
                  pl.BlockSpec((d, d), lambda i: (0, 0), pipeline_mode=pl.Buffered(1)),
                  pl.BlockSpec((tm, d), lambda i: (i, 0)),
                  pl.BlockSpec((1, 1, d), lambda i: (i // nb, 0, 2)),
                  pl.BlockSpec((1, d), lambda i: (0, 0))],
        out_specs=pl.BlockSpec((tm, d), lambda i: (i, 0)),
        compiler_params=_params("parallel"),
    )(o2, w, x2, mod3, g.reshape(1, d))


def _flash_step_t(s_t, s_max, v_t, m_ref, l_ref, acc_ref, idx, cols):
    m_prev = m_ref[idx, :, cols]
    m_new = jnp.maximum(m_prev, s_max)
    alpha = jnp.exp2(m_prev - m_new)
    p_t = jnp.exp2(s_t - m_new)
    l_ref[idx, :, cols] = alpha * l_ref[idx, :, cols] + jnp.sum(p_t, axis=0, keepdims=True)
    acc_ref[idx, :, cols] = (alpha * acc_ref[idx, :, cols]
                             + jnp.dot(v_t, p_t.astype(BF16), preferred_element_type=F32))
    m_ref[idx, :, cols] = m_new


def _flash_init(m_ref, l_ref, acc_ref):
    m_ref[...] = jnp.full(m_ref.shape, NEG_INF, F32)
    l_ref[...] = jnp.zeros(l_ref.shape, F32)
    acc_ref[...] = jnp.zeros(acc_ref.shape, F32)


ATTN_TK = 512


def _diag_masked(s_t, granule):
    kg = lax.broadcasted_iota(jnp.int32, s_t.shape, 0) // granule
    qg = lax.broadcasted_iota(jnp.int32, s_t.shape, 1) // granule
    return jnp.where(kg <= qg, s_t, NEG_INF)


def _causal_blocks(nq, begin, scores, update, finish, buf0, buf1):
    every, late = slice(None), slice(ATTN_TK, 2 * ATTN_TK)
    begin(0)
    scores(0, buf0, every)

    def qblock(qi, carry):
        def pair(jj, c):
            j = 2 * jj
            scores(j + 1, buf1, every)
            update(j, buf0, False, every)
            scores(j + 2, buf0, every)
            update(j + 1, buf1, False, every)
            return c

        lax.fori_loop(0, qi, pair, 0)
        scores(2 * qi + 1, buf1, late)
        update(2 * qi, buf0, True, every)
        update(2 * qi + 1, buf1, True, late)
        finish(qi)
        begin(jnp.minimum(qi + 1, nq - 1))
        scores(0, buf0, every)
        return carry

    lax.fori_loop(0, nq, qblock, 0)


def _diff_attn_kernel(q_ref, k_ref, vt_ref, z_ref, lq1_ref, lk1_ref, lq2_ref, lk2_ref, g_ref,
                      o_ref, m_ref, l_ref, acc_ref, s0_ref, s1_ref, mx0_ref, mx1_ref, qc_ref,
                      *, nq, lambda_init):
    tq = 2 * ATTN_TK
    lam = (jnp.exp(jnp.sum(lq1_ref[...] * lk1_ref[...], axis=-1, keepdims=True))
           - jnp.exp(jnp.sum(lq2_ref[...] * lk2_ref[...], axis=-1, keepdims=True)) + lambda_init)
    g = g_ref[...] * (1.0 - lambda_init)

    def begin(qi):
        qc_ref[...] = q_ref[pl.ds(pl.multiple_of(qi * tq, tq), tq), :]
        _flash_init(m_ref, l_ref, acc_ref)

    def finish(qi):
        rows = pl.ds(pl.multiple_of(qi * tq, tq), tq)
        o_t = acc_ref[0] * (1.0 / l_ref[0]) - acc_ref[1] * (lam / l_ref[1])
        o = o_t.T
        on = o * lax.rsqrt(jnp.mean(o * o, axis=-1, keepdims=True) + EPS) * g
        o_ref[rows, :] = (on * _silu(z_ref[rows, :].astype(F32))).astype(o_ref.dtype)

    def scores(ki, buf, cols):
        s_ref, mx_ref = buf
        k = k_ref[pl.ds(pl.multiple_of(ki * ATTN_TK, ATTN_TK), ATTN_TK), :]
        for m in range(2):
            sl = slice(m * DIFF_QK_DIM, (m + 1) * DIFF_QK_DIM)
            s_t = lax.dot_general(k[:, sl], qc_ref[cols, sl], NT_DIMS, preferred_element_type=F32)
            s_ref[m, :, cols] = s_t
            mx_ref[m, :, cols] = jnp.max(s_t, axis=0, keepdims=True)

    def update(ki, buf, masked, cols):
        s_ref, mx_ref = buf
        v_t = vt_ref[:, pl.ds(pl.multiple_of(ki * ATTN_TK, ATTN_TK), ATTN_TK)]
        for m in range(2):
            s_t = s_ref[m, :, cols]
            s_max = mx_ref[m, :, cols]
            if masked:
                s_t = _diag_masked(s_t, CHUNK)
                s_max = jnp.max(s_t, axis=0, keepdims=True)
            _flash_step_t(s_t, s_max, v_t, m_ref, l_ref, acc_ref, m, cols)

    _causal_blocks(nq, begin, scores, update, finish, (s0_ref, mx0_ref), (s1_ref, mx1_ref))


def _diff_attention(qkz, v_t, batch, seq, lq1, lk1, lq2, lk2, subln_g, lambda_init):
    t = qkz.shape[0]
    tq = 2 * ATTN_TK
    nq = seq // tq
    h, dv = DIFF_HEADS, DIFF_V_DIM
    vec = lambda a: a.reshape(1, -1).astype(F32)
    small = lambda n: pl.BlockSpec((1, n), lambda b, hh: (0, 0))
    once = pl.Buffered(1)
    return pl.pallas_call(
        functools.partial(_diff_attn_kernel, nq=nq, lambda_init=lambda_init),
        name="diff_attn",
        out_shape=jax.ShapeDtypeStruct((t, h * dv), BF16),
        grid=(batch, h),
        in_specs=[pl.BlockSpec((seq, dv), lambda b, hh: (b, hh)),
                  pl.BlockSpec((seq, dv), lambda b, hh: (b, h + hh), pipeline_mode=once),
                  pl.BlockSpec((dv, seq), lambda b, hh: (hh, b), pipeline_mode=once),
                  pl.BlockSpec((seq, dv), lambda b, hh: (b, 2 * h + hh)),
                  small(DIFF_QK_DIM), small(DIFF_QK_DIM), small(DIFF_QK_DIM), small(DIFF_QK_DIM),
                  small(dv)],
        out_specs=pl.BlockSpec((seq, dv), lambda b, hh: (b, hh)),
        scratch_shapes=[pltpu.VMEM((2, 1, tq), F32), pltpu.VMEM((2, 1, tq), F32),
                        pltpu.VMEM((2, dv, tq), F32),
                        pltpu.VMEM((2, ATTN_TK, tq), F32), pltpu.VMEM((2, ATTN_TK, tq), F32),
                        pltpu.VMEM((2, 1, tq), F32), pltpu.VMEM((2, 1, tq), F32),
                        pltpu.VMEM((tq, dv), BF16)],
        compiler_params=_params("parallel", "parallel"),
    )(qkz, qkz, v_t, qkz, vec(lq1), vec(lk1), vec(lq2), vec(lk2), vec(subln_g))


FOX_PIECES = 3


def _fox_gate_kernel(f_ref, b_ref, o_ref):
    x = f_ref[...] + b_ref[...]
    lf = jnp.minimum(x, 0.0) - jnp.log1p(jnp.exp(-jnp.abs(x)))
    n = lf.shape[0]
    row = lax.broadcasted_iota(jnp.int32, lf.shape, 0)
    d = 1
    while d < n:
        lf = lf + jnp.where(row >= d, pltpu.roll(lf, d, 0), 0.0)
        d *= 2
    rest = lf * LOG2E
    lane = lax.broadcasted_iota(jnp.int32, lf.shape, 1)
    out = jnp.zeros(lf.shape, F32)
    for p in range(FOX_PIECES):
        piece = rest.astype(BF16).astype(F32)
        rest = rest - piece
        moved = piece if p == 0 else pltpu.roll(piece, p * FOX_HEADS, 1)
        out = jnp.where((lane >= p * FOX_HEADS) & (lane < (p + 1) * FOX_HEADS), moved, out)
    o_ref[...] = out.astype(BF16)


def _fox_gates(f_logit, bias, batch, seq):
    bpad = jnp.zeros((1, LANES), F32).at[0, :FOX_HEADS].set(bias.astype(F32))
    return pl.pallas_call(
        _fox_gate_kernel,
        name="fox_gates",
        out_shape=jax.ShapeDtypeStruct((batch * seq, LANES), BF16),
        grid=(batch,),
        in_specs=[pl.BlockSpec((seq, LANES), lambda b: (b, 0)),
                  pl.BlockSpec((1, LANES), lambda b: (0, 0))],
        out_specs=pl.BlockSpec((seq, LANES), lambda b: (b, 0)),
        compiler_params=_params("parallel"),
    )(f_logit, bpad)


def _fox_place_kernel(f_ref, pk_ref, pq_ref, ck_ref, cq_ref, kx_ref, qx_ref):
    f = f_ref[...]
    kx_ref[...] = (jnp.dot(f, pk_ref[...], preferred_element_type=F32) + ck_ref[...]).astype(BF16)
    qx_ref[...] = (jnp.dot(f, pq_ref[...], preferred_element_type=F32) + cq_ref[...]).astype(BF16)


def _fox_extend(pieces, tm=1024):
    t = pieces.shape[0]
    tm = min(tm, t)
    n = FOX_HEADS * LANES
    src = jnp.arange(LANES)[:, None]
    dst = jnp.arange(n)[None, :]
    p_src, h_src = src // FOX_HEADS, src % FOX_HEADS
    h_dst, c_dst = dst // LANES, dst % LANES
    live = (p_src < FOX_PIECES) & (h_src == h_dst)
    pk = jnp.where(live & (c_dst == p_src), -1.0, 0.0).astype(BF16)
    pq = jnp.where(live & (c_dst == p_src + FOX_PIECES), 1.0, 0.0).astype(BF16)
    ck = ((c_dst >= FOX_PIECES) & (c_dst < 2 * FOX_PIECES)).astype(F32)
    cq = (c_dst < FOX_PIECES).astype(F32)
    mat = pl.BlockSpec((LANES, n), lambda i: (0, 0))
    vec = pl.BlockSpec((1, n), lambda i: (0, 0))
    return pl.pallas_call(
        _fox_place_kernel,
        name="fox_extend",
        out_shape=[jax.ShapeDtypeStruct((t, n), BF16), jax.ShapeDtypeStruct((t, n), BF16)],
        grid=(t // tm,),
        in_specs=[pl.BlockSpec((tm, LANES), lambda i: (i, 0)), mat, mat, vec, vec],
        out_specs=[pl.BlockSpec((tm, n), lambda i: (i, 0)), pl.BlockSpec((tm, n), lambda i: (i, 0))],
        compiler_params=_params("parallel"),
    )(pieces, pk, pq, ck, cq)


def _fox_attn_kernel(q_ref, qx_ref, k_ref, kx_ref, vt_ref, z_ref, o_ref, m_ref, l_ref, acc_ref,
                     s0_ref, s1_ref, mx0_ref, mx1_ref, qt_ref, *, nq):
    tq = 2 * ATTN_TK
    dh = FOX_HEAD_DIM

    def begin(qi):
        rows = pl.ds(pl.multiple_of(qi * tq, tq), tq)
        qt_ref[:dh, :] = q_ref[rows, :].T
        qt_ref[dh:, :] = qx_ref[rows, :].T
        _flash_init(m_ref, l_ref, acc_ref)

    def finish(qi):
        rows = pl.ds(pl.multiple_of(qi * tq, tq), tq)
        o = (acc_ref[0] * (1.0 / l_ref[0])).T
        o_ref[rows, :] = (o * _silu(z_ref[rows, :].astype(F32))).astype(o_ref.dtype)

    def scores(ki, buf, cols):
        s_ref, mx_ref = buf
        off = pl.multiple_of(ki * ATTN_TK, ATTN_TK)
        k = jnp.concatenate([k_ref[pl.ds(off, ATTN_TK), :], kx_ref[pl.ds(off, ATTN_TK), :]], axis=1)
        s_t = jnp.dot(k, qt_ref[:, cols], preferred_element_type=F32)
        s_ref[:, cols] = s_t
        mx_ref[:, cols] = jnp.max(s_t, axis=0, keepdims=True)

    def update(ki, buf, masked, cols):
        s_ref, mx_ref = buf
        v_t = vt_ref[:, pl.ds(pl.multiple_of(ki * ATTN_TK, ATTN_TK), ATTN_TK)]
        s_t = s_ref[:, cols]
        s_max = mx_ref[:, cols]
        if masked:
            s_t = _diag_masked(s_t, 1)
            s_max = jnp.max(s_t, axis=0, keepdims=True)
        _flash_step_t(s_t, s_max, v_t, m_ref, l_ref, acc_ref, 0, cols)

    _causal_blocks(nq, begin, scores, update, finish, (s0_ref, mx0_ref), (s1_ref, mx1_ref))


def _fox_attention(qkz, v_t, kx, qx, batch, seq):
    t = qkz.shape[0]
    tq = 2 * ATTN_TK
    nq = seq // tq
    h, dh = FOX_HEADS, FOX_HEAD_DIM
    return pl.pallas_call(
        functools.partial(_fox_attn_kernel, nq=nq),
        name="fox_attn",
        out_shape=jax.ShapeDtypeStruct((t, h * dh), BF16),
        grid=(batch, h),
        in_specs=[pl.BlockSpec((seq, dh), lambda b, hh: (b, hh)),
                  pl.BlockSpec((seq, LANES), lambda b, hh: (b, hh)),
                  pl.BlockSpec((seq, dh), lambda b, hh: (b, h + hh)),
                  pl.BlockSpec((seq, LANES), lambda b, hh: (b, hh)),
                  pl.BlockSpec((dh, seq), lambda b, hh: (hh, b)),
                  pl.BlockSpec((seq, dh), lambda b, hh: (b, 2 * h + hh))],
        out_specs=pl.BlockSpec((seq, dh), lambda b, hh: (b, hh)),
        scratch_shapes=[pltpu.VMEM((1, 1, tq), F32), pltpu.VMEM((1, 1, tq), F32),
                        pltpu.VMEM((1, dh, tq), F32),
                        pltpu.VMEM((ATTN_TK, tq), F32), pltpu.VMEM((ATTN_TK, tq), F32),
                        pltpu.VMEM((1, tq), F32), pltpu.VMEM((1, tq), F32),
                        pltpu.VMEM((dh + LANES, tq), BF16)],
        compiler_params=_params("parallel", "parallel"),
    )(qkz, qx, qkz, kx, v_t, qkz)


BAND_TQ = 256
BAND_WIN = BAND_PAST + BAND_TQ
BAND_HPB = LANES // BAND_HEAD_DIM
BAND_REV = 1024


def _band_bias_kernel(row_ref, o_ref):
    x = jnp.broadcast_to(row_ref[0], (BAND_WIN, BAND_REV))
    x = pltpu.roll(x, BAND_REV - BAND_WIN, 1, stride=1, stride_axis=0)
    x = x[:, :BAND_TQ]
    kc = lax.broadcasted_iota(jnp.int32, x.shape, 0) // CHUNK
    qc = lax.broadcasted_iota(jnp.int32, x.shape, 1) // CHUNK
    band = (kc >= qc) & (kc <= qc + BAND_PAST // CHUNK)
    o_ref[0] = jnp.where(band, x * LOG2E, NEG_INF)


def _band_bias(rel_table):
    nh = rel_table.shape[0]
    t = rel_table.astype(F32)
    row = jnp.concatenate([t, jnp.broadcast_to(t[:, 2 * REL_CLIP:], (nh, BAND_REV - t.shape[1]))], axis=1)
    return pl.pallas_call(
        _band_bias_kernel,
        name="band_bias",
        out_shape=jax.ShapeDtypeStruct((nh, BAND_WIN, BAND_TQ), F32),
        grid=(nh,),
        in_specs=[pl.BlockSpec((1, 1, BAND_REV), lambda h: (h, 0, 0))],
        out_specs=pl.BlockSpec((1, BAND_WIN, BAND_TQ), lambda h: (h, 0, 0)),
        compiler_params=_params("parallel"),
    )(row.reshape(nh, 1, BAND_REV))


def _band_attn_kernel(q_ref, k_ref, vt_ref, z_ref, bias_ref, o_ref, kp_ref, vtp_ref, s0_ref, s1_ref,
                      mx0_ref, mx1_ref, *, nq):
    hd = BAND_HEAD_DIM
    for hh in range(BAND_HPB):
        kp_ref[hh, :BAND_PAST, :] = jnp.zeros((BAND_PAST, hd), BF16)
        kp_ref[hh, BAND_PAST:, :] = k_ref[:, hh * hd:(hh + 1) * hd]
    vtp_ref[:, :BAND_PAST] = jnp.zeros((LANES, BAND_PAST), BF16)
    vtp_ref[:, BAND_PAST:] = vt_ref[...]

    def scores(j, buf):
        s_ref, mx_ref = buf
        start = pl.multiple_of(j * BAND_TQ, BAND_TQ)
        q = q_ref[pl.ds(start, BAND_TQ), :]
        for hh in range(BAND_HPB):
            s_t = lax.dot_general(kp_ref[hh, pl.ds(start, BAND_WIN), :], q[:, hh * hd:(hh + 1) * hd],
                                  NT_DIMS, preferred_element_type=F32) + bias_ref[hh]
            s_ref[hh] = s_t
            mx_ref[hh] = jnp.max(s_t, axis=0, keepdims=True)

    def update(j, buf, masked):
        s_ref, mx_ref = buf
        start = pl.multiple_of(j * BAND_TQ, BAND_TQ)
        outs = []
        for hh in range(BAND_HPB):
            s_t = s_ref[hh]
            s_max = mx_ref[hh]
            if masked:
                kpos = start - BAND_PAST + lax.broadcasted_iota(jnp.int32, (BAND_WIN, 1), 0)
                s_t = jnp.where(kpos >= 0, s_t, NEG_INF)
                s_max = jnp.max(s_t, axis=0, keepdims=True)
            p_t = jnp.exp2(s_t - s_max)
            l = jnp.sum(p_t, axis=0, keepdims=True)
            v_t = vtp_ref[hh * hd:(hh + 1) * hd, pl.ds(start, BAND_WIN)]
            outs.append(jnp.dot(v_t, p_t.astype(BF16), preferred_element_type=F32) * (1.0 / l))
        o = jnp.concatenate(outs, axis=0).T
        z = z_ref[pl.ds(start, BAND_TQ), :].astype(F32)
        o_ref[pl.ds(start, BAND_TQ), :] = (o * _silu(z)).astype(o_ref.dtype)

    buf0, buf1 = (s0_ref, mx0_ref), (s1_ref, mx1_ref)
    scores(0, buf0)
    scores(1, buf1)
    update(0, buf0, True)
    scores(2, buf0)
    update(1, buf1, True)

    def pair(jj, carry):
        j = 2 + 2 * jj
        scores(j + 1, buf1)
        update(j, buf0, False)
        scores(j + 2, buf0)
        update(j + 1, buf1, False)
        return carry

    lax.fori_loop(0, (nq - 4) // 2, pair, 0)
    scores(nq - 1, buf1)
    update(nq - 2, buf0, False)
    update(nq - 1, buf1, False)


def _band_attention(qkz, v_t, bias_t, batch, seq):
    t = qkz.shape[0]
    nq = seq // BAND_TQ
    assert nq >= 4 and nq % 2 == 0
    hp = BAND_HEADS // BAND_HPB
    return pl.pallas_call(
        functools.partial(_band_attn_kernel, nq=nq),
        name="band_attn",
        out_shape=jax.ShapeDtypeStruct((t, BAND_HEADS * BAND_HEAD_DIM), BF16),
        grid=(batch, hp),
        in_specs=[pl.BlockSpec((seq, LANES), lambda b, g: (b, g)),
                  pl.BlockSpec((seq, LANES), lambda b, g: (b, hp + g)),
                  pl.BlockSpec((LANES, seq), lambda b, g: (g, b)),
                  pl.BlockSpec((seq, LANES), lambda b, g: (b, 2 * hp + g)),
                  pl.BlockSpec((BAND_HPB, BAND_WIN, BAND_TQ), lambda b, g: (g, 0, 0))],
        out_specs=pl.BlockSpec((seq, LANES), lambda b, g: (b, g)),
        scratch_shapes=[pltpu.VMEM((BAND_HPB, seq + BAND_PAST, BAND_HEAD_DIM), BF16),
                        pltpu.VMEM((LANES, seq + BAND_PAST), BF16),
                        pltpu.VMEM((BAND_HPB, BAND_WIN, BAND_TQ), F32),
                        pltpu.VMEM((BAND_HPB, BAND_WIN, BAND_TQ), F32),
                        pltpu.VMEM((BAND_HPB, 1, BAND_TQ), F32),
                        pltpu.VMEM((BAND_HPB, 1, BAND_TQ), F32)],
        compiler_params=_params("parallel", "parallel"),
    )(qkz, qkz, v_t, qkz, bias_t)


def _pool_kernel(u_ref, up_ref, z_ref, w_ref, ps_ref, o_ref, *, tm, nb):
    i = pl.program_id(0)
    t0 = (i % nb) * tm
    has_history = t0 > 0
    t = t0 + lax.broadcasted_iota(jnp.int32, (tm, 1), 0)
    for g, win in enumerate(POOL_WINDOWS):
        sl = slice(g * POOL_GROUP, (g + 1) * POOL_GROUP)
        u = u_ref[:, sl]
        acc = jnp.concatenate([jnp.where(has_history, up_ref[:, sl], 0.0), u], axis=0)
        d = 1
        while d < win:
            acc = acc + pltpu.roll(acc, d, 0)
            d *= 2
        wsum = acc[POOL_HALO:, :]
        cnt = jnp.minimum(t + 1, win).astype(F32)
        delta = (wsum / cnt - u).astype(BF16)
        y = jnp.dot(delta, w_ref[g], preferred_element_type=F32) * ps_ref[:, sl]
        o_ref[:, sl] = (y * _silu(z_ref[:, sl])).astype(o_ref.dtype)


def _pool_mix(uz, seq, pool_w, pool_scale, tm=256):
    t = uz.shape[0]
    d = D_MODEL
    tm = min(tm, seq)
    nb = seq // tm
    hb = tm // POOL_HALO
    return pl.pallas_call(
        functools.partial(_pool_kernel, tm=tm, nb=nb),
        name="pool_mix",
        out_shape=jax.ShapeDtypeStruct((t, d), BF16),
        grid=(t // tm,),
        in_specs=[pl.BlockSpec((tm, d), lambda i: (i, 0)),
                  pl.BlockSpec((POOL_HALO, d), lambda i: (jnp.maximum(i * hb - 1, 0), 0)),
                  pl.BlockSpec((tm, d), lambda i: (i, 1)),
                  pl.BlockSpec((len(POOL_WINDOWS), POOL_GROUP, POOL_GROUP), lambda i: (0, 0, 0)),
                  pl.BlockSpec((1, d), lambda i: (0, 0))],
        out_specs=pl.BlockSpec((tm, d), lambda i: (i, 0)),
        compiler_params=_params("parallel"),
    )(uz, uz, uz, pool_w, pool_scale.reshape(1, d).astype(F32))


PROJ_TN = 1024


def _qkz_scale(qk_dim, branch):
    return jnp.concatenate([jnp.full((branch,), qk_dim ** -0.5 * LOG2E, F32), jnp.ones((2 * branch,), F32)])


def kernel(x, c, ada_w, ada_b, norm_pre, norm_post, diff_w_in, diff_w_out, diff_lambda_q1, diff_lambda_k1, diff_lambda_q2, diff_lambda_k2, diff_subln, band_w_in, band_w_out, band_rel_bias, fox_w_in, fox_w_out, fox_forget_bias, pool_w_in, pool_w_out, pool_group_w, pool_scale):
    batch, seq, d = x.shape
    depth = ada_w.shape[0]
    branch = d
    nblk = branch // PROJ_TN
    qkz_blocks = list(range(2 * nblk)) + list(range(3 * nblk, 4 * nblk))
    mod = _ada_mod(c, ada_w, ada_b)
    x2 = x.reshape(batch * seq, d)
    for i in range(depth):
        mod3 = mod[i].reshape(mod.shape[1], 1, 3 * d)
        kind = i % 4
        if kind == 0:
            lambda_init = 0.8 - 0.6 * math.exp(-0.3 * i)
            qkz, h = _norm_proj(x2, seq, norm_pre[i], mod3, diff_w_in.astype(BF16), qkz_blocks,
                                _qkz_scale(DIFF_QK_DIM, branch), emit_h=True)
            v_t = _proj_t(diff_w_in[:, 2 * branch:3 * branch].T.astype(BF16), h)
            o = _diff_attention(qkz, v_t, batch, seq, diff_lambda_q1, diff_lambda_k1, diff_lambda_q2,
                                diff_lambda_k2, diff_subln, lambda_init)
            w_out = diff_w_out
        elif kind == 1:
            qkz, h = _norm_proj(x2, seq, norm_pre[i], mod3, band_w_in.astype(BF16), qkz_blocks,
                                _qkz_scale(BAND_HEAD_DIM, branch), emit_h=True)
            v_t = _proj_t(band_w_in[:, 2 * branch:3 * branch].T.astype(BF16), h)
            o = _band_attention(qkz, v_t, _band_bias(band_rel_bias), batch, seq)
            w_out = band_w_out
        elif kind == 2:
            wf = jnp.pad(fox_w_in[:, 4 * branch:], ((0, 0), (0, LANES - FOX_HEADS))).astype(BF16)
            qkz, h, f_logit = _norm_proj(x2, seq, norm_pre[i], mod3, fox_w_in.astype(BF16), qkz_blocks,
                                         _qkz_scale(FOX_HEAD_DIM, branch), wf=wf, emit_h=True)
            v_t = _proj_t(fox_w_in[:, 2 * branch:3 * branch].T.astype(BF16), h)
            kx, qx = _fox_extend(_fox_gates(f_logit, fox_forget_bias, batch, seq))
            o = _fox_attention(qkz, v_t, kx, qx, batch, seq)
            w_out = fox_w_out
        else:
            uz, = _norm_proj(x2, seq, norm_pre[i], mod3, pool_w_in.astype(BF16), list(range(2 * nblk)),
                             jnp.ones((2 * branch,), F32), out_dtype=F32)
            o = _pool_mix(uz, seq, pool_group_w.astype(BF16), pool_scale)
            w_out = pool_w_out
        x2 = _out_proj_residual(o, w_out.astype(BF16), x2, seq, mod3, norm_post[i])
    return x2.reshape(batch, seq, d)
```

```python
import functools
import math

import jax
import jax.numpy as jnp
from jax import lax
from jax.experimental import pallas as pl
from jax.experimental.pallas import tpu as pltpu

F32 = jnp.float32
BF16 = jnp.bfloat16

D_MODEL = 2048
EPS = 1e-6
NEG_INF = -1e30
LOG2E = math.log2(math.e)
CHUNK = 64
DIFF_HEADS, DIFF_V_DIM, DIFF_QK_DIM = 8, 256, 128
BAND_HEADS, BAND_HEAD_DIM, BAND_PAST, REL_CLIP = 32, 64, 512, 256
FOX_HEADS, FOX_HEAD_DIM = 16, 128
POOL_WINDOWS = (2, 4, 8, 16)
POOL_GROUP = D_MODEL // len(POOL_WINDOWS)
POOL_HALO = 16

LANES = 128
VMEM_LIMIT = 52 * 1024 * 1024

NT_DIMS = (((1,), (1,)), ((), ()))


def _silu(x):
    return x / (1.0 + jnp.exp(-x))


def _params(*sem):
    return pltpu.CompilerParams(dimension_semantics=sem, vmem_limit_bytes=VMEM_LIMIT)


def _ada_kernel(c_ref, w_ref, b_ref, o_ref):
    ca = _silu(c_ref[...]).astype(BF16)
    o_ref[0] = jnp.dot(ca, w_ref[0].astype(BF16), preferred_element_type=F32) + b_ref[0]


def _ada_mod(c, ada_w, ada_b):
    b, d = c.shape
    bp = -(-b // 8) * 8
    depth, _, n3 = ada_w.shape
    tn = 768
    cp = jnp.pad(c, ((0, bp - b), (0, 0)))
    return pl.pallas_call(
        _ada_kernel,
        name="ada_mod",
        out_shape=jax.ShapeDtypeStruct((depth, bp, n3), F32),
        grid=(depth, n3 // tn),
        in_specs=[pl.BlockSpec((bp, d), lambda l, j: (0, 0)),
                  pl.BlockSpec((1, d, tn), lambda l, j: (l, 0, j)),
                  pl.BlockSpec((1, 1, tn), lambda l, j: (l, 0, j))],
        out_specs=pl.BlockSpec((1, bp, tn), lambda l, j: (l, 0, j)),
        compiler_params=_params("parallel", "parallel"),
    )(cp, ada_w, ada_b.reshape(depth, 1, n3))


def _norm_proj_kernel(x_ref, g_ref, sh_ref, sc_ref, w_ref, cs_ref, *rest, has_f, emit_h, sub):
    rest = list(rest)
    wf_ref = rest.pop(0) if has_f else None
    o_ref = rest.pop(0)
    hout_ref = rest.pop(0) if emit_h else None
    f_ref = rest.pop(0) if has_f else None
    h_ref = rest.pop(0)

    @pl.when(pl.program_id(1) == 0)
    def _():
        g = g_ref[...]
        sc = 1.0 + sc_ref[0]
        sh = sh_ref[0]
        for r in range(x_ref.shape[0] // sub):
            rows = slice(r * sub, (r + 1) * sub)
            x = x_ref[rows, :]
            y = x * lax.rsqrt(jnp.mean(x * x, axis=-1, keepdims=True) + EPS) * g
            h = (y * sc + sh).astype(BF16)
            h_ref[rows, :] = h
            if emit_h:
                hout_ref[rows, :] = h
            if has_f:
                f_ref[rows, :] = jnp.dot(h, wf_ref[...], preferred_element_type=F32)
            y = jnp.dot(h, w_ref[...], preferred_element_type=F32)
            o_ref[rows, :] = (y * cs_ref[...]).astype(o_ref.dtype)

    @pl.when(pl.program_id(1) > 0)
    def _():
        y = jnp.dot(h_ref[...], w_ref[...], preferred_element_type=F32)
        o_ref[...] = (y * cs_ref[...]).astype(o_ref.dtype)


def _norm_proj(x2, seq, g, mod3, w, col_blocks, col_scale, wf=None, emit_h=False, out_dtype=BF16,
               tm=1024, tn=1024):
    t, d = x2.shape
    tm = min(tm, seq)
    nb = seq // tm
    nj = len(col_blocks)
    n = nj * tn
    first, skip_from, skip = col_blocks[0], None, 0
    for a, b2 in zip(col_blocks, col_blocks[1:]):
        if b2 != a + 1:
            skip_from, skip = a - first + 1, b2 - a - 1
    if skip_from is None:
        wmap = lambda i, j: (0, j + first)
    else:
        wmap = lambda i, j: (0, j + first + jnp.where(j >= skip_from, skip, 0))
    in_specs = [pl.BlockSpec((tm, d), lambda i, j: (i, 0)),
                pl.BlockSpec((1, d), lambda i, j: (0, 0)),
                pl.BlockSpec((1, 1, d), lambda i, j: (i // nb, 0, 0)),
                pl.BlockSpec((1, 1, d), lambda i, j: (i // nb, 0, 1)),
                pl.BlockSpec((d, tn), wmap),
                pl.BlockSpec((1, tn), lambda i, j: (0, j))]
    out_shape = [jax.ShapeDtypeStruct((t, n), out_dtype)]
    out_specs = [pl.BlockSpec((tm, tn), lambda i, j: (i, j))]
    args = [x2, g.reshape(1, d), mod3, mod3, w, col_scale.reshape(1, n).astype(F32)]
    if wf is not None:
        in_specs.append(pl.BlockSpec((d, LANES), lambda i, j: (0, 0)))
        args.append(wf)
    if emit_h:
        out_shape.append(jax.ShapeDtypeStruct((t, d), BF16))
        out_specs.append(pl.BlockSpec((tm, d), lambda i, j: (i, 0)))
    if wf is not None:
        out_shape.append(jax.ShapeDtypeStruct((t, LANES), F32))
        out_specs.append(pl.BlockSpec((tm, LANES), lambda i, j: (i, 0)))
    return pl.pallas_call(
        functools.partial(_norm_proj_kernel, has_f=wf is not None, emit_h=emit_h, sub=min(256, tm)),
        name="norm_proj",
        out_shape=out_shape,
        grid=(t // tm, nj),
        in_specs=in_specs,
        out_specs=out_specs,
        scratch_shapes=[pltpu.VMEM((tm, d), BF16)],
        compiler_params=_params("parallel", "arbitrary"),
    )(*args)


def _proj_t_kernel(w_ref, h_ref, o_ref):
    o_ref[...] = lax.dot_general(w_ref[...], h_ref[...], NT_DIMS,
                                 preferred_element_type=F32).astype(o_ref.dtype)


def _proj_t(w_t, h, tn=1024, tm=1024):
    n, d = w_t.shape
    t = h.shape[0]
    tm = min(tm, t)
    return pl.pallas_call(
        _proj_t_kernel,
        name="proj_t",
        out_shape=jax.ShapeDtypeStruct((n, t), BF16),
        grid=(t // tm, n // tn),
        in_specs=[pl.BlockSpec((tn, d), lambda i, j: (j, 0)),
                  pl.BlockSpec((tm, d), lambda i, j: (i, 0))],
        out_specs=pl.BlockSpec((tn, tm), lambda i, j: (j, i)),
        compiler_params=_params("parallel", "arbitrary"),
    )(w_t, h)


def _out_proj_kernel(o_ref, w_ref, x_ref, gate_ref, g_ref, out_ref, *, sub):
    for r in range(o_ref.shape[0] // sub):
        rows = slice(r * sub, (r + 1) * sub)
        y = jnp.dot(o_ref[rows, :], w_ref[...], preferred_element_type=F32)
        yn = y * lax.rsqrt(jnp.mean(y * y, axis=-1, keepdims=True) + EPS) * g_ref[...]
        out_ref[rows, :] = x_ref[rows, :] + gate_ref[0] * yn


def _out_proj_residual(o2, w, x2, seq, mod3, g, tm=512):
    t, d = x2.shape
    tm = min(tm, seq)
    nb = seq // tm
    return pl.pallas_call(
        functools.partial(_out_proj_kernel, sub=min(256, tm)),
        name="out_proj",
        out_shape=jax.ShapeDtypeStruct((t, d), F32),
        grid=(t // tm,),
        in_specs=[pl.BlockSpec((tm, d), lambda i: (i, 0)),
                  pl.BlockSpec((d, d), lambda i: (0, 0), pipeline_mode=pl.Buffered(1)),
                  pl.BlockSpec((tm, d), lambda i: (i, 0)),
                  pl.BlockSpec((1, 1, d), lambda i: (i // nb, 0, 2)),
                  pl.BlockSpec((1, d), lambda i: (0, 0))],
        out_specs=pl.BlockSpec((tm, d), lambda i: (i, 0)),
        compiler_params=_params("parallel"),
    )(o2, w, x2, mod3, g.reshape(1, d))


def _flash_step_t(s_t, s_max, v_t, m_ref, l_ref, acc_ref, idx, cols):
    m_prev = m_ref[idx, :, cols]
    m_new = jnp.maximum(m_prev, s_max)
    alpha = jnp.exp2(m_prev - m_new)
    p_t = jnp.exp2(s_t - m_new)
    l_ref[idx, :, cols] = alpha * l_ref[idx, :, cols] + jnp.sum(p_t, axis=0, keepdims=True)
    acc_ref[idx, :, cols] = (alpha * acc_ref[idx, :, cols]
                             + jnp.dot(v_t, p_t.astype(BF16), preferred_element_type=F32))
    m_ref[idx, :, cols] = m_new


def _flash_init(m_ref, l_ref, acc_ref):
    m_ref[...] = jnp.full(m_ref.shape, NEG_INF, F32)
    l_ref[...] = jnp.zeros(l_ref.shape, F32)
    acc_ref[...] = jnp.zeros(acc_ref.shape, F32)


ATTN_TK = 512


def _diag_masked(s_t, granule):
    kg = lax.broadcasted_iota(jnp.int32, s_t.shape, 0) // granule
    qg = lax.broadcasted_iota(jnp.int32, s_t.shape, 1) // granule
    return jnp.where(kg <= qg, s_t, NEG_INF)


def _causal_blocks(nq, begin, scores, update, finish, buf0, buf1):
    every, late = slice(None), slice(ATTN_TK, 2 * ATTN_TK)
    begin(0)
    scores(0, buf0, every)

    def qblock(qi, carry):
        def pair(jj, c):
            j = 2 * jj
            scores(j + 1, buf1, every)
            update(j, buf0, False, every)
            scores(j + 2, buf0, every)
            update(j + 1, buf1, False, every)
            return c

        lax.fori_loop(0, qi, pair, 0)
        scores(2 * qi + 1, buf1, late)
        update(2 * qi, buf0, True, every)
        update(2 * qi + 1, buf1, True, late)
        finish(qi)
        begin(jnp.minimum(qi + 1, nq - 1))
        scores(0, buf0, every)
        return carry

    lax.fori_loop(0, nq, qblock, 0)


def _diff_attn_kernel(q_ref, k_ref, vt_ref, z_ref, lq1_ref, lk1_ref, lq2_ref, lk2_ref, g_ref,
                      o_ref, m_ref, l_ref, acc_ref, s0_ref, s1_ref, mx0_ref, mx1_ref, qc_ref,
                      *, nq, lambda_init):
    tq = 2 * ATTN_TK
    lam = (jnp.exp(jnp.sum(lq1_ref[...] * lk1_ref[...], axis=-1, keepdims=True))
           - jnp.exp(jnp.sum(lq2_ref[...] * lk2_ref[...], axis=-1, keepdims=True)) + lambda_init)
    g = g_ref[...] * (1.0 - lambda_init)

    def begin(qi):
        qc_ref[...] = q_ref[pl.ds(pl.multiple_of(qi * tq, tq), tq), :]
        _flash_init(m_ref, l_ref, acc_ref)

    def finish(qi):
        rows = pl.ds(pl.multiple_of(qi * tq, tq), tq)
        o_t = acc_ref[0] * (1.0 / l_ref[0]) - acc_ref[1] * (lam / l_ref[1])
        o = o_t.T
        on = o * lax.rsqrt(jnp.mean(o * o, axis=-1, keepdims=True) + EPS) * g
        o_ref[rows, :] = (on * _silu(z_ref[rows, :].astype(F32))).astype(o_ref.dtype)

    def scores(ki, buf, cols):
        s_ref, mx_ref = buf
        k = k_ref[pl.ds(pl.multiple_of(ki * ATTN_TK, ATTN_TK), ATTN_TK), :]
        for m in range(2):
            sl = slice(m * DIFF_QK_DIM, (m + 1) * DIFF_QK_DIM)
            s_t = lax.dot_general(k[:, sl], qc_ref[cols, sl], NT_DIMS, preferred_element_type=F32)
            s_ref[m, :, cols] = s_t
            mx_ref[m, :, cols] = jnp.max(s_t, axis=0, keepdims=True)

    def update(ki, buf, masked, cols):
        s_ref, mx_ref = buf
        v_t = vt_ref[:, pl.ds(pl.multiple_of(ki * ATTN_TK, ATTN_TK), ATTN_TK)]
        for m in range(2):
            s_t = s_ref[m, :, cols]
            s_max = mx_ref[m, :, cols]
            if masked:
                s_t = _diag_masked(s_t, CHUNK)
                s_max = jnp.max(s_t, axis=0, keepdims=True)
            _flash_step_t(s_t, s_max, v_t, m_ref, l_ref, acc_ref, m, cols)

    _causal_blocks(nq, begin, scores, update, finish, (s0_ref, mx0_ref), (s1_ref, mx1_ref))


def _diff_attention(qkz, v_t, batch, seq, lq1, lk1, lq2, lk2, subln_g, lambda_init):
    t = qkz.shape[0]
    tq = 2 * ATTN_TK
    nq = seq // tq
    h, dv = DIFF_HEADS, DIFF_V_DIM
    vec = lambda a: a.reshape(1, -1).astype(F32)
    small = lambda n: pl.BlockSpec((1, n), lambda b, hh: (0, 0))
    once = pl.Buffered(1)
    return pl.pallas_call(
        functools.partial(_diff_attn_kernel, nq=nq, lambda_init=lambda_init),
        name="diff_attn",
        out_shape=jax.ShapeDtypeStruct((t, h * dv), BF16),
        grid=(batch, h),
        in_specs=[pl.BlockSpec((seq, dv), lambda b, hh: (b, hh)),
                  pl.BlockSpec((seq, dv), lambda b, hh: (b, h + hh), pipeline_mode=once),
                  pl.BlockSpec((dv, seq), lambda b, hh: (hh, b), pipeline_mode=once),
                  pl.BlockSpec((seq, dv), lambda b, hh: (b, 2 * h + hh)),
                  small(DIFF_QK_DIM), small(DIFF_QK_DIM), small(DIFF_QK_DIM), small(DIFF_QK_DIM),
                  small(dv)],
        out_specs=pl.BlockSpec((seq, dv), lambda b, hh: (b, hh)),
        scratch_shapes=[pltpu.VMEM((2, 1, tq), F32), pltpu.VMEM((2, 1, tq), F32),
                        pltpu.VMEM((2, dv, tq), F32),
                        pltpu.VMEM((2, ATTN_TK, tq), F32), pltpu.VMEM((2, ATTN_TK, tq), F32),
                        pltpu.VMEM((2, 1, tq), F32), pltpu.VMEM((2, 1, tq), F32),
                        pltpu.VMEM((tq, dv), BF16)],
        compiler_params=_params("parallel", "parallel"),
    )(qkz, qkz, v_t, qkz, vec(lq1), vec(lk1), vec(lq2), vec(lk2), vec(subln_g))


FOX_PIECES = 3


def _fox_gate_kernel(f_ref, b_ref, o_ref):
    x = f_ref[...] + b_ref[...]
    lf = jnp.minimum(x, 0.0) - jnp.log1p(jnp.exp(-jnp.abs(x)))
    n = lf.shape[0]
    row = lax.broadcasted_iota(jnp.int32, lf.shape, 0)
    d = 1
    while d < n:
        lf = lf + jnp.where(row >= d, pltpu.roll(lf, d, 0), 0.0)
        d *= 2
    rest = lf * LOG2E
    lane = lax.broadcasted_iota(jnp.int32, lf.shape, 1)
    out = jnp.zeros(lf.shape, F32)
    for p in range(FOX_PIECES):
        piece = rest.astype(BF16).astype(F32)
        rest = rest - piece
        moved = piece if p == 0 else pltpu.roll(piece, p * FOX_HEADS, 1)
        out = jnp.where((lane >= p * FOX_HEADS) & (lane < (p + 1) * FOX_HEADS), moved, out)
    o_ref[...] = out.astype(BF16)


def _fox_gates(f_logit, bias, batch, seq):
    bpad = jnp.zeros((1, LANES), F32).at[0, :FOX_HEADS].set(bias.astype(F32))
    return pl.pallas_call(
        _fox_gate_kernel,
        name="fox_gates",
        out_shape=jax.ShapeDtypeStruct((batch * seq, LANES), BF16),
        grid=(batch,),
        in_specs=[pl.BlockSpec((seq, LANES), lambda b: (b, 0)),
                  pl.BlockSpec((1, LANES), lambda b: (0, 0))],
        out_specs=pl.BlockSpec((seq, LANES), lambda b: (b, 0)),
        compiler_params=_params("parallel"),
    )(f_logit, bpad)


def _fox_place_kernel(f_ref, pk_ref, pq_ref, ck_ref, cq_ref, kx_ref, qx_ref):
    f = f_ref[...]
    kx_ref[...] = (jnp.dot(f, pk_ref[...], preferred_element_type=F32) + ck_ref[...]).astype(BF16)
    qx_ref[...] = (jnp.dot(f, pq_ref[...], preferred_element_type=F32) + cq_ref[...]).astype(BF16)


def _fox_extend(pieces, tm=1024):
    t = pieces.shape[0]
    tm = min(tm, t)
    n = FOX_HEADS * LANES
    src = jnp.arange(LANES)[:, None]
    dst = jnp.arange(n)[None, :]
    p_src, h_src = src // FOX_HEADS, src % FOX_HEADS
    h_dst, c_dst = dst // LANES, dst % LANES
    live = (p_src < FOX_PIECES) & (h_src == h_dst)
    pk = jnp.where(live & (c_dst == p_src), -1.0, 0.0).astype(BF16)
    pq = jnp.where(live & (c_dst == p_src + FOX_PIECES), 1.0, 0.0).astype(BF16)
    ck = ((c_dst >= FOX_PIECES) & (c_dst < 2 * FOX_PIECES)).astype(F32)
    cq = (c_dst < FOX_PIECES).astype(F32)
    mat = pl.BlockSpec((LANES, n), lambda i: (0, 0))
    vec = pl.BlockSpec((1, n), lambda i: (0, 0))
    return pl.pallas_call(
        _fox_place_kernel,
        name="fox_extend",
        out_shape=[jax.ShapeDtypeStruct((t, n), BF16), jax.ShapeDtypeStruct((t, n), BF16)],
        grid=(t // tm,),
        in_specs=[pl.BlockSpec((tm, LANES), lambda i: (i, 0)), mat, mat, vec, vec],
        out_specs=[pl.BlockSpec((tm, n), lambda i: (i, 0)), pl.BlockSpec((tm, n), lambda i: (i, 0))],
        compiler_params=_params("parallel"),
    )(pieces, pk, pq, ck, cq)


def _fox_attn_kernel(q_ref, qx_ref, k_ref, kx_ref, vt_ref, z_ref, o_ref, m_ref, l_ref, acc_ref,
                     s0_ref, s1_ref, mx0_ref, mx1_ref, qt_ref, *, nq):
    tq = 2 * ATTN_TK
    dh = FOX_HEAD_DIM

    def begin(qi):
        rows = pl.ds(pl.multiple_of(qi * tq, tq), tq)
        qt_ref[:dh, :] = q_ref[rows, :].T
        qt_ref[dh:, :] = qx_ref[rows, :].T
        _flash_init(m_ref, l_ref, acc_ref)

    def finish(qi):
        rows = pl.ds(pl.multiple_of(qi * tq, tq), tq)
        o = (acc_ref[0] * (1.0 / l_ref[0])).T
        o_ref[rows, :] = (o * _silu(z_ref[rows, :].astype(F32))).astype(o_ref.dtype)

    def scores(ki, buf, cols):
        s_ref, mx_ref = buf
        off = pl.multiple_of(ki * ATTN_TK, ATTN_TK)
        k = jnp.concatenate([k_ref[pl.ds(off, ATTN_TK), :], kx_ref[pl.ds(off, ATTN_TK), :]], axis=1)
        s_t = jnp.dot(k, qt_ref[:, cols], preferred_element_type=F32)
        s_ref[:, cols] = s_t
        mx_ref[:, cols] = jnp.max(s_t, axis=0, keepdims=True)

    def update(ki, buf, masked, cols):
        s_ref, mx_ref = buf
        v_t = vt_ref[:, pl.ds(pl.multiple_of(ki * ATTN_TK, ATTN_TK), ATTN_TK)]
        s_t = s_ref[:, cols]
        s_max = mx_ref[:, cols]
        if masked:
            s_t = _diag_masked(s_t, 1)
            s_max = jnp.max(s_t, axis=0, keepdims=True)
        _flash_step_t(s_t, s_max, v_t, m_ref, l_ref, acc_ref, 0, cols)

    _causal_blocks(nq, begin, scores, update, finish, (s0_ref, mx0_ref), (s1_ref, mx1_ref))


def _fox_attention(qkz, v_t, kx, qx, batch, seq):
    t = qkz.shape[0]
    tq = 2 * ATTN_TK
    nq = seq // tq
    h, dh = FOX_HEADS, FOX_HEAD_DIM
    return pl.pallas_call(
        functools.partial(_fox_attn_kernel, nq=nq),
        name="fox_attn",
        out_shape=jax.ShapeDtypeStruct((t, h * dh), BF16),
        grid=(batch, h),
        in_specs=[pl.BlockSpec((seq, dh), lambda b, hh: (b, hh)),
                  pl.BlockSpec((seq, LANES), lambda b, hh: (b, hh)),
                  pl.BlockSpec((seq, dh), lambda b, hh: (b, h + hh)),
                  pl.BlockSpec((seq, LANES), lambda b, hh: (b, hh)),
                  pl.BlockSpec((dh, seq), lambda b, hh: (hh, b)),
                  pl.BlockSpec((seq, dh), lambda b, hh: (b, 2 * h + hh))],
        out_specs=pl.BlockSpec((seq, dh), lambda b, hh: (b, hh)),
        scratch_shapes=[pltpu.VMEM((1, 1, tq), F32), pltpu.VMEM((1, 1, tq), F32),
                        pltpu.VMEM((1, dh, tq), F32),
                        pltpu.VMEM((ATTN_TK, tq), F32), pltpu.VMEM((ATTN_TK, tq), F32),
                        pltpu.VMEM((1, tq), F32), pltpu.VMEM((1, tq), F32),
                        pltpu.VMEM((dh + LANES, tq), BF16)],
        compiler_params=_params("parallel", "parallel"),
    )(qkz, qx, qkz, kx, v_t, qkz)


BAND_TQ = 256
BAND_WIN = BAND_PAST + BAND_TQ
BAND_HPB = LANES // BAND_HEAD_DIM
BAND_REV = 1024


def _band_bias_kernel(row_ref, o_ref):
    x = jnp.broadcast_to(row_ref[0], (BAND_WIN, BAND_REV))
    x = pltpu.roll(x, BAND_REV - BAND_WIN, 1, stride=1, stride_axis=0)
    x = x[:, :BAND_TQ]
    kc = lax.broadcasted_iota(jnp.int32, x.shape, 0) // CHUNK
    qc = lax.broadcasted_iota(jnp.int32, x.shape, 1) // CHUNK
    band = (kc >= qc) & (kc <= qc + BAND_PAST // CHUNK)
    o_ref[0] = jnp.where(band, x * LOG2E, NEG_INF)


def _band_bias(rel_table):
    nh = rel_table.shape[0]
    t = rel_table.astype(F32)
    row = jnp.concatenate([t, jnp.broadcast_to(t[:, 2 * REL_CLIP:], (nh, BAND_REV - t.shape[1]))], axis=1)
    return pl.pallas_call(
        _band_bias_kernel,
        name="band_bias",
        out_shape=jax.ShapeDtypeStruct((nh, BAND_WIN, BAND_TQ), F32),
        grid=(nh,),
        in_specs=[pl.BlockSpec((1, 1, BAND_REV), lambda h: (h, 0, 0))],
        out_specs=pl.BlockSpec((1, BAND_WIN, BAND_TQ), lambda h: (h, 0, 0)),
        compiler_params=_params("parallel"),
    )(row.reshape(nh, 1, BAND_REV))


def _band_attn_kernel(q_ref, k_ref, vt_ref, z_ref, bias_ref, o_ref, kp_ref, vtp_ref, s0_ref, s1_ref,
                      mx0_ref, mx1_ref, *, nq):
    hd = BAND_HEAD_DIM
    for hh in range(BAND_HPB):
        kp_ref[hh, :BAND_PAST, :] = jnp.zeros((BAND_PAST, hd), BF16)
        kp_ref[hh, BAND_PAST:, :] = k_ref[:, hh * hd:(hh + 1) * hd]
    vtp_ref[:, :BAND_PAST] = jnp.zeros((LANES, BAND_PAST), BF16)
    vtp_ref[:, BAND_PAST:] = vt_ref[...]

    def scores(j, buf):
        s_ref, mx_ref = buf
        start = pl.multiple_of(j * BAND_TQ, BAND_TQ)
        q = q_ref[pl.ds(start, BAND_TQ), :]
        for hh in range(BAND_HPB):
            s_t = lax.dot_general(kp_ref[hh, pl.ds(start, BAND_WIN), :], q[:, hh * hd:(hh + 1) * hd],
                                  NT_DIMS, preferred_element_type=F32) + bias_ref[hh]
            s_ref[hh] = s_t
            mx_ref[hh] = jnp.max(s_t, axis=0, keepdims=True)

    def update(j, buf, masked):
        s_ref, mx_ref = buf
        start = pl.multiple_of(j * BAND_TQ, BAND_TQ)
        outs = []
        for hh in range(BAND_HPB):
            s_t = s_ref[hh]
            s_max = mx_ref[hh]
            if masked:
                kpos = start - BAND_PAST + lax.broadcasted_iota(jnp.int32, (BAND_WIN, 1), 0)
                s_t = jnp.where(kpos >= 0, s_t, NEG_INF)
                s_max = jnp.max(s_t, axis=0, keepdims=True)
            p_t = jnp.exp2(s_t - s_max)
            l = jnp.sum(p_t, axis=0, keepdims=True)
            v_t = vtp_ref[hh * hd:(hh + 1) * hd, pl.ds(start, BAND_WIN)]
            outs.append(jnp.dot(v_t, p_t.astype(BF16), preferred_element_type=F32) * (1.0 / l))
        o = jnp.concatenate(outs, axis=0).T
        z = z_ref[pl.ds(start, BAND_TQ), :].astype(F32)
        o_ref[pl.ds(start, BAND_TQ), :] = (o * _silu(z)).astype(o_ref.dtype)

    buf0, buf1 = (s0_ref, mx0_ref), (s1_ref, mx1_ref)
    scores(0, buf0)
    scores(1, buf1)
    update(0, buf0, True)
    scores(2, buf0)
    update(1, buf1, True)

    def pair(jj, carry):
        j = 2 + 2 * jj
        scores(j + 1, buf1)
        update(j, buf0, False)
        scores(j + 2, buf0)
        update(j + 1, buf1, False)
        return carry

    lax.fori_loop(0, (nq - 4) // 2, pair, 0)
    scores(nq - 1, buf1)
    update(nq - 2, buf0, False)
    update(nq - 1, buf1, False)


def _band_attention(qkz, v_t, bias_t, batch, seq):
    t = qkz.shape[0]
    nq = seq // BAND_TQ
    assert nq >= 4 and nq % 2 == 0
    hp = BAND_HEADS // BAND_HPB
    return pl.pallas_call(
        functools.partial(_band_attn_kernel, nq=nq),
        name="band_attn",
        out_shape=jax.ShapeDtypeStruct((t, BAND_HEADS * BAND_HEAD_DIM), BF16),
        grid=(batch, hp),
        in_specs=[pl.BlockSpec((seq, LANES), lambda b, g: (b, g)),
                  pl.BlockSpec((seq, LANES), lambda b, g: (b, hp + g)),
                  pl.BlockSpec((LANES, seq), lambda b, g: (g, b)),
                  pl.BlockSpec((seq, LANES), lambda b, g: (b, 2 * hp + g)),
                  pl.BlockSpec((BAND_HPB, BAND_WIN, BAND_TQ), lambda b, g: (g, 0, 0))],
        out_specs=pl.BlockSpec((seq, LANES), lambda b, g: (b, g)),
        scratch_shapes=[pltpu.VMEM((BAND_HPB, seq + BAND_PAST, BAND_HEAD_DIM), BF16),
                        pltpu.VMEM((LANES, seq + BAND_PAST), BF16),
                        pltpu.VMEM((BAND_HPB, BAND_WIN, BAND_TQ), F32),
                        pltpu.VMEM((BAND_HPB, BAND_WIN, BAND_TQ), F32),
                        pltpu.VMEM((BAND_HPB, 1, BAND_TQ), F32),
                        pltpu.VMEM((BAND_HPB, 1, BAND_TQ), F32)],
        compiler_params=_params("parallel", "parallel"),
    )(qkz, qkz, v_t, qkz, bias_t)


def _pool_kernel(u_ref, up_ref, z_ref, w_ref, ps_ref, o_ref, *, tm, nb):
    i = pl.program_id(0)
    t0 = (i % nb) * tm
    has_history = t0 > 0
    t = t0 + lax.broadcasted_iota(jnp.int32, (tm, 1), 0)
    for g, win in enumerate(POOL_WINDOWS):
        sl = slice(g * POOL_GROUP, (g + 1) * POOL_GROUP)
        u = u_ref[:, sl]
        acc = jnp.concatenate([jnp.where(has_history, up_ref[:, sl], 0.0), u], axis=0)
        d = 1
        while d < win:
            acc = acc + pltpu.roll(acc, d, 0)
            d *= 2
        wsum = acc[POOL_HALO:, :]
        cnt = jnp.minimum(t + 1, win).astype(F32)
        delta = (wsum / cnt - u).astype(BF16)
        y = jnp.dot(delta, w_ref[g], preferred_element_type=F32) * ps_ref[:, sl]
        o_ref[:, sl] = (y * _silu(z_ref[:, sl])).astype(o_ref.dtype)


def _pool_mix(uz, seq, pool_w, pool_scale, tm=256):
    t = uz.shape[0]
    d = D_MODEL
    tm = min(tm, seq)
    nb = seq // tm
    hb = tm // POOL_HALO
    return pl.pallas_call(
        functools.partial(_pool_kernel, tm=tm, nb=nb),
        name="pool_mix",
        out_shape=jax.ShapeDtypeStruct((t, d), BF16),
        grid=(t // tm,),
        in_specs=[pl.BlockSpec((tm, d), lambda i: (i, 0)),
                  pl.BlockSpec((POOL_HALO, d), lambda i: (jnp.maximum(i * hb - 1, 0), 0)),
                  pl.BlockSpec((tm, d), lambda i: (i, 1)),
                  pl.BlockSpec((len(POOL_WINDOWS), POOL_GROUP, POOL_GROUP), lambda i: (0, 0, 0)),
                  pl.BlockSpec((1, d), lambda i: (0, 0))],
        out_specs=pl.BlockSpec((tm, d), lambda i: (i, 0)),
        compiler_params=_params("parallel"),
    )(uz, uz, uz, pool_w, pool_scale.reshape(1, d).astype(F32))


PROJ_TN = 1024


def _qkz_scale(qk_dim, branch):
    return jnp.concatenate([jnp.full((branch,), qk_dim ** -0.5 * LOG2E, F32), jnp.ones((2 * branch,), F32)])


def kernel(x, c, ada_w, ada_b, norm_pre, norm_post, diff_w_in, diff_w_out, diff_lambda_q1, diff_lambda_k1, diff_lambda_q2, diff_lambda_k2, diff_subln, band_w_in, band_w_out, band_rel_bias, fox_w_in, fox_w_out, fox_forget_bias, pool_w_in, pool_w_out, pool_group_w, pool_scale):
    batch, seq, d = x.shape
    depth = ada_w.shape[0]
    branch = d
    nblk = branch // PROJ_TN
    qkz_blocks = list(range(2 * nblk)) + list(range(3 * nblk, 4 * nblk))
    mod = _ada_mod(c, ada_w, ada_b)
    x2 = x.reshape(batch * seq, d)
    for i in range(depth):
        mod3 = mod[i].reshape(mod.shape[1], 1, 3 * d)
        kind = i % 4
        if kind == 0:
            lambda_init = 0.8 - 0.6 * math.exp(-0.3 * i)
            qkz, h = _norm_proj(x2, seq, norm_pre[i], mod3, diff_w_in.astype(BF16), qkz_blocks,
                                _qkz_scale(DIFF_QK_DIM, branch), emit_h=True)
            v_t = _proj_t(diff_w_in[:, 2 * branch:3 * branch].T.astype(BF16), h)
            o = _diff_attention(qkz, v_t, batch, seq, diff_lambda_q1, diff_lambda_k1, diff_lambda_q2,
                                diff_lambda_k2, diff_subln, lambda_init)
            w_out = diff_w_out
        elif kind == 1:
            qkz, h = _norm_proj(x2, seq, norm_pre[i], mod3, band_w_in.astype(BF16), qkz_blocks,
                                _qkz_scale(BAND_HEAD_DIM, branch), emit_h=True)
            v_t = _proj_t(band_w_in[:, 2 * branch:3 * branch].T.astype(BF16), h)
            o = _band_attention(qkz, v_t, _band_bias(band_rel_bias), batch, seq)
            w_out = band_w_out
        elif kind == 2:
            wf = jnp.pad(fox_w_in[:, 4 * branch:], ((0, 0), (0, LANES - FOX_HEADS))).astype(BF16)
            qkz, h, f_logit = _norm_proj(x2, seq, norm_pre[i], mod3, fox_w_in.astype(BF16), qkz_blocks,
                                         _qkz_scale(FOX_HEAD_DIM, branch), wf=wf, emit_h=True)
            v_t = _proj_t(fox_w_in[:, 2 * branch:3 * branch].T.astype(BF16), h)
            kx, qx = _fox_extend(_fox_gates(f_logit, fox_forget_bias, batch, seq))
            o = _fox_attention(qkz, v_t, kx, qx, batch, seq)
            w_out = fox_w_out
        else:
            uz, = _norm_proj(x2, seq, norm_pre[i], mod3, pool_w_in.astype(BF16), list(range(2 * nblk)),
                             jnp.ones((2 * branch,), F32), out_dtype=F32)
            o = _pool_mix(uz, seq, pool_group_w.astype(BF16), pool_scale)
            w_out = pool_w_out
        x2 = _out_proj_residual(o, w_out.astype(BF16), x2, seq, mod3, norm_post[i])
    return x2.reshape(batch, seq, d)
```

```python
import functools
import math

import jax
import jax.numpy as jnp
from jax import lax
from jax.experimental import pallas as pl
from jax.experimental.pallas import tpu as pltpu

F32 = jnp.float32
BF16 = jnp.bfloat16

D_MODEL = 2048
EPS = 1e-6
NEG_INF = -1e30
LOG2E = math.log2(math.e)
CHUNK = 64
DIFF_HEADS, DIFF_V_DIM, DIFF_QK_DIM = 8, 256, 128
BAND_HEADS, BAND_HEAD_DIM, BAND_PAST, REL_CLIP = 32, 64, 512, 256
FOX_HEADS, FOX_HEAD_DIM = 16, 128
POOL_WINDOWS = (2, 4, 8, 16)
POOL_GROUP = D_MODEL // len(POOL_WINDOWS)
POOL_HALO = 16

LANES = 128
VMEM_LIMIT = 52 * 1024 * 1024

NT_DIMS = (((1,), (1,)), ((), ()))


def _silu(x):
    return x / (1.0 + jnp.exp(-x))


def _params(*sem):
    return pltpu.CompilerParams(dimension_semantics=sem, vmem_limit_bytes=VMEM_LIMIT)


def _ada_kernel(c_ref, w_ref, b_ref, o_ref):
    ca = _silu(c_ref[...]).astype(BF16)
    o_ref[0] = jnp.dot(ca, w_ref[0].astype(BF16), preferred_element_type=F32) + b_ref[0]


def _ada_mod(c, ada_w, ada_b):
    b, d = c.shape
    bp = -(-b // 8) * 8
    depth, _, n3 = ada_w.shape
    tn = 768
    cp = jnp.pad(c, ((0, bp - b), (0, 0)))
    return pl.pallas_call(
        _ada_kernel,
        name="ada_mod",
        out_shape=jax.ShapeDtypeStruct((depth, bp, n3), F32),
        grid=(depth, n3 // tn),
        in_specs=[pl.BlockSpec((bp, d), lambda l, j: (0, 0)),
                  pl.BlockSpec((1, d, tn), lambda l, j: (l, 0, j)),
                  pl.BlockSpec((1, 1, tn), lambda l, j: (l, 0, j))],
        out_specs=pl.BlockSpec((1, bp, tn), lambda l, j: (l, 0, j)),
        compiler_params=_params("parallel", "parallel"),
    )(cp, ada_w, ada_b.reshape(depth, 1, n3))


def _norm_proj_kernel(x_ref, g_ref, sh_ref, sc_ref, w_ref, cs_ref, *rest, has_f, emit_h, sub):
    rest = list(rest)
    wf_ref = rest.pop(0) if has_f else None
    o_ref = rest.pop(0)
    hout_ref = rest.pop(0) if emit_h else None
    f_ref = rest.pop(0) if has_f else None
    h_ref = rest.pop(0)

    @pl.when(pl.program_id(1) == 0)
    def _():
        g = g_ref[...]
        sc = 1.0 + sc_ref[0]
        sh = sh_ref[0]
        for r in range(x_ref.shape[0] // sub):
            rows = slice(r * sub, (r + 1) * sub)
            x = x_ref[rows, :]
            y = x * lax.rsqrt(jnp.mean(x * x, axis=-1, keepdims=True) + EPS) * g
            h = (y * sc + sh).astype(BF16)
            h_ref[rows, :] = h
            if emit_h:
                hout_ref[rows, :] = h
            if has_f:
                f_ref[rows, :] = jnp.dot(h, wf_ref[...], preferred_element_type=F32)
            y = jnp.dot(h, w_ref[...], preferred_element_type=F32)
            o_ref[rows, :] = (y * cs_ref[...]).astype(o_ref.dtype)

    @pl.when(pl.program_id(1) > 0)
    def _():
        y = jnp.dot(h_ref[...], w_ref[...], preferred_element_type=F32)
        o_ref[...] = (y * cs_ref[...]).astype(o_ref.dtype)


def _norm_proj(x2, seq, g, mod3, w, col_blocks, col_scale, wf=None, emit_h=False, out_dtype=BF16,
               tm=1024, tn=1024):
    t, d = x2.shape
    tm = min(tm, seq)
    nb = seq // tm
    nj = len(col_blocks)
    n = nj * tn
    first, skip_from, skip = col_blocks[0], None, 0
    for a, b2 in zip(col_blocks, col_blocks[1:]):
        if b2 != a + 1:
            skip_from, skip = a - first + 1, b2 - a - 1
    if skip_from is None:
        wmap = lambda i, j: (0, j + first)
    else:
        wmap = lambda i, j: (0, j + first + jnp.where(j >= skip_from, skip, 0))
    in_specs = [pl.BlockSpec((tm, d), lambda i, j: (i, 0)),
                pl.BlockSpec((1, d), lambda i, j: (0, 0)),
                pl.BlockSpec((1, 1, d), lambda i, j: (i // nb, 0, 0)),
                pl.BlockSpec((1, 1, d), lambda i, j: (i // nb, 0, 1)),
                pl.BlockSpec((d, tn), wmap),
                pl.BlockSpec((1, tn), lambda i, j: (0, j))]
    out_shape = [jax.ShapeDtypeStruct((t, n), out_dtype)]
    out_specs = [pl.BlockSpec((tm, tn), lambda i, j: (i, j))]
    args = [x2, g.reshape(1, d), mod3, mod3, w, col_scale.reshape(1, n).astype(F32)]
    if wf is not None:
        in_specs.append(pl.BlockSpec((d, LANES), lambda i, j: (0, 0)))
        args.append(wf)
    if emit_h:
        out_shape.append(jax.ShapeDtypeStruct((t, d), BF16))
        out_specs.append(pl.BlockSpec((tm, d), lambda i, j: (i, 0)))
    if wf is not None:
        out_shape.append(jax.ShapeDtypeStruct((t, LANES), F32))
        out_specs.append(pl.BlockSpec((tm, LANES), lambda i, j: (i, 0)))
    return pl.pallas_call(
        functools.partial(_norm_proj_kernel, has_f=wf is not None, emit_h=emit_h, sub=min(256, tm)),
        name="norm_proj",
        out_shape=out_shape,
        grid=(t // tm, nj),
        in_specs=in_specs,
        out_specs=out_specs,
        scratch_shapes=[pltpu.VMEM((tm, d), BF16)],
        compiler_params=_params("parallel", "arbitrary"),
    )(*args)


def _proj_t_kernel(w_ref, h_ref, o_ref):
    o_ref[...] = lax.dot_general(w_ref[...], h_ref[...], NT_DIMS,
                                 preferred_element_type=F32).astype(o_ref.dtype)


def _proj_t(w_t, h, tn=1024, tm=2048):
    n, d = w_t.shape
    t = h.shape[0]
    tm = min(tm, t)
    return pl.pallas_call(
        _proj_t_kernel,
        name="proj_t",
        out_shape=jax.ShapeDtypeStruct((n, t), BF16),
        grid=(t // tm, n // tn),
        in_specs=[pl.BlockSpec((tn, d), lambda i, j: (j, 0)),
                  pl.BlockSpec((tm, d), lambda i, j: (i, 0))],
        out_specs=pl.BlockSpec((tn, tm), lambda i, j: (j, i)),
        compiler_params=_params("parallel", "arbitrary"),
    )(w_t, h)


def _out_proj_kernel(o_ref, w_ref, x_ref, gate_ref, g_ref, out_ref, *, sub):
    for r in range(o_ref.shape[0] // sub):
        rows = slice(r * sub, (r + 1) * sub)
        y = jnp.dot(o_ref[rows, :], w_ref[...], preferred_element_type=F32)
        yn = y * lax.rsqrt(jnp.mean(y * y, axis=-1, keepdims=True) + EPS) * g_ref[...]
        out_ref[rows, :] = x_ref[rows, :] + gate_ref[0] * yn


def _out_proj_residual(o2, w, x2, seq, mod3, g, tm=512):
    t, d = x2.shape
    tm = min(tm, seq)
    nb = seq // tm
    return pl.pallas_call(
        functools.partial(_out_proj_kernel, sub=min(256, tm)),
        name="out_proj",
        out_shape=jax.ShapeDtypeStruct((t, d), F32),
        grid=(t // tm,),
        in_specs=[pl.BlockSpec((tm, d), lambda i: (i, 0)),
                  pl.BlockSpec((d, d), lambda i: (0, 0), pipeline_mode=pl.Buffered(1)),
                  pl.BlockSpec((tm, d), lambda i: (i, 0)),
                  pl.BlockSpec((1, 1, d), lambda i: (i // nb, 0, 2)),
                  pl.BlockSpec((1, d), lambda i: (0, 0))],
        out_specs=pl.BlockSpec((tm, d), lambda i: (i, 0)),
        compiler_params=_params("parallel"),
    )(o2, w, x2, mod3, g.reshape(1, d))


def _flash_step_t(s_t, s_max, v_t, m_ref, l_ref, acc_ref, idx, cols):
    m_prev = m_ref[idx, :, cols]
    m_new = jnp.maximum(m_prev, s_max)
    alpha = jnp.exp2(m_prev - m_new)
    p_t = jnp.exp2(s_t - m_new)
    if l_ref is not None:
        l_ref[idx, :, cols] = alpha * l_ref[idx, :, cols] + jnp.sum(p_t, axis=0, keepdims=True)
    acc_ref[idx, :, cols] = (alpha * acc_ref[idx, :, cols]
                             + jnp.dot(v_t, p_t.astype(BF16), preferred_element_type=F32))
    m_ref[idx, :, cols] = m_new


def _flash_init(m_ref, l_ref, acc_ref):
    m_ref[...] = jnp.full(m_ref.shape, NEG_INF, F32)
    if l_ref is not None:
        l_ref[...] = jnp.zeros(l_ref.shape, F32)
    acc_ref[...] = jnp.zeros(acc_ref.shape, F32)


ONES_ROWS = 16


ATTN_TK = 512


def _mask_diagonal(s_ref, mx_ref, c0, granule):
    t = LANES
    kg = lax.broadcasted_iota(jnp.int32, (t, t), 0) // granule
    qg = lax.broadcasted_iota(jnp.int32, (t, t), 1) // granule
    visible = kg <= qg
    for r in range(ATTN_TK // t):
        rows = slice(r * t, (r + 1) * t)
        if r:
            s_ref[rows, c0:c0 + r * t] = jnp.full((t, r * t), NEG_INF, F32)
        diag = slice(c0 + r * t, c0 + (r + 1) * t)
        s_ref[rows, diag] = jnp.where(visible, s_ref[rows, diag], NEG_INF)
    cols = slice(c0, c0 + ATTN_TK)
    mx_ref[:, cols] = jnp.max(s_ref[:, cols], axis=0, keepdims=True)


def _causal_blocks(nq, begin, scores, mask, update, finish, buf0, buf1):
    every, late = slice(None), slice(ATTN_TK, 2 * ATTN_TK)
    begin(0)
    scores(0, buf0, every)

    def pairs(j, n):
        for _ in range(n):
            scores(j + 1, buf1, every)
            update(j, buf0, every)
            scores(j + 2, buf0, every)
            update(j + 1, buf1, every)
            j = j + 2

    def qblock(qi, carry):
        def quad(jj, c):
            pairs(4 * jj, 2)
            return c

        lax.fori_loop(0, qi // 2, quad, 0)

        @pl.when(qi % 2 == 1)
        def _():
            pairs(2 * qi - 2, 1)

        scores(2 * qi + 1, buf1, late)
        mask(buf0, 0)
        update(2 * qi, buf0, every)
        mask(buf1, ATTN_TK)
        update(2 * qi + 1, buf1, late)
        finish(qi)
        begin(jnp.minimum(qi + 1, nq - 1))
        scores(0, buf0, every)
        return carry

    lax.fori_loop(0, nq, qblock, 0)


def _diff_attn_kernel(q_ref, k_ref, vt_ref, z_ref, lq1_ref, lk1_ref, lq2_ref, lk2_ref, g_ref,
                      o_ref, m_ref, l_ref, acc_ref, s0_ref, s1_ref, mx0_ref, mx1_ref, qc_ref,
                      *, nq, lambda_init):
    tq = 2 * ATTN_TK
    lam = (jnp.exp(jnp.sum(lq1_ref[...] * lk1_ref[...], axis=-1, keepdims=True))
           - jnp.exp(jnp.sum(lq2_ref[...] * lk2_ref[...], axis=-1, keepdims=True)) + lambda_init)
    g = g_ref[...] * (1.0 - lambda_init)

    def begin(qi):
        qc_ref[...] = q_ref[pl.ds(pl.multiple_of(qi * tq, tq), tq), :]
        _flash_init(m_ref, l_ref, acc_ref)

    def finish(qi):
        rows = pl.ds(pl.multiple_of(qi * tq, tq), tq)
        o_t = acc_ref[0] * (1.0 / l_ref[0]) - acc_ref[1] * (lam / l_ref[1])
        o = o_t.T
        on = o * lax.rsqrt(jnp.mean(o * o, axis=-1, keepdims=True) + EPS) * g
        o_ref[rows, :] = (on * _silu(z_ref[rows, :].astype(F32))).astype(o_ref.dtype)

    def scores(ki, buf, cols):
        s_ref, mx_ref = buf
        k = k_ref[pl.ds(pl.multiple_of(ki * ATTN_TK, ATTN_TK), ATTN_TK), :]
        for m in range(2):
            sl = slice(m * DIFF_QK_DIM, (m + 1) * DIFF_QK_DIM)
            s_t = lax.dot_general(k[:, sl], qc_ref[cols, sl], NT_DIMS, preferred_element_type=F32)
            s_ref[m, :, cols] = s_t
            mx_ref[m, :, cols] = jnp.max(s_t, axis=0, keepdims=True)

    def mask(buf, c0):
        s_ref, mx_ref = buf
        for m in range(2):
            _mask_diagonal(s_ref.at[m], mx_ref.at[m], c0, CHUNK)

    def update(ki, buf, cols):
        s_ref, mx_ref = buf
        v_t = vt_ref[:, pl.ds(pl.multiple_of(ki * ATTN_TK, ATTN_TK), ATTN_TK)]
        for m in range(2):
            _flash_step_t(s_ref[m, :, cols], mx_ref[m, :, cols], v_t, m_ref, l_ref, acc_ref, m, cols)

    _causal_blocks(nq, begin, scores, mask, update, finish, (s0_ref, mx0_ref), (s1_ref, mx1_ref))


def _diff_attention(qkz, v_t, batch, seq, lq1, lk1, lq2, lk2, subln_g, lambda_init):
    t = qkz.shape[0]
    tq = 2 * ATTN_TK
    nq = seq // tq
    h, dv = DIFF_HEADS, DIFF_V_DIM
    vec = lambda a: a.reshape(1, -1).astype(F32)
    small = lambda n: pl.BlockSpec((1, n), lambda b, hh: (0, 0))
    once = pl.Buffered(1)
    return pl.pallas_call(
        functools.partial(_diff_attn_kernel, nq=nq, lambda_init=lambda_init),
        name="diff_attn",
        out_shape=jax.ShapeDtypeStruct((t, h * dv), BF16),
        grid=(batch, h),
        in_specs=[pl.BlockSpec((seq, dv), lambda b, hh: (b, hh)),
                  pl.BlockSpec((seq, dv), lambda b, hh: (b, h + hh), pipeline_mode=once),
                  pl.BlockSpec((dv, seq), lambda b, hh: (hh, b), pipeline_mode=once),
                  pl.BlockSpec((seq, dv), lambda b, hh: (b, 2 * h + hh)),
                  small(DIFF_QK_DIM), small(DIFF_QK_DIM), small(DIFF_QK_DIM), small(DIFF_QK_DIM),
                  small(dv)],
        out_specs=pl.BlockSpec((seq, dv), lambda b, hh: (b, hh)),
        scratch_shapes=[pltpu.VMEM((2, 1, tq), F32), pltpu.VMEM((2, 1, tq), F32),
                        pltpu.VMEM((2, dv, tq), F32),
                        pltpu.VMEM((2, ATTN_TK, tq), F32), pltpu.VMEM((2, ATTN_TK, tq), F32),
                        pltpu.VMEM((2, 1, tq), F32), pltpu.VMEM((2, 1, tq), F32),
                        pltpu.VMEM((tq, dv), BF16)],
        compiler_params=_params("parallel", "parallel"),
    )(qkz, qkz, v_t, qkz, vec(lq1), vec(lk1), vec(lq2), vec(lk2), vec(subln_g))


FOX_PIECES = 3


def _fox_gate_kernel(f_ref, b_ref, o_ref):
    x = f_ref[...] + b_ref[...]
    lf = jnp.minimum(x, 0.0) - jnp.log1p(jnp.exp(-jnp.abs(x)))
    n = lf.shape[0]
    row = lax.broadcasted_iota(jnp.int32, lf.shape, 0)
    d = 1
    while d < n:
        lf = lf + jnp.where(row >= d, pltpu.roll(lf, d, 0), 0.0)
        d *= 2
    rest = lf * LOG2E
    lane = lax.broadcasted_iota(jnp.int32, lf.shape, 1)
    out = jnp.zeros(lf.shape, F32)
    for p in range(FOX_PIECES):
        piece = rest.astype(BF16).astype(F32)
        rest = rest - piece
        moved = piece if p == 0 else pltpu.roll(piece, p * FOX_HEADS, 1)
        out = jnp.where((lane >= p * FOX_HEADS) & (lane < (p + 1) * FOX_HEADS), moved, out)
    o_ref[...] = out.astype(BF16)


def _fox_gates(f_logit, bias, batch, seq):
    bpad = jnp.zeros((1, LANES), F32).at[0, :FOX_HEADS].set(bias.astype(F32))
    return pl.pallas_call(
        _fox_gate_kernel,
        name="fox_gates",
        out_shape=jax.ShapeDtypeStruct((batch * seq, LANES), BF16),
        grid=(batch,),
        in_specs=[pl.BlockSpec((seq, LANES), lambda b: (b, 0)),
                  pl.BlockSpec((1, LANES), lambda b: (0, 0))],
        out_specs=pl.BlockSpec((seq, LANES), lambda b: (b, 0)),
        compiler_params=_params("parallel"),
    )(f_logit, bpad)


def _fox_place_kernel(f_ref, pk_ref, pq_ref, ck_ref, cq_ref, kx_ref, qx_ref):
    f = f_ref[...]
    kx_ref[...] = (jnp.dot(f, pk_ref[...], preferred_element_type=F32) + ck_ref[...]).astype(BF16)
    qx_ref[...] = (jnp.dot(f, pq_ref[...], preferred_element_type=F32) + cq_ref[...]).astype(BF16)


def _fox_extend(pieces, tm=1024):
    t = pieces.shape[0]
    tm = min(tm, t)
    n = FOX_HEADS * LANES
    src = jnp.arange(LANES)[:, None]
    dst = jnp.arange(n)[None, :]
    p_src, h_src = src // FOX_HEADS, src % FOX_HEADS
    h_dst, c_dst = dst // LANES, dst % LANES
    live = (p_src < FOX_PIECES) & (h_src == h_dst)
    pk = jnp.where(live & (c_dst == p_src), -1.0, 0.0).astype(BF16)
    pq = jnp.where(live & (c_dst == p_src + FOX_PIECES), 1.0, 0.0).astype(BF16)
    ck = ((c_dst >= FOX_PIECES) & (c_dst < 2 * FOX_PIECES)).astype(F32)
    cq = (c_dst < FOX_PIECES).astype(F32)
    mat = pl.BlockSpec((LANES, n), lambda i: (0, 0))
    vec = pl.BlockSpec((1, n), lambda i: (0, 0))
    return pl.pallas_call(
        _fox_place_kernel,
        name="fox_extend",
        out_shape=[jax.ShapeDtypeStruct((t, n), BF16), jax.ShapeDtypeStruct((t, n), BF16)],
        grid=(t // tm,),
        in_specs=[pl.BlockSpec((tm, LANES), lambda i: (i, 0)), mat, mat, vec, vec],
        out_specs=[pl.BlockSpec((tm, n), lambda i: (i, 0)), pl.BlockSpec((tm, n), lambda i: (i, 0))],
        compiler_params=_params("parallel"),
    )(pieces, pk, pq, ck, cq)


def _fox_attn_kernel(q_ref, qx_ref, k_ref, kx_ref, vt_ref, z_ref, o_ref, m_ref, acc_ref,
                     s0_ref, s1_ref, mx0_ref, mx1_ref, qt_ref, vte_ref, *, nq):
    tq = 2 * ATTN_TK
    dh = FOX_HEAD_DIM
    vte_ref[:dh, :] = vt_ref[...]
    vte_ref[dh:, :] = jnp.ones((ONES_ROWS, vte_ref.shape[1]), BF16)

    def begin(qi):
        rows = pl.ds(pl.multiple_of(qi * tq, tq), tq)
        qt_ref[:dh, :] = q_ref[rows, :].T
        qt_ref[dh:, :] = qx_ref[rows, :].T
        _flash_init(m_ref, None, acc_ref)

    def finish(qi):
        rows = pl.ds(pl.multiple_of(qi * tq, tq), tq)
        o = (acc_ref[0, :dh, :] * (1.0 / acc_ref[0, dh:dh + 1, :])).T
        o_ref[rows, :] = (o * _silu(z_ref[rows, :].astype(F32))).astype(o_ref.dtype)

    def scores(ki, buf, cols):
        s_ref, mx_ref = buf
        off = pl.multiple_of(ki * ATTN_TK, ATTN_TK)
        k = jnp.concatenate([k_ref[pl.ds(off, ATTN_TK), :], kx_ref[pl.ds(off, ATTN_TK), :]], axis=1)
        s_t = jnp.dot(k, qt_ref[:, cols], preferred_element_type=F32)
        s_ref[:, cols] = s_t
        mx_ref[:, cols] = jnp.max(s_t, axis=0, keepdims=True)

    def mask(buf, c0):
        _mask_diagonal(buf[0], buf[1], c0, 1)

    def update(ki, buf, cols):
        s_ref, mx_ref = buf
        v_t = vte_ref[:, pl.ds(pl.multiple_of(ki * ATTN_TK, ATTN_TK), ATTN_TK)]
        _flash_step_t(s_ref[:, cols], mx_ref[:, cols], v_t, m_ref, None, acc_ref, 0, cols)

    _causal_blocks(nq, begin, scores, mask, update, finish, (s0_ref, mx0_ref), (s1_ref, mx1_ref))


def _fox_attention(qkz, v_t, kx, qx, batch, seq):
    t = qkz.shape[0]
    tq = 2 * ATTN_TK
    nq = seq // tq
    h, dh = FOX_HEADS, FOX_HEAD_DIM
    return pl.pallas_call(
        functools.partial(_fox_attn_kernel, nq=nq),
        name="fox_attn",
        out_shape=jax.ShapeDtypeStruct((t, h * dh), BF16),
        grid=(batch, h),
        in_specs=[pl.BlockSpec((seq, dh), lambda b, hh: (b, hh)),
                  pl.BlockSpec((seq, LANES), lambda b, hh: (b, hh)),
                  pl.BlockSpec((seq, dh), lambda b, hh: (b, h + hh)),
                  pl.BlockSpec((seq, LANES), lambda b, hh: (b, hh)),
                  pl.BlockSpec((dh, seq), lambda b, hh: (hh, b)),
                  pl.BlockSpec((seq, dh), lambda b, hh: (b, 2 * h + hh))],
        out_specs=pl.BlockSpec((seq, dh), lambda b, hh: (b, hh)),
        scratch_shapes=[pltpu.VMEM((1, 1, tq), F32),
                        pltpu.VMEM((1, dh + ONES_ROWS, tq), F32),
                        pltpu.VMEM((ATTN_TK, tq), F32), pltpu.VMEM((ATTN_TK, tq), F32),
                        pltpu.VMEM((1, tq), F32), pltpu.VMEM((1, tq), F32),
                        pltpu.VMEM((dh + LANES, tq), BF16),
                        pltpu.VMEM((dh + ONES_ROWS, seq), BF16)],
        compiler_params=_params("parallel", "parallel"),
    )(qkz, qx, qkz, kx, v_t, qkz)


BAND_TQ = 256
BAND_WIN = BAND_PAST + BAND_TQ
BAND_HPB = LANES // BAND_HEAD_DIM
BAND_REV = 1024


def _band_bias_kernel(row_ref, o_ref):
    x = jnp.broadcast_to(row_ref[0], (BAND_WIN, BAND_REV))
    x = pltpu.roll(x, BAND_REV - BAND_WIN, 1, stride=1, stride_axis=0)
    x = x[:, :BAND_TQ]
    kc = lax.broadcasted_iota(jnp.int32, x.shape, 0) // CHUNK
    qc = lax.broadcasted_iota(jnp.int32, x.shape, 1) // CHUNK
    band = (kc >= qc) & (kc <= qc + BAND_PAST // CHUNK)
    o_ref[0] = jnp.where(band, x * LOG2E, NEG_INF)


def _band_bias(rel_table):
    nh = rel_table.shape[0]
    t = rel_table.astype(F32)
    row = jnp.concatenate([t, jnp.broadcast_to(t[:, 2 * REL_CLIP:], (nh, BAND_REV - t.shape[1]))], axis=1)
    return pl.pallas_call(
        _band_bias_kernel,
        name="band_bias",
        out_shape=jax.ShapeDtypeStruct((nh, BAND_WIN, BAND_TQ), F32),
        grid=(nh,),
        in_specs=[pl.BlockSpec((1, 1, BAND_REV), lambda h: (h, 0, 0))],
        out_specs=pl.BlockSpec((1, BAND_WIN, BAND_TQ), lambda h: (h, 0, 0)),
        compiler_params=_params("parallel"),
    )(row.reshape(nh, 1, BAND_REV))


def _band_attn_kernel(q_ref, k_ref, vt_ref, z_ref, bias_ref, o_ref, kp_ref, vtp_ref, s0_ref, s1_ref,
                      mx0_ref, mx1_ref, *, nq):
    hd = BAND_HEAD_DIM
    for hh in range(BAND_HPB):
        kp_ref[hh, :BAND_PAST, :] = jnp.zeros((BAND_PAST, hd), BF16)
        kp_ref[hh, BAND_PAST:, :] = k_ref[:, hh * hd:(hh + 1) * hd]
        vtp_ref[hh, :hd, :BAND_PAST] = jnp.zeros((hd, BAND_PAST), BF16)
        vtp_ref[hh, :hd, BAND_PAST:] = vt_ref[hh * hd:(hh + 1) * hd, :]
        vtp_ref[hh, hd:, :] = jnp.ones((ONES_ROWS, vtp_ref.shape[2]), BF16)

    def scores(j, buf):
        s_ref, mx_ref = buf
        start = pl.multiple_of(j * BAND_TQ, BAND_TQ)
        q = q_ref[pl.ds(start, BAND_TQ), :]
        for hh in range(BAND_HPB):
            s_t = lax.dot_general(kp_ref[hh, pl.ds(start, BAND_WIN), :], q[:, hh * hd:(hh + 1) * hd],
                                  NT_DIMS, preferred_element_type=F32) + bias_ref[hh]
            s_ref[hh] = s_t
            mx_ref[hh] = jnp.max(s_t, axis=0, keepdims=True)

    def update(j, buf, masked):
        s_ref, mx_ref = buf
        start = pl.multiple_of(j * BAND_TQ, BAND_TQ)
        outs = []
        for hh in range(BAND_HPB):
            s_t = s_ref[hh]
            s_max = mx_ref[hh]
            if masked:
                kpos = start - BAND_PAST + lax.broadcasted_iota(jnp.int32, (BAND_WIN, 1), 0)
                s_t = jnp.where(kpos >= 0, s_t, NEG_INF)
                s_max = jnp.max(s_t, axis=0, keepdims=True)
            p_t = jnp.exp2(s_t - s_max).astype(BF16)
            ol = jnp.dot(vtp_ref[hh, :, pl.ds(start, BAND_WIN)], p_t, preferred_element_type=F32)
            outs.append(ol[:hd] * (1.0 / ol[hd:hd + 1]))
        o = jnp.concatenate(outs, axis=0).T
        z = z_ref[pl.ds(start, BAND_TQ), :].astype(F32)
        o_ref[pl.ds(start, BAND_TQ), :] = (o * _silu(z)).astype(o_ref.dtype)

    buf0, buf1 = (s0_ref, mx0_ref), (s1_ref, mx1_ref)
    scores(0, buf0)
    scores(1, buf1)
    update(0, buf0, True)
    scores(2, buf0)
    update(1, buf1, True)

    def pair(jj, carry):
        j = 2 + 2 * jj
        scores(j + 1, buf1)
        update(j, buf0, False)
        scores(j + 2, buf0)
        update(j + 1, buf1, False)
        return carry

    lax.fori_loop(0, (nq - 4) // 2, pair, 0)
    scores(nq - 1, buf1)
    update(nq - 2, buf0, False)
    update(nq - 1, buf1, False)


def _band_attention(qkz, v_t, bias_t, batch, seq):
    t = qkz.shape[0]
    nq = seq // BAND_TQ
    assert nq >= 4 and nq % 2 == 0
    hp = BAND_HEADS // BAND_HPB
    return pl.pallas_call(
        functools.partial(_band_attn_kernel, nq=nq),
        name="band_attn",
        out_shape=jax.ShapeDtypeStruct((t, BAND_HEADS * BAND_HEAD_DIM), BF16),
        grid=(batch, hp),
        in_specs=[pl.BlockSpec((seq, LANES), lambda b, g: (b, g)),
                  pl.BlockSpec((seq, LANES), lambda b, g: (b, hp + g)),
                  pl.BlockSpec((LANES, seq), lambda b, g: (g, b)),
                  pl.BlockSpec((seq, LANES), lambda b, g: (b, 2 * hp + g)),
                  pl.BlockSpec((BAND_HPB, BAND_WIN, BAND_TQ), lambda b, g: (g, 0, 0))],
        out_specs=pl.BlockSpec((seq, LANES), lambda b, g: (b, g)),
        scratch_shapes=[pltpu.VMEM((BAND_HPB, seq + BAND_PAST, BAND_HEAD_DIM), BF16),
                        pltpu.VMEM((BAND_HPB, BAND_HEAD_DIM + ONES_ROWS, seq + BAND_PAST), BF16),
                        pltpu.VMEM((BAND_HPB, BAND_WIN, BAND_TQ), F32),
                        pltpu.VMEM((BAND_HPB, BAND_WIN, BAND_TQ), F32),
                        pltpu.VMEM((BAND_HPB, 1, BAND_TQ), F32),
                        pltpu.VMEM((BAND_HPB, 1, BAND_TQ), F32)],
        compiler_params=_params("parallel", "parallel"),
    )(qkz, qkz, v_t, qkz, bias_t)


def _pool_kernel(u_ref, up_ref, z_ref, w_ref, ps_ref, o_ref, *, tm, nb):
    i = pl.program_id(0)
    t0 = (i % nb) * tm
    has_history = t0 > 0
    t = t0 + lax.broadcasted_iota(jnp.int32, (tm, 1), 0)
    for g, win in enumerate(POOL_WINDOWS):
        sl = slice(g * POOL_GROUP, (g + 1) * POOL_GROUP)
        u = u_ref[:, sl]
        acc = jnp.concatenate([jnp.where(has_history, up_ref[:, sl], 0.0), u], axis=0)
        d = 1
        while d < win:
            acc = acc + pltpu.roll(acc, d, 0)
            d *= 2
        wsum = acc[POOL_HALO:, :]
        inv_cnt = 1.0 / jnp.minimum(t + 1, win).astype(F32)
        delta = (wsum * inv_cnt - u).astype(BF16)
        y = jnp.dot(delta, w_ref[g], preferred_element_type=F32) * ps_ref[:, sl]
        o_ref[:, sl] = (y * _silu(z_ref[:, sl])).astype(o_ref.dtype)


def _pool_mix(uz, seq, pool_w, pool_scale, tm=256):
    t = uz.shape[0]
    d = D_MODEL
    tm = min(tm, seq)
    nb = seq // tm
    hb = tm // POOL_HALO
    return pl.pallas_call(
        functools.partial(_pool_kernel, tm=tm, nb=nb),
        name="pool_mix",
        out_shape=jax.ShapeDtypeStruct((t, d), BF16),
        grid=(t // tm,),
        in_specs=[pl.BlockSpec((tm, d), lambda i: (i, 0)),
                  pl.BlockSpec((POOL_HALO, d), lambda i: (jnp.maximum(i * hb - 1, 0), 0)),
                  pl.BlockSpec((tm, d), lambda i: (i, 1)),
                  pl.BlockSpec((len(POOL_WINDOWS), POOL_GROUP, POOL_GROUP), lambda i: (0, 0, 0)),
                  pl.BlockSpec((1, d), lambda i: (0, 0))],
        out_specs=pl.BlockSpec((tm, d), lambda i: (i, 0)),
        compiler_params=_params("parallel"),
    )(uz, uz, uz, pool_w, pool_scale.reshape(1, d).astype(F32))


PROJ_TN = 1024


def _qkz_scale(qk_dim, branch):
    return jnp.concatenate([jnp.full((branch,), qk_dim ** -0.5 * LOG2E, F32), jnp.ones((2 * branch,), F32)])


def kernel(x, c, ada_w, ada_b, norm_pre, norm_post, diff_w_in, diff_w_out, diff_lambda_q1, diff_lambda_k1, diff_lambda_q2, diff_lambda_k2, diff_subln, band_w_in, band_w_out, band_rel_bias, fox_w_in, fox_w_out, fox_forget_bias, pool_w_in, pool_w_out, pool_group_w, pool_scale):
    batch, seq, d = x.shape
    depth = ada_w.shape[0]
    branch = d
    nblk = branch // PROJ_TN
    qkz_blocks = list(range(2 * nblk)) + list(range(3 * nblk, 4 * nblk))
    mod = _ada_mod(c, ada_w, ada_b)
    x2 = x.reshape(batch * seq, d)
    for i in range(depth):
        mod3 = mod[i].reshape(mod.shape[1], 1, 3 * d)
        kind = i % 4
        if kind == 0:
            lambda_init = 0.8 - 0.6 * math.exp(-0.3 * i)
            qkz, h = _norm_proj(x2, seq, norm_pre[i], mod3, diff_w_in.astype(BF16), qkz_blocks,
                                _qkz_scale(DIFF_QK_DIM, branch), emit_h=True)
            v_t = _proj_t(diff_w_in[:, 2 * branch:3 * branch].T.astype(BF16), h)
            o = _diff_attention(qkz, v_t, batch, seq, diff_lambda_q1, diff_lambda_k1, diff_lambda_q2,
                                diff_lambda_k2, diff_subln, lambda_init)
            w_out = diff_w_out
        elif kind == 1:
            qkz, h = _norm_proj(x2, seq, norm_pre[i], mod3, band_w_in.astype(BF16), qkz_blocks,
                                _qkz_scale(BAND_HEAD_DIM, branch), emit_h=True)
            v_t = _proj_t(band_w_in[:, 2 * branch:3 * branch].T.astype(BF16), h)
            o = _band_attention(qkz, v_t, _band_bias(band_rel_bias), batch, seq)
            w_out = band_w_out
        elif kind == 2:
            wf = jnp.pad(fox_w_in[:, 4 * branch:], ((0, 0), (0, LANES - FOX_HEADS))).astype(BF16)
            qkz, h, f_logit = _norm_proj(x2, seq, norm_pre[i], mod3, fox_w_in.astype(BF16), qkz_blocks,
                                         _qkz_scale(FOX_HEAD_DIM, branch), wf=wf, emit_h=True)
            v_t = _proj_t(fox_w_in[:, 2 * branch:3 * branch].T.astype(BF16), h)
            kx, qx = _fox_extend(_fox_gates(f_logit, fox_forget_bias, batch, seq))
            o = _fox_attention(qkz, v_t, kx, qx, batch, seq)
            w_out = fox_w_out
        else:
            uz, = _norm_proj(x2, seq, norm_pre[i], mod3, pool_w_in.astype(BF16), list(range(2 * nblk)),
                             jnp.ones((2 * branch,), F32), out_dtype=F32)
            o = _pool_mix(uz, seq, pool_group_w.astype(BF16), pool_scale)
            w_out = pool_w_out
        x2 = _out_proj_residual(o, w_out.astype(BF16), x2, seq, mod3, norm_post[i])
    return x2.reshape(batch, seq, d)
```

```python
import functools
import math

import jax
import jax.numpy as jnp
from jax import lax
from jax.experimental import pallas as pl
from jax.experimental.pallas import tpu as pltpu

F32 = jnp.float32
BF16 = jnp.bfloat16

D_MODEL = 2048
EPS = 1e-6
NEG_INF = -1e30
LOG2E = math.log2(math.e)
CHUNK = 64
DIFF_HEADS, DIFF_V_DIM, DIFF_QK_DIM = 8, 256, 128
BAND_HEADS, BAND_HEAD_DIM, BAND_PAST, REL_CLIP = 32, 64, 512, 256
FOX_HEADS, FOX_HEAD_DIM = 16, 128
POOL_WINDOWS = (2, 4, 8, 16)
POOL_GROUP = D_MODEL // len(POOL_WINDOWS)
POOL_HALO = 16

LANES = 128
VMEM_LIMIT = 52 * 1024 * 1024

NT_DIMS = (((1,), (1,)), ((), ()))


def _silu(x):
    return x / (1.0 + jnp.exp(-x))


def _params(*sem):
    return pltpu.CompilerParams(dimension_semantics=sem, vmem_limit_bytes=VMEM_LIMIT)


def _ada_kernel(c_ref, w_ref, b_ref, o_ref):
    ca = _silu(c_ref[...]).astype(BF16)
    o_ref[0] = jnp.dot(ca, w_ref[0].astype(BF16), preferred_element_type=F32) + b_ref[0]


def _ada_mod(c, ada_w, ada_b):
    b, d = c.shape
    bp = -(-b // 8) * 8
    depth, _, n3 = ada_w.shape
    tn = 768
    cp = jnp.pad(c, ((0, bp - b), (0, 0)))
    return pl.pallas_call(
        _ada_kernel,
        name="ada_mod",
        out_shape=jax.ShapeDtypeStruct((depth, bp, n3), F32),
        grid=(depth, n3 // tn),
        in_specs=[pl.BlockSpec((bp, d), lambda l, j: (0, 0)),
                  pl.BlockSpec((1, d, tn), lambda l, j: (l, 0, j)),
                  pl.BlockSpec((1, 1, tn), lambda l, j: (l, 0, j))],
        out_specs=pl.BlockSpec((1, bp, tn), lambda l, j: (l, 0, j)),
        compiler_params=_params("parallel", "parallel"),
    )(cp, ada_w, ada_b.reshape(depth, 1, n3))


def _norm_proj_kernel(x_ref, g_ref, sh_ref, sc_ref, w_ref, cs_ref, *rest, has_f, emit_h, sub):
    rest = list(rest)
    wf_ref = rest.pop(0) if has_f else None
    o_ref = rest.pop(0)
    hout_ref = rest.pop(0) if emit_h else None
    f_ref = rest.pop(0) if has_f else None
    h_ref = rest.pop(0)

    @pl.when(pl.program_id(1) == 0)
    def _():
        g = g_ref[...]
        sc = 1.0 + sc_ref[0]
        sh = sh_ref[0]
        for r in range(x_ref.shape[0] // sub):
            rows = slice(r * sub, (r + 1) * sub)
            x = x_ref[rows, :]
            y = x * lax.rsqrt(jnp.mean(x * x, axis=-1, keepdims=True) + EPS) * g
            h = (y * sc + sh).astype(BF16)
            h_ref[rows, :] = h
            if emit_h:
                hout_ref[rows, :] = h
            if has_f:
                f_ref[rows, :] = jnp.dot(h, wf_ref[...], preferred_element_type=F32)
            y = jnp.dot(h, w_ref[...], preferred_element_type=F32)
            o_ref[rows, :] = (y * cs_ref[...]).astype(o_ref.dtype)

    @pl.when(pl.program_id(1) > 0)
    def _():
        y = jnp.dot(h_ref[...], w_ref[...], preferred_element_type=F32)
        o_ref[...] = (y * cs_ref[...]).astype(o_ref.dtype)


def _norm_proj(x2, seq, g, mod3, w, col_blocks, col_scale, wf=None, emit_h=False, out_dtype=BF16,
               tm=1024, tn=1024):
    t, d = x2.shape
    tm = min(tm, seq)
    nb = seq // tm
    nj = len(col_blocks)
    n = nj * tn
    first, skip_from, skip = col_blocks[0], None, 0
    for a, b2 in zip(col_blocks, col_blocks[1:]):
        if b2 != a + 1:
            skip_from, skip = a - first + 1, b2 - a - 1
    if skip_from is None:
        wmap = lambda i, j: (0, j + first)
    else:
        wmap = lambda i, j: (0, j + first + jnp.where(j >= skip_from, skip, 0))
    in_specs = [pl.BlockSpec((tm, d), lambda i, j: (i, 0)),
                pl.BlockSpec((1, d), lambda i, j: (0, 0)),
                pl.BlockSpec((1, 1, d), lambda i, j: (i // nb, 0, 0)),
                pl.BlockSpec((1, 1, d), lambda i, j: (i // nb, 0, 1)),
                pl.BlockSpec((d, tn), wmap),
                pl.BlockSpec((1, tn), lambda i, j: (0, j))]
    out_shape = [jax.ShapeDtypeStruct((t, n), out_dtype)]
    out_specs = [pl.BlockSpec((tm, tn), lambda i, j: (i, j))]
    args = [x2, g.reshape(1, d), mod3, mod3, w, col_scale.reshape(1, n).astype(F32)]
    if wf is not None:
        in_specs.append(pl.BlockSpec((d, LANES), lambda i, j: (0, 0)))
        args.append(wf)
    if emit_h:
        out_shape.append(jax.ShapeDtypeStruct((t, d), BF16))
        out_specs.append(pl.BlockSpec((tm, d), lambda i, j: (i, 0)))
    if wf is not None:
        out_shape.append(jax.ShapeDtypeStruct((t, LANES), F32))
        out_specs.append(pl.BlockSpec((tm, LANES), lambda i, j: (i, 0)))
    return pl.pallas_call(
        functools.partial(_norm_proj_kernel, has_f=wf is not None, emit_h=emit_h, sub=min(256, tm)),
        name="norm_proj",
        out_shape=out_shape,
        grid=(t // tm, nj),
        in_specs=in_specs,
        out_specs=out_specs,
        scratch_shapes=[pltpu.VMEM((tm, d), BF16)],
        compiler_params=_params("parallel", "arbitrary"),
    )(*args)


def _proj_t_kernel(w_ref, h_ref, o_ref):
    o_ref[...] = lax.dot_general(w_ref[...], h_ref[...], NT_DIMS,
                                 preferred_element_type=F32).astype(o_ref.dtype)


def _proj_t(w_t, h, tn=1024, tm=2048):
    n, d = w_t.shape
    t = h.shape[0]
    tm = min(tm, t)
    return pl.pallas_call(
        _proj_t_kernel,
        name="proj_t",
        out_shape=jax.ShapeDtypeStruct((n, t), BF16),
        grid=(t // tm, n // tn),
        in_specs=[pl.BlockSpec((tn, d), lambda i, j: (j, 0)),
                  pl.BlockSpec((tm, d), lambda i, j: (i, 0))],
        out_specs=pl.BlockSpec((tn, tm), lambda i, j: (j, i)),
        compiler_params=_params("parallel", "arbitrary"),
    )(w_t, h)


def _out_proj_kernel(o_ref, w_ref, x_ref, gate_ref, g_ref, out_ref, *, sub):
    for r in range(o_ref.shape[0] // sub):
        rows = slice(r * sub, (r + 1) * sub)
        y = jnp.dot(o_ref[rows, :], w_ref[...], preferred_element_type=F32)
        yn = y * lax.rsqrt(jnp.mean(y * y, axis=-1, keepdims=True) + EPS) * g_ref[...]
        out_ref[rows, :] = x_ref[rows, :] + gate_ref[0] * yn


def _out_proj_residual(o2, w, x2, seq, mod3, g, tm=512):
    t, d = x2.shape
    tm = min(tm, seq)
    nb = seq // tm
    return pl.pallas_call(
        functools.partial(_out_proj_kernel, sub=min(256, tm)),
        name="out_proj",
        out_shape=jax.ShapeDtypeStruct((t, d), F32),
        grid=(t // tm,),
        in_specs=[pl.BlockSpec((tm, d), lambda i: (i, 0)),
                  pl.BlockSpec((d, d), lambda i: (0, 0), pipeline_mode=pl.Buffered(1)),
                  pl.BlockSpec((tm, d), lambda i: (i, 0)),
                  pl.BlockSpec((1, 1, d), lambda i: (i // nb, 0, 2)),
                  pl.BlockSpec((1, d), lambda i: (0, 0))],
        out_specs=pl.BlockSpec((tm, d), lambda i: (i, 0)),
        compiler_params=_params("parallel"),
    )(o2, w, x2, mod3, g.reshape(1, d))


def _flash_step_t(s_t, s_max, v_t, m_ref, l_ref, acc_ref, idx, cols):
    m_prev = m_ref[idx, :, cols]
    m_new = jnp.maximum(m_prev, s_max)
    alpha = jnp.exp2(m_prev - m_new)
    p_t = jnp.exp2(s_t - m_new)
    if l_ref is not None:
        l_ref[idx, :, cols] = alpha * l_ref[idx, :, cols] + jnp.sum(p_t, axis=0, keepdims=True)
    acc_ref[idx, :, cols] = (alpha * acc_ref[idx, :, cols]
                             + jnp.dot(v_t, p_t.astype(BF16), preferred_element_type=F32))
    m_ref[idx, :, cols] = m_new


def _flash_init(m_ref, l_ref, acc_ref):
    m_ref[...] = jnp.full(m_ref.shape, NEG_INF, F32)
    if l_ref is not None:
        l_ref[...] = jnp.zeros(l_ref.shape, F32)
    acc_ref[...] = jnp.zeros(acc_ref.shape, F32)


ONES_ROWS = 16


ATTN_TK = 512


def _mask_diagonal(s_ref, mx_ref, c0, granule):
    t = LANES
    kg = lax.broadcasted_iota(jnp.int32, (t, t), 0) // granule
    qg = lax.broadcasted_iota(jnp.int32, (t, t), 1) // granule
    visible = kg <= qg
    for r in range(ATTN_TK // t):
        rows = slice(r * t, (r + 1) * t)
        if r:
            s_ref[rows, c0:c0 + r * t] = jnp.full((t, r * t), NEG_INF, F32)
        diag = slice(c0 + r * t, c0 + (r + 1) * t)
        s_ref[rows, diag] = jnp.where(visible, s_ref[rows, diag], NEG_INF)
    cols = slice(c0, c0 + ATTN_TK)
    mx_ref[:, cols] = jnp.max(s_ref[:, cols], axis=0, keepdims=True)


def _causal_blocks(nq, begin, scores, mask, update, finish, buf0, buf1):
    every, late = slice(None), slice(ATTN_TK, 2 * ATTN_TK)
    begin(0)
    scores(0, buf0, every)

    def pairs(j, n):
        for _ in range(n):
            scores(j + 1, buf1, every)
            update(j, buf0, every)
            scores(j + 2, buf0, every)
            update(j + 1, buf1, every)
            j = j + 2

    def qblock(qi, carry):
        def quad(jj, c):
            pairs(4 * jj, 2)
            return c

        lax.fori_loop(0, qi // 2, quad, 0)

        @pl.when(qi % 2 == 1)
        def _():
            pairs(2 * qi - 2, 1)

        scores(2 * qi + 1, buf1, late)
        mask(buf0, 0)
        update(2 * qi, buf0, every)
        mask(buf1, ATTN_TK)
        update(2 * qi + 1, buf1, late)
        finish(qi)
        begin(jnp.minimum(qi + 1, nq - 1))
        scores(0, buf0, every)
        return carry

    lax.fori_loop(0, nq, qblock, 0)


def _diff_attn_kernel(q_ref, k_ref, vt_ref, z_ref, lq1_ref, lk1_ref, lq2_ref, lk2_ref, g_ref,
                      o_ref, m_ref, l_ref, acc_ref, s0_ref, s1_ref, mx0_ref, mx1_ref, qc_ref,
                      *, nq, lambda_init):
    tq = 2 * ATTN_TK
    lam = (jnp.exp(jnp.sum(lq1_ref[...] * lk1_ref[...], axis=-1, keepdims=True))
           - jnp.exp(jnp.sum(lq2_ref[...] * lk2_ref[...], axis=-1, keepdims=True)) + lambda_init)
    g = g_ref[...] * (1.0 - lambda_init)

    def begin(qi):
        qc_ref[...] = q_ref[pl.ds(pl.multiple_of(qi * tq, tq), tq), :]
        _flash_init(m_ref, l_ref, acc_ref)

    def finish(qi):
        rows = pl.ds(pl.multiple_of(qi * tq, tq), tq)
        o_t = acc_ref[0] * (1.0 / l_ref[0]) - acc_ref[1] * (lam / l_ref[1])
        o = o_t.T
        on = o * lax.rsqrt(jnp.mean(o * o, axis=-1, keepdims=True) + EPS) * g
        o_ref[rows, :] = (on * _silu(z_ref[rows, :].astype(F32))).astype(o_ref.dtype)

    def scores(ki, buf, cols):
        s_ref, mx_ref = buf
        k = k_ref[pl.ds(pl.multiple_of(ki * ATTN_TK, ATTN_TK), ATTN_TK), :]
        for m in range(2):
            sl = slice(m * DIFF_QK_DIM, (m + 1) * DIFF_QK_DIM)
            s_t = lax.dot_general(k[:, sl], qc_ref[cols, sl], NT_DIMS, preferred_element_type=F32)
            s_ref[m, :, cols] = s_t
            mx_ref[m, :, cols] = jnp.max(s_t, axis=0, keepdims=True)

    def mask(buf, c0):
        s_ref, mx_ref = buf
        for m in range(2):
            _mask_diagonal(s_ref.at[m], mx_ref.at[m], c0, CHUNK)

    def update(ki, buf, cols):
        s_ref, mx_ref = buf
        v_t = vt_ref[:, pl.ds(pl.multiple_of(ki * ATTN_TK, ATTN_TK), ATTN_TK)]
        for m in range(2):
            _flash_step_t(s_ref[m, :, cols], mx_ref[m, :, cols], v_t, m_ref, l_ref, acc_ref, m, cols)

    _causal_blocks(nq, begin, scores, mask, update, finish, (s0_ref, mx0_ref), (s1_ref, mx1_ref))


def _diff_attention(qkz, v_t, batch, seq, lq1, lk1, lq2, lk2, subln_g, lambda_init):
    t = qkz.shape[0]
    tq = 2 * ATTN_TK
    nq = seq // tq
    h, dv = DIFF_HEADS, DIFF_V_DIM
    vec = lambda a: a.reshape(1, -1).astype(F32)
    small = lambda n: pl.BlockSpec((1, n), lambda b, hh: (0, 0))
    once = pl.Buffered(1)
    return pl.pallas_call(
        functools.partial(_diff_attn_kernel, nq=nq, lambda_init=lambda_init),
        name="diff_attn",
        out_shape=jax.ShapeDtypeStruct((t, h * dv), BF16),
        grid=(batch, h),
        in_specs=[pl.BlockSpec((seq, dv), lambda b, hh: (b, hh)),
                  pl.BlockSpec((seq, dv), lambda b, hh: (b, h + hh), pipeline_mode=once),
                  pl.BlockSpec((dv, seq), lambda b, hh: (hh, b), pipeline_mode=once),
                  pl.BlockSpec((seq, dv), lambda b, hh: (b, 2 * h + hh)),
                  small(DIFF_QK_DIM), small(DIFF_QK_DIM), small(DIFF_QK_DIM), small(DIFF_QK_DIM),
                  small(dv)],
        out_specs=pl.BlockSpec((seq, dv), lambda b, hh: (b, hh)),
        scratch_shapes=[pltpu.VMEM((2, 1, tq), F32), pltpu.VMEM((2, 1, tq), F32),
                        pltpu.VMEM((2, dv, tq), F32),
                        pltpu.VMEM((2, ATTN_TK, tq), F32), pltpu.VMEM((2, ATTN_TK, tq), F32),
                        pltpu.VMEM((2, 1, tq), F32), pltpu.VMEM((2, 1, tq), F32),
                        pltpu.VMEM((tq, dv), BF16)],
        compiler_params=_params("parallel", "parallel"),
    )(qkz, qkz, v_t, qkz, vec(lq1), vec(lk1), vec(lq2), vec(lk2), vec(subln_g))


FOX_PIECES = 3


def _fox_gate_kernel(f_ref, b_ref, o_ref):
    x = f_ref[...] + b_ref[...]
    lf = jnp.minimum(x, 0.0) - jnp.log1p(jnp.exp(-jnp.abs(x)))
    n = lf.shape[0]
    row = lax.broadcasted_iota(jnp.int32, lf.shape, 0)
    d = 1
    while d < n:
        lf = lf + jnp.where(row >= d, pltpu.roll(lf, d, 0), 0.0)
        d *= 2
    rest = lf * LOG2E
    lane = lax.broadcasted_iota(jnp.int32, lf.shape, 1)
    out = jnp.zeros(lf.shape, F32)
    for p in range(FOX_PIECES):
        piece = rest.astype(BF16).astype(F32)
        rest = rest - piece
        moved = piece if p == 0 else pltpu.roll(piece, p * FOX_HEADS, 1)
        out = jnp.where((lane >= p * FOX_HEADS) & (lane < (p + 1) * FOX_HEADS), moved, out)
    o_ref[...] = out.astype(BF16)


def _fox_gates(f_logit, bias, batch, seq):
    bpad = jnp.zeros((1, LANES), F32).at[0, :FOX_HEADS].set(bias.astype(F32))
    return pl.pallas_call(
        _fox_gate_kernel,
        name="fox_gates",
        out_shape=jax.ShapeDtypeStruct((batch * seq, LANES), BF16),
        grid=(batch,),
        in_specs=[pl.BlockSpec((seq, LANES), lambda b: (b, 0)),
                  pl.BlockSpec((1, LANES), lambda b: (0, 0))],
        out_specs=pl.BlockSpec((seq, LANES), lambda b: (b, 0)),
        compiler_params=_params("parallel"),
    )(f_logit, bpad)


def _fox_place_kernel(f_ref, pk_ref, pq_ref, ck_ref, cq_ref, kx_ref, qx_ref):
    f = f_ref[...]
    kx_ref[...] = (jnp.dot(f, pk_ref[...], preferred_element_type=F32) + ck_ref[...]).astype(BF16)
    qx_ref[...] = (jnp.dot(f, pq_ref[...], preferred_element_type=F32) + cq_ref[...]).astype(BF16)


def _fox_extend(pieces, tm=1024):
    t = pieces.shape[0]
    tm = min(tm, t)
    n = FOX_HEADS * LANES
    src = jnp.arange(LANES)[:, None]
    dst = jnp.arange(n)[None, :]
    p_src, h_src = src // FOX_HEADS, src % FOX_HEADS
    h_dst, c_dst = dst // LANES, dst % LANES
    live = (p_src < FOX_PIECES) & (h_src == h_dst)
    pk = jnp.where(live & (c_dst == p_src), -1.0, 0.0).astype(BF16)
    pq = jnp.where(live & (c_dst == p_src + FOX_PIECES), 1.0, 0.0).astype(BF16)
    ck = ((c_dst >= FOX_PIECES) & (c_dst < 2 * FOX_PIECES)).astype(F32)
    cq = (c_dst < FOX_PIECES).astype(F32)
    mat = pl.BlockSpec((LANES, n), lambda i: (0, 0))
    vec = pl.BlockSpec((1, n), lambda i: (0, 0))
    return pl.pallas_call(
        _fox_place_kernel,
        name="fox_extend",
        out_shape=[jax.ShapeDtypeStruct((t, n), BF16), jax.ShapeDtypeStruct((t, n), BF16)],
        grid=(t // tm,),
        in_specs=[pl.BlockSpec((tm, LANES), lambda i: (i, 0)), mat, mat, vec, vec],
        out_specs=[pl.BlockSpec((tm, n), lambda i: (i, 0)), pl.BlockSpec((tm, n), lambda i: (i, 0))],
        compiler_params=_params("parallel"),
    )(pieces, pk, pq, ck, cq)


def _fox_attn_kernel(q_ref, qx_ref, k_ref, kx_ref, vt_ref, z_ref, o_ref, m_ref, acc_ref,
                     s0_ref, s1_ref, mx0_ref, mx1_ref, qt_ref, vte_ref, *, nq):
    tq = 2 * ATTN_TK
    dh = FOX_HEAD_DIM
    vte_ref[:dh, :] = vt_ref[...]
    vte_ref[dh:, :] = jnp.ones((ONES_ROWS, vte_ref.shape[1]), BF16)

    def begin(qi):
        rows = pl.ds(pl.multiple_of(qi * tq, tq), tq)
        qt_ref[:dh, :] = q_ref[rows, :].T
        qt_ref[dh:, :] = qx_ref[rows, :].T
        _flash_init(m_ref, None, acc_ref)

    def finish(qi):
        rows = pl.ds(pl.multiple_of(qi * tq, tq), tq)
        o = (acc_ref[0, :dh, :] * (1.0 / acc_ref[0, dh:dh + 1, :])).T
        o_ref[rows, :] = (o * _silu(z_ref[rows, :].astype(F32))).astype(o_ref.dtype)

    def scores(ki, buf, cols):
        s_ref, mx_ref = buf
        off = pl.multiple_of(ki * ATTN_TK, ATTN_TK)
        k = jnp.concatenate([k_ref[pl.ds(off, ATTN_TK), :], kx_ref[pl.ds(off, ATTN_TK), :]], axis=1)
        s_t = jnp.dot(k, qt_ref[:, cols], preferred_element_type=F32)
        s_ref[:, cols] = s_t
        mx_ref[:, cols] = jnp.max(s_t, axis=0, keepdims=True)

    def mask(buf, c0):
        _mask_diagonal(buf[0], buf[1], c0, 1)

    def update(ki, buf, cols):
        s_ref, mx_ref = buf
        v_t = vte_ref[:, pl.ds(pl.multiple_of(ki * ATTN_TK, ATTN_TK), ATTN_TK)]
        _flash_step_t(s_ref[:, cols], mx_ref[:, cols], v_t, m_ref, None, acc_ref, 0, cols)

    _causal_blocks(nq, begin, scores, mask, update, finish, (s0_ref, mx0_ref), (s1_ref, mx1_ref))


def _fox_attention(qkz, v_t, kx, qx, batch, seq):
    t = qkz.shape[0]
    tq = 2 * ATTN_TK
    nq = seq // tq
    h, dh = FOX_HEADS, FOX_HEAD_DIM
    return pl.pallas_call(
        functools.partial(_fox_attn_kernel, nq=nq),
        name="fox_attn",
        out_shape=jax.ShapeDtypeStruct((t, h * dh), BF16),
        grid=(batch, h),
        in_specs=[pl.BlockSpec((seq, dh), lambda b, hh: (b, hh)),
                  pl.BlockSpec((seq, LANES), lambda b, hh: (b, hh)),
                  pl.BlockSpec((seq, dh), lambda b, hh: (b, h + hh)),
                  pl.BlockSpec((seq, LANES), lambda b, hh: (b, hh)),
                  pl.BlockSpec((dh, seq), lambda b, hh: (hh, b)),
                  pl.BlockSpec((seq, dh), lambda b, hh: (b, 2 * h + hh))],
        out_specs=pl.BlockSpec((seq, dh), lambda b, hh: (b, hh)),
        scratch_shapes=[pltpu.VMEM((1, 1, tq), F32),
                        pltpu.VMEM((1, dh + ONES_ROWS, tq), F32),
                        pltpu.VMEM((ATTN_TK, tq), F32), pltpu.VMEM((ATTN_TK, tq), F32),
                        pltpu.VMEM((1, tq), F32), pltpu.VMEM((1, tq), F32),
                        pltpu.VMEM((dh + LANES, tq), BF16),
                        pltpu.VMEM((dh + ONES_ROWS, seq), BF16)],
        compiler_params=_params("parallel", "parallel"),
    )(qkz, qx, qkz, kx, v_t, qkz)


BAND_TQ = 256
BAND_WIN = BAND_PAST + BAND_TQ
BAND_HPB = LANES // BAND_HEAD_DIM
BAND_REV = 1024


def _band_bias_kernel(row_ref, o_ref):
    x = jnp.broadcast_to(row_ref[0], (BAND_WIN, BAND_REV))
    x = pltpu.roll(x, BAND_REV - BAND_WIN, 1, stride=1, stride_axis=0)
    x = x[:, :BAND_TQ]
    kc = lax.broadcasted_iota(jnp.int32, x.shape, 0) // CHUNK
    qc = lax.broadcasted_iota(jnp.int32, x.shape, 1) // CHUNK
    band = (kc >= qc) & (kc <= qc + BAND_PAST // CHUNK)
    o_ref[0] = jnp.where(band, x * LOG2E, NEG_INF)


def _band_bias(rel_table):
    nh = rel_table.shape[0]
    t = rel_table.astype(F32)
    row = jnp.concatenate([t, jnp.broadcast_to(t[:, 2 * REL_CLIP:], (nh, BAND_REV - t.shape[1]))], axis=1)
    return pl.pallas_call(
        _band_bias_kernel,
        name="band_bias",
        out_shape=jax.ShapeDtypeStruct((nh, BAND_WIN, BAND_TQ), F32),
        grid=(nh,),
        in_specs=[pl.BlockSpec((1, 1, BAND_REV), lambda h: (h, 0, 0))],
        out_specs=pl.BlockSpec((1, BAND_WIN, BAND_TQ), lambda h: (h, 0, 0)),
        compiler_params=_params("parallel"),
    )(row.reshape(nh, 1, BAND_REV))


def _band_attn_kernel(q_ref, k_ref, vt_ref, z_ref, bias_ref, o_ref, kp_ref, vtp_ref, s0_ref, s1_ref,
                      mx0_ref, mx1_ref, *, nq):
    hd = BAND_HEAD_DIM
    for hh in range(BAND_HPB):
        kp_ref[hh, :BAND_PAST, :] = jnp.zeros((BAND_PAST, hd), BF16)
        kp_ref[hh, BAND_PAST:, :] = k_ref[:, hh * hd:(hh + 1) * hd]
        vtp_ref[hh, :hd, :BAND_PAST] = jnp.zeros((hd, BAND_PAST), BF16)
        vtp_ref[hh, :hd, BAND_PAST:] = vt_ref[hh * hd:(hh + 1) * hd, :]
        vtp_ref[hh, hd:, :] = jnp.ones((ONES_ROWS, vtp_ref.shape[2]), BF16)

    def scores(j, buf):
        s_ref, mx_ref = buf
        start = pl.multiple_of(j * BAND_TQ, BAND_TQ)
        q = q_ref[pl.ds(start, BAND_TQ), :]
        for hh in range(BAND_HPB):
            s_t = lax.dot_general(kp_ref[hh, pl.ds(start, BAND_WIN), :], q[:, hh * hd:(hh + 1) * hd],
                                  NT_DIMS, preferred_element_type=F32) + bias_ref[hh]
            s_ref[hh] = s_t
            mx_ref[hh] = jnp.max(s_t, axis=0, keepdims=True)

    def update(j, buf, masked):
        s_ref, mx_ref = buf
        start = pl.multiple_of(j * BAND_TQ, BAND_TQ)
        outs = []
        for hh in range(BAND_HPB):
            s_t = s_ref[hh]
            s_max = mx_ref[hh]
            if masked:
                kpos = start - BAND_PAST + lax.broadcasted_iota(jnp.int32, (BAND_WIN, 1), 0)
                s_t = jnp.where(kpos >= 0, s_t, NEG_INF)
                s_max = jnp.max(s_t, axis=0, keepdims=True)
            p_t = jnp.exp2(s_t - s_max).astype(BF16)
            ol = jnp.dot(vtp_ref[hh, :, pl.ds(start, BAND_WIN)], p_t, preferred_element_type=F32)
            outs.append(ol[:hd] * (1.0 / ol[hd:hd + 1]))
        o = jnp.concatenate(outs, axis=0).T
        z = z_ref[pl.ds(start, BAND_TQ), :].astype(F32)
        o_ref[pl.ds(start, BAND_TQ), :] = (o * _silu(z)).astype(o_ref.dtype)

    buf0, buf1 = (s0_ref, mx0_ref), (s1_ref, mx1_ref)
    scores(0, buf0)
    scores(1, buf1)
    update(0, buf0, True)
    scores(2, buf0)
    update(1, buf1, True)

    def pairs(j, n):
        for _ in range(n):
            scores(j + 1, buf1)
            update(j, buf0, False)
            scores(j + 2, buf0)
            update(j + 1, buf1, False)
            j = j + 2

    def quad(jj, carry):
        pairs(2 + 4 * jj, 2)
        return carry

    n_pairs = (nq - 4) // 2
    lax.fori_loop(0, n_pairs // 2, quad, 0)
    pairs(2 + 4 * (n_pairs // 2), n_pairs % 2)
    scores(nq - 1, buf1)
    update(nq - 2, buf0, False)
    update(nq - 1, buf1, False)


def _band_attention(qkz, v_t, bias_t, batch, seq):
    t = qkz.shape[0]
    nq = seq // BAND_TQ
    assert nq >= 4 and nq % 2 == 0
    hp = BAND_HEADS // BAND_HPB
    return pl.pallas_call(
        functools.partial(_band_attn_kernel, nq=nq),
        name="band_attn",
        out_shape=jax.ShapeDtypeStruct((t, BAND_HEADS * BAND_HEAD_DIM), BF16),
        grid=(batch, hp),
        in_specs=[pl.BlockSpec((seq, LANES), lambda b, g: (b, g)),
                  pl.BlockSpec((seq, LANES), lambda b, g: (b, hp + g)),
                  pl.BlockSpec((LANES, seq), lambda b, g: (g, b)),
                  pl.BlockSpec((seq, LANES), lambda b, g: (b, 2 * hp + g)),
                  pl.BlockSpec((BAND_HPB, BAND_WIN, BAND_TQ), lambda b, g: (g, 0, 0))],
        out_specs=pl.BlockSpec((seq, LANES), lambda b, g: (b, g)),
        scratch_shapes=[pltpu.VMEM((BAND_HPB, seq + BAND_PAST, BAND_HEAD_DIM), BF16),
                        pltpu.VMEM((BAND_HPB, BAND_HEAD_DIM + ONES_ROWS, seq + BAND_PAST), BF16),
                        pltpu.VMEM((BAND_HPB, BAND_WIN, BAND_TQ), F32),
                        pltpu.VMEM((BAND_HPB, BAND_WIN, BAND_TQ), F32),
                        pltpu.VMEM((BAND_HPB, 1, BAND_TQ), F32),
                        pltpu.VMEM((BAND_HPB, 1, BAND_TQ), F32)],
        compiler_params=_params("parallel", "parallel"),
    )(qkz, qkz, v_t, qkz, bias_t)


def _pool_kernel(u_ref, up_ref, z_ref, w_ref, ps_ref, o_ref, *, tm, nb):
    i = pl.program_id(0)
    t0 = (i % nb) * tm
    has_history = t0 > 0
    t = t0 + lax.broadcasted_iota(jnp.int32, (tm, 1), 0)
    for g, win in enumerate(POOL_WINDOWS):
        sl = slice(g * POOL_GROUP, (g + 1) * POOL_GROUP)
        u = u_ref[:, sl]
        acc = jnp.concatenate([jnp.where(has_history, up_ref[:, sl], 0.0), u], axis=0)
        d = 1
        while d < win:
            acc = acc + pltpu.roll(acc, d, 0)
            d *= 2
        wsum = acc[POOL_HALO:, :]
        inv_cnt = 1.0 / jnp.minimum(t + 1, win).astype(F32)
        delta = (wsum * inv_cnt - u).astype(BF16)
        y = jnp.dot(delta, w_ref[g], preferred_element_type=F32) * ps_ref[:, sl]
        o_ref[:, sl] = (y * _silu(z_ref[:, sl])).astype(o_ref.dtype)


def _pool_mix(uz, seq, pool_w, pool_scale, tm=256):
    t = uz.shape[0]
    d = D_MODEL
    tm = min(tm, seq)
    nb = seq // tm
    hb = tm // POOL_HALO
    return pl.pallas_call(
        functools.partial(_pool_kernel, tm=tm, nb=nb),
        name="pool_mix",
        out_shape=jax.ShapeDtypeStruct((t, d), BF16),
        grid=(t // tm,),
        in_specs=[pl.BlockSpec((tm, d), lambda i: (i, 0)),
                  pl.BlockSpec((POOL_HALO, d), lambda i: (jnp.maximum(i * hb - 1, 0), 0)),
                  pl.BlockSpec((tm, d), lambda i: (i, 1)),
                  pl.BlockSpec((len(POOL_WINDOWS), POOL_GROUP, POOL_GROUP), lambda i: (0, 0, 0)),
                  pl.BlockSpec((1, d), lambda i: (0, 0))],
        out_specs=pl.BlockSpec((tm, d), lambda i: (i, 0)),
        compiler_params=_params("parallel"),
    )(uz, uz, uz, pool_w, pool_scale.reshape(1, d).astype(F32))


PROJ_TN = 1024


def _transpose_cast_kernel(x_ref, o_ref):
    o_ref[...] = x_ref[...].T.astype(o_ref.dtype)


def _bf16_weights(w_in, branch, tile=512):
    d = w_in.shape[0]
    first = 2 * branch // tile
    w_vt = pl.pallas_call(
        _transpose_cast_kernel,
        name="transpose_cast",
        out_shape=jax.ShapeDtypeStruct((branch, d), BF16),
        grid=(d // tile, branch // tile),
        in_specs=[pl.BlockSpec((tile, tile), lambda i, j: (i, first + j))],
        out_specs=pl.BlockSpec((tile, tile), lambda i, j: (j, i)),
        compiler_params=_params("parallel", "parallel"),
    )(w_in)
    return w_in.astype(BF16), w_vt


def _qkz_scale(qk_dim, branch):
    return jnp.concatenate([jnp.full((branch,), qk_dim ** -0.5 * LOG2E, F32), jnp.ones((2 * branch,), F32)])


def kernel(x, c, ada_w, ada_b, norm_pre, norm_post, diff_w_in, diff_w_out, diff_lambda_q1, diff_lambda_k1, diff_lambda_q2, diff_lambda_k2, diff_subln, band_w_in, band_w_out, band_rel_bias, fox_w_in, fox_w_out, fox_forget_bias, pool_w_in, pool_w_out, pool_group_w, pool_scale):
    batch, seq, d = x.shape
    depth = ada_w.shape[0]
    branch = d
    nblk = branch // PROJ_TN
    qkz_blocks = list(range(2 * nblk)) + list(range(3 * nblk, 4 * nblk))
    mod = _ada_mod(c, ada_w, ada_b)
    x2 = x.reshape(batch * seq, d)
    for i in range(depth):
        mod3 = mod[i].reshape(mod.shape[1], 1, 3 * d)
        kind = i % 4
        if kind == 0:
            lambda_init = 0.8 - 0.6 * math.exp(-0.3 * i)
            w, w_vt = _bf16_weights(diff_w_in, branch)
            qkz, h = _norm_proj(x2, seq, norm_pre[i], mod3, w, qkz_blocks,
                                _qkz_scale(DIFF_QK_DIM, branch), emit_h=True)
            v_t = _proj_t(w_vt, h)
            o = _diff_attention(qkz, v_t, batch, seq, diff_lambda_q1, diff_lambda_k1, diff_lambda_q2,
                                diff_lambda_k2, diff_subln, lambda_init)
            w_out = diff_w_out
        elif kind == 1:
            w, w_vt = _bf16_weights(band_w_in, branch)
            qkz, h = _norm_proj(x2, seq, norm_pre[i], mod3, w, qkz_blocks,
                                _qkz_scale(BAND_HEAD_DIM, branch), emit_h=True)
            v_t = _proj_t(w_vt, h)
            o = _band_attention(qkz, v_t, _band_bias(band_rel_bias), batch, seq)
            w_out = band_w_out
        elif kind == 2:
            wf = jnp.pad(fox_w_in[:, 4 * branch:], ((0, 0), (0, LANES - FOX_HEADS))).astype(BF16)
            w, w_vt = _bf16_weights(fox_w_in, branch)
            qkz, h, f_logit = _norm_proj(x2, seq, norm_pre[i], mod3, w, qkz_blocks,
                                         _qkz_scale(FOX_HEAD_DIM, branch), wf=wf, emit_h=True)
            v_t = _proj_t(w_vt, h)
            kx, qx = _fox_extend(_fox_gates(f_logit, fox_forget_bias, batch, seq))
            o = _fox_attention(qkz, v_t, kx, qx, batch, seq)
            w_out = fox_w_out
        else:
            uz, = _norm_proj(x2, seq, norm_pre[i], mod3, pool_w_in.astype(BF16), list(range(2 * nblk)),
                             jnp.ones((2 * branch,), F32), out_dtype=F32)
            o = _pool_mix(uz, seq, pool_group_w.astype(BF16), pool_scale)
            w_out = pool_w_out
        x2 = _out_proj_residual(o, w_out.astype(BF16), x2, seq, mod3, norm_post[i])
    return x2.reshape(batch, seq, d)
```

```python
import functools
import math

import jax
import jax.numpy as jnp
from jax import lax
from jax.experimental import pallas as pl
from jax.experimental.pallas import tpu as pltpu

F32 = jnp.float32
BF16 = jnp.bfloat16

D_MODEL = 2048
EPS = 1e-6
NEG_INF = -1e30
LOG2E = math.log2(math.e)
CHUNK = 64
DIFF_HEADS, DIFF_V_DIM, DIFF_QK_DIM = 8, 256, 128
BAND_HEADS, BAND_HEAD_DIM, BAND_PAST, REL_CLIP = 32, 64, 512, 256
FOX_HEADS, FOX_HEAD_DIM = 16, 128
POOL_WINDOWS = (2, 4, 8, 16)
POOL_GROUP = D_MODEL // len(POOL_WINDOWS)
POOL_HALO = 16

LANES = 128
VMEM_LIMIT = 52 * 1024 * 1024

NT_DIMS = (((1,), (1,)), ((), ()))


def _silu(x):
    return x / (1.0 + jnp.exp(-x))


def _params(*sem):
    return pltpu.CompilerParams(dimension_semantics=sem, vmem_limit_bytes=VMEM_LIMIT)


def _ada_kernel(c_ref, w_ref, b_ref, o_ref):
    ca = _silu(c_ref[...]).astype(BF16)
    o_ref[0] = jnp.dot(ca, w_ref[0].astype(BF16), preferred_element_type=F32) + b_ref[0]


def _ada_mod(c, ada_w, ada_b):
    b, d = c.shape
    bp = -(-b // 8) * 8
    depth, _, n3 = ada_w.shape
    tn = 768
    cp = jnp.pad(c, ((0, bp - b), (0, 0)))
    return pl.pallas_call(
        _ada_kernel,
        name="ada_mod",
        out_shape=jax.ShapeDtypeStruct((depth, bp, n3), F32),
        grid=(depth, n3 // tn),
        in_specs=[pl.BlockSpec((bp, d), lambda l, j: (0, 0)),
                  pl.BlockSpec((1, d, tn), lambda l, j: (l, 0, j)),
                  pl.BlockSpec((1, 1, tn), lambda l, j: (l, 0, j))],
        out_specs=pl.BlockSpec((1, bp, tn), lambda l, j: (l, 0, j)),
        compiler_params=_params("parallel", "parallel"),
    )(cp, ada_w, ada_b.reshape(depth, 1, n3))


def _norm_proj_kernel(x_ref, g_ref, sh_ref, sc_ref, w_ref, cs_ref, *rest, has_f, emit_h, sub):
    rest = list(rest)
    wf_ref = rest.pop(0) if has_f else None
    o_ref = rest.pop(0)
    hout_ref = rest.pop(0) if emit_h else None
    f_ref = rest.pop(0) if has_f else None
    h_ref = rest.pop(0)

    @pl.when(pl.program_id(1) == 0)
    def _():
        g = g_ref[...]
        sc = 1.0 + sc_ref[0]
        sh = sh_ref[0]
        for r in range(x_ref.shape[0] // sub):
            rows = slice(r * sub, (r + 1) * sub)
            x = x_ref[rows, :]
            y = x * lax.rsqrt(jnp.mean(x * x, axis=-1, keepdims=True) + EPS) * g
            h = (y * sc + sh).astype(BF16)
            h_ref[rows, :] = h
            if emit_h:
                hout_ref[rows, :] = h
            if has_f:
                f_ref[rows, :] = jnp.dot(h, wf_ref[...], preferred_element_type=F32)
            y = jnp.dot(h, w_ref[...], preferred_element_type=F32)
            o_ref[rows, :] = (y * cs_ref[...]).astype(o_ref.dtype)

    @pl.when(pl.program_id(1) > 0)
    def _():
        y = jnp.dot(h_ref[...], w_ref[...], preferred_element_type=F32)
        o_ref[...] = (y * cs_ref[...]).astype(o_ref.dtype)


def _norm_proj(x2, seq, g, mod3, w, col_blocks, col_scale, wf=None, emit_h=False, out_dtype=BF16,
               tm=1024, tn=1024):
    t, d = x2.shape
    tm = min(tm, seq)
    nb = seq // tm
    nj = len(col_blocks)
    n = nj * tn
    first, skip_from, skip = col_blocks[0], None, 0
    for a, b2 in zip(col_blocks, col_blocks[1:]):
        if b2 != a + 1:
            skip_from, skip = a - first + 1, b2 - a - 1
    if skip_from is None:
        wmap = lambda i, j: (0, j + first)
    else:
        wmap = lambda i, j: (0, j + first + jnp.where(j >= skip_from, skip, 0))
    in_specs = [pl.BlockSpec((tm, d), lambda i, j: (i, 0)),
                pl.BlockSpec((1, d), lambda i, j: (0, 0)),
                pl.BlockSpec((1, 1, d), lambda i, j: (i // nb, 0, 0)),
                pl.BlockSpec((1, 1, d), lambda i, j: (i // nb, 0, 1)),
                pl.BlockSpec((d, tn), wmap),
                pl.BlockSpec((1, tn), lambda i, j: (0, j))]
    out_shape = [jax.ShapeDtypeStruct((t, n), out_dtype)]
    out_specs = [pl.BlockSpec((tm, tn), lambda i, j: (i, j))]
    args = [x2, g.reshape(1, d), mod3, mod3, w, col_scale.reshape(1, n).astype(F32)]
    if wf is not None:
        in_specs.append(pl.BlockSpec((d, LANES), lambda i, j: (0, 0)))
        args.append(wf)
    if emit_h:
        out_shape.append(jax.ShapeDtypeStruct((t, d), BF16))
        out_specs.append(pl.BlockSpec((tm, d), lambda i, j: (i, 0)))
    if wf is not None:
        out_shape.append(jax.ShapeDtypeStruct((t, LANES), F32))
        out_specs.append(pl.BlockSpec((tm, LANES), lambda i, j: (i, 0)))
    return pl.pallas_call(
        functools.partial(_norm_proj_kernel, has_f=wf is not None, emit_h=emit_h, sub=min(256, tm)),
        name="norm_proj",
        out_shape=out_shape,
        grid=(t // tm, nj),
        in_specs=in_specs,
        out_specs=out_specs,
        scratch_shapes=[pltpu.VMEM((tm, d), BF16)],
        compiler_params=_params("parallel", "arbitrary"),
    )(*args)


def _proj_t_kernel(w_ref, h_ref, o_ref):
    o_ref[...] = lax.dot_general(w_ref[...], h_ref[...], NT_DIMS,
                                 preferred_element_type=F32).astype(o_ref.dtype)


def _proj_t(w_t, h, tn=1024, tm=2048):
    n, d = w_t.shape
    t = h.shape[0]
    tm = min(tm, t)
    return pl.pallas_call(
        _proj_t_kernel,
        name="proj_t",
        out_shape=jax.ShapeDtypeStruct((n, t), BF16),
        grid=(t // tm, n // tn),
        in_specs=[pl.BlockSpec((tn, d), lambda i, j: (j, 0)),
                  pl.BlockSpec((tm, d), lambda i, j: (i, 0))],
        out_specs=pl.BlockSpec((tn, tm), lambda i, j: (j, i)),
        compiler_params=_params("parallel", "arbitrary"),
    )(w_t, h)


def _out_proj_kernel(o_ref, w_ref, x_ref, gate_ref, g_ref, out_ref, *, sub):
    blocks = [slice(r * sub, (r + 1) * sub) for r in range(o_ref.shape[0] // sub)]
    ys = [jnp.dot(o_ref[rows, :], w_ref[...], preferred_element_type=F32) for rows in blocks]
    for rows, y in zip(blocks, ys):
        yn = y * lax.rsqrt(jnp.mean(y * y, axis=-1, keepdims=True) + EPS) * g_ref[...]
        out_ref[rows, :] = x_ref[rows, :] + gate_ref[0] * yn


def _out_proj_residual(o2, w, x2, seq, mod3, g, tm=512):
    t, d = x2.shape
    tm = min(tm, seq)
    nb = seq // tm
    return pl.pallas_call(
        functools.partial(_out_proj_kernel, sub=min(256, tm)),
        name="out_proj",
        out_shape=jax.ShapeDtypeStruct((t, d), F32),
        grid=(t // tm,),
        in_specs=[pl.BlockSpec((tm, d), lambda i: (i, 0)),
                  pl.BlockSpec((d, d), lambda i: (0, 0), pipeline_mode=pl.Buffered(1)),
                  pl.BlockSpec((tm, d), lambda i: (i, 0)),
                  pl.BlockSpec((1, 1, d), lambda i: (i // nb, 0, 2)),
                  pl.BlockSpec((1, d), lambda i: (0, 0))],
        out_specs=pl.BlockSpec((tm, d), lambda i: (i, 0)),
        compiler_params=_params("parallel"),
    )(o2, w, x2, mod3, g.reshape(1, d))


def _flash_step_t(s_t, s_max, v_t, m_ref, l_ref, acc_ref, idx, cols):
    m_prev = m_ref[idx, :, cols]
    m_new = jnp.maximum(m_prev, s_max)
    alpha = jnp.exp2(m_prev - m_new)
    p_t = jnp.exp2(s_t - m_new)
    if l_ref is not None:
        l_ref[idx, :, cols] = alpha * l_ref[idx, :, cols] + jnp.sum(p_t, axis=0, keepdims=True)
    acc_ref[idx, :, cols] = (alpha * acc_ref[idx, :, cols]
                             + jnp.dot(v_t, p_t.astype(BF16), preferred_element_type=F32))
    m_ref[idx, :, cols] = m_new


def _flash_init(m_ref, l_ref, acc_ref):
    m_ref[...] = jnp.full(m_ref.shape, NEG_INF, F32)
    if l_ref is not None:
        l_ref[...] = jnp.zeros(l_ref.shape, F32)
    acc_ref[...] = jnp.zeros(acc_ref.shape, F32)


ONES_ROWS = 16


ATTN_TK = 512


def _mask_diagonal(s_ref, mx_ref, c0, granule):
    t = LANES
    kg = lax.broadcasted_iota(jnp.int32, (t, t), 0) // granule
    qg = lax.broadcasted_iota(jnp.int32, (t, t), 1) // granule
    visible = kg <= qg
    for r in range(ATTN_TK // t):
        rows = slice(r * t, (r + 1) * t)
        if r:
            s_ref[rows, c0:c0 + r * t] = jnp.full((t, r * t), NEG_INF, F32)
        diag = slice(c0 + r * t, c0 + (r + 1) * t)
        s_ref[rows, diag] = jnp.where(visible, s_ref[rows, diag], NEG_INF)
    cols = slice(c0, c0 + ATTN_TK)
    mx_ref[:, cols] = jnp.max(s_ref[:, cols], axis=0, keepdims=True)


def _causal_blocks(nq, begin, scores, mask, update, finish, buf0, buf1):
    every, late = slice(None), slice(ATTN_TK, 2 * ATTN_TK)
    begin(0)
    scores(0, buf0, every)

    def pairs(j, n):
        for _ in range(n):
            scores(j + 1, buf1, every)
            update(j, buf0, every)
            scores(j + 2, buf0, every)
            update(j + 1, buf1, every)
            j = j + 2

    def qblock(qi, carry):
        def quad(jj, c):
            pairs(4 * jj, 2)
            return c

        lax.fori_loop(0, qi // 2, quad, 0)

        @pl.when(qi % 2 == 1)
        def _():
            pairs(2 * qi - 2, 1)

        scores(2 * qi + 1, buf1, late)
        mask(buf0, 0)
        update(2 * qi, buf0, every)
        mask(buf1, ATTN_TK)
        update(2 * qi + 1, buf1, late)
        finish(qi)
        begin(jnp.minimum(qi + 1, nq - 1))
        scores(0, buf0, every)
        return carry

    lax.fori_loop(0, nq, qblock, 0)


def _diff_attn_kernel(q_ref, k_ref, vt_ref, z_ref, lq1_ref, lk1_ref, lq2_ref, lk2_ref, g_ref,
                      o_ref, m_ref, l_ref, acc_ref, s0_ref, s1_ref, mx0_ref, mx1_ref, qc_ref,
                      *, nq, lambda_init):
    tq = 2 * ATTN_TK
    lam = (jnp.exp(jnp.sum(lq1_ref[...] * lk1_ref[...], axis=-1, keepdims=True))
           - jnp.exp(jnp.sum(lq2_ref[...] * lk2_ref[...], axis=-1, keepdims=True)) + lambda_init)
    g = g_ref[...] * (1.0 - lambda_init)

    def begin(qi):
        qc_ref[...] = q_ref[pl.ds(pl.multiple_of(qi * tq, tq), tq), :].T
        _flash_init(m_ref, l_ref, acc_ref)

    def finish(qi):
        rows = pl.ds(pl.multiple_of(qi * tq, tq), tq)
        o_t = acc_ref[0] * (1.0 / l_ref[0]) - acc_ref[1] * (lam / l_ref[1])
        o = o_t.T
        on = o * lax.rsqrt(jnp.mean(o * o, axis=-1, keepdims=True) + EPS) * g
        o_ref[rows, :] = (on * _silu(z_ref[rows, :].astype(F32))).astype(o_ref.dtype)

    def scores(ki, buf, cols):
        s_ref, mx_ref = buf
        k = k_ref[pl.ds(pl.multiple_of(ki * ATTN_TK, ATTN_TK), ATTN_TK), :]
        for m in range(2):
            sl = slice(m * DIFF_QK_DIM, (m + 1) * DIFF_QK_DIM)
            s_t = jnp.dot(k[:, sl], qc_ref[sl, cols], preferred_element_type=F32)
            s_ref[m, :, cols] = s_t
            mx_ref[m, :, cols] = jnp.max(s_t, axis=0, keepdims=True)

    def mask(buf, c0):
        s_ref, mx_ref = buf
        for m in range(2):
            _mask_diagonal(s_ref.at[m], mx_ref.at[m], c0, CHUNK)

    def update(ki, buf, cols):
        s_ref, mx_ref = buf
        v_t = vt_ref[:, pl.ds(pl.multiple_of(ki * ATTN_TK, ATTN_TK), ATTN_TK)]
        for m in range(2):
            _flash_step_t(s_ref[m, :, cols], mx_ref[m, :, cols], v_t, m_ref, l_ref, acc_ref, m, cols)

    _causal_blocks(nq, begin, scores, mask, update, finish, (s0_ref, mx0_ref), (s1_ref, mx1_ref))


def _diff_attention(qkz, v_t, batch, seq, lq1, lk1, lq2, lk2, subln_g, lambda_init):
    t = qkz.shape[0]
    tq = 2 * ATTN_TK
    nq = seq // tq
    h, dv = DIFF_HEADS, DIFF_V_DIM
    vec = lambda a: a.reshape(1, -1).astype(F32)
    small = lambda n: pl.BlockSpec((1, n), lambda b, hh: (0, 0))
    once = pl.Buffered(1)
    return pl.pallas_call(
        functools.partial(_diff_attn_kernel, nq=nq, lambda_init=lambda_init),
        name="diff_attn",
        out_shape=jax.ShapeDtypeStruct((t, h * dv), BF16),
        grid=(batch, h),
        in_specs=[pl.BlockSpec((seq, dv), lambda b, hh: (b, hh)),
                  pl.BlockSpec((seq, dv), lambda b, hh: (b, h + hh), pipeline_mode=once),
                  pl.BlockSpec((dv, seq), lambda b, hh: (hh, b), pipeline_mode=once),
                  pl.BlockSpec((seq, dv), lambda b, hh: (b, 2 * h + hh)),
                  small(DIFF_QK_DIM), small(DIFF_QK_DIM), small(DIFF_QK_DIM), small(DIFF_QK_DIM),
                  small(dv)],
        out_specs=pl.BlockSpec((seq, dv), lambda b, hh: (b, hh)),
        scratch_shapes=[pltpu.VMEM((2, 1, tq), F32), pltpu.VMEM((2, 1, tq), F32),
                        pltpu.VMEM((2, dv, tq), F32),
                        pltpu.VMEM((2, ATTN_TK, tq), F32), pltpu.VMEM((2, ATTN_TK, tq), F32),
                        pltpu.VMEM((2, 1, tq), F32), pltpu.VMEM((2, 1, tq), F32),
                        pltpu.VMEM((dv, tq), BF16)],
        compiler_params=_params("parallel", "parallel"),
    )(qkz, qkz, v_t, qkz, vec(lq1), vec(lk1), vec(lq2), vec(lk2), vec(subln_g))


FOX_PIECES = 3


def _fox_gate_kernel(f_ref, b_ref, o_ref):
    x = f_ref[...] + b_ref[...]
    lf = jnp.minimum(x, 0.0) - jnp.log1p(jnp.exp(-jnp.abs(x)))
    n = lf.shape[0]
    row = lax.broadcasted_iota(jnp.int32, lf.shape, 0)
    d = 1
    while d < n:
        lf = lf + jnp.where(row >= d, pltpu.roll(lf, d, 0), 0.0)
        d *= 2
    rest = lf * LOG2E
    lane = lax.broadcasted_iota(jnp.int32, lf.shape, 1)
    out = jnp.zeros(lf.shape, F32)
    for p in range(FOX_PIECES):
        piece = rest.astype(BF16).astype(F32)
        rest = rest - piece
        moved = piece if p == 0 else pltpu.roll(piece, p * FOX_HEADS, 1)
        out = jnp.where((lane >= p * FOX_HEADS) & (lane < (p + 1) * FOX_HEADS), moved, out)
    o_ref[...] = out.astype(BF16)


def _fox_gates(f_logit, bias, batch, seq):
    bpad = jnp.zeros((1, LANES), F32).at[0, :FOX_HEADS].set(bias.astype(F32))
    return pl.pallas_call(
        _fox_gate_kernel,
        name="fox_gates",
        out_shape=jax.ShapeDtypeStruct((batch * seq, LANES), BF16),
        grid=(batch,),
        in_specs=[pl.BlockSpec((seq, LANES), lambda b: (b, 0)),
                  pl.BlockSpec((1, LANES), lambda b: (0, 0))],
        out_specs=pl.BlockSpec((seq, LANES), lambda b: (b, 0)),
        compiler_params=_params("parallel"),
    )(f_logit, bpad)


def _fox_place_kernel(f_ref, pk_ref, pq_ref, ck_ref, cq_ref, kx_ref, qx_ref):
    f = f_ref[...]
    kx_ref[...] = (jnp.dot(f, pk_ref[...], preferred_element_type=F32) + ck_ref[...]).astype(BF16)
    qx_ref[...] = (jnp.dot(f, pq_ref[...], preferred_element_type=F32) + cq_ref[...]).astype(BF16)


def _fox_extend(pieces, tm=1024):
    t = pieces.shape[0]
    tm = min(tm, t)
    n = FOX_HEADS * LANES
    src = jnp.arange(LANES)[:, None]
    dst = jnp.arange(n)[None, :]
    p_src, h_src = src // FOX_HEADS, src % FOX_HEADS
    h_dst, c_dst = dst // LANES, dst % LANES
    live = (p_src < FOX_PIECES) & (h_src == h_dst)
    pk = jnp.where(live & (c_dst == p_src), -1.0, 0.0).astype(BF16)
    pq = jnp.where(live & (c_dst == p_src + FOX_PIECES), 1.0, 0.0).astype(BF16)
    ck = ((c_dst >= FOX_PIECES) & (c_dst < 2 * FOX_PIECES)).astype(F32)
    cq = (c_dst < FOX_PIECES).astype(F32)
    mat = pl.BlockSpec((LANES, n), lambda i: (0, 0))
    vec = pl.BlockSpec((1, n), lambda i: (0, 0))
    return pl.pallas_call(
        _fox_place_kernel,
        name="fox_extend",
        out_shape=[jax.ShapeDtypeStruct((t, n), BF16), jax.ShapeDtypeStruct((t, n), BF16)],
        grid=(t // tm,),
        in_specs=[pl.BlockSpec((tm, LANES), lambda i: (i, 0)), mat, mat, vec, vec],
        out_specs=[pl.BlockSpec((tm, n), lambda i: (i, 0)), pl.BlockSpec((tm, n), lambda i: (i, 0))],
        compiler_params=_params("parallel"),
    )(pieces, pk, pq, ck, cq)


def _fox_attn_kernel(q_ref, qx_ref, k_ref, kx_ref, vt_ref, z_ref, o_ref, m_ref, acc_ref,
                     s0_ref, s1_ref, mx0_ref, mx1_ref, qt_ref, vte_ref, *, nq):
    tq = 2 * ATTN_TK
    dh = FOX_HEAD_DIM
    vte_ref[:dh, :] = vt_ref[...]
    vte_ref[dh:, :] = jnp.ones((ONES_ROWS, vte_ref.shape[1]), BF16)

    def begin(qi):
        rows = pl.ds(pl.multiple_of(qi * tq, tq), tq)
        qt_ref[:dh, :] = q_ref[rows, :].T
        qt_ref[dh:, :] = qx_ref[rows, :].T
        _flash_init(m_ref, None, acc_ref)

    def finish(qi):
        rows = pl.ds(pl.multiple_of(qi * tq, tq), tq)
        o = (acc_ref[0, :dh, :] * (1.0 / acc_ref[0, dh:dh + 1, :])).T
        o_ref[rows, :] = (o * _silu(z_ref[rows, :].astype(F32))).astype(o_ref.dtype)

    def scores(ki, buf, cols):
        s_ref, mx_ref = buf
        off = pl.multiple_of(ki * ATTN_TK, ATTN_TK)
        k = jnp.concatenate([k_ref[pl.ds(off, ATTN_TK), :], kx_ref[pl.ds(off, ATTN_TK), :]], axis=1)
        s_t = jnp.dot(k, qt_ref[:, cols], preferred_element_type=F32)
        s_ref[:, cols] = s_t
        mx_ref[:, cols] = jnp.max(s_t, axis=0, keepdims=True)

    def mask(buf, c0):
        _mask_diagonal(buf[0], buf[1], c0, 1)

    def update(ki, buf, cols):
        s_ref, mx_ref = buf
        v_t = vte_ref[:, pl.ds(pl.multiple_of(ki * ATTN_TK, ATTN_TK), ATTN_TK)]
        _flash_step_t(s_ref[:, cols], mx_ref[:, cols], v_t, m_ref, None, acc_ref, 0, cols)

    _causal_blocks(nq, begin, scores, mask, update, finish, (s0_ref, mx0_ref), (s1_ref, mx1_ref))


def _fox_attention(qkz, v_t, kx, qx, batch, seq):
    t = qkz.shape[0]
    tq = 2 * ATTN_TK
    nq = seq // tq
    h, dh = FOX_HEADS, FOX_HEAD_DIM
    return pl.pallas_call(
        functools.partial(_fox_attn_kernel, nq=nq),
        name="fox_attn",
        out_shape=jax.ShapeDtypeStruct((t, h * dh), BF16),
        grid=(batch, h),
        in_specs=[pl.BlockSpec((seq, dh), lambda b, hh: (b, hh)),
                  pl.BlockSpec((seq, LANES), lambda b, hh: (b, hh)),
                  pl.BlockSpec((seq, dh), lambda b, hh: (b, h + hh)),
                  pl.BlockSpec((seq, LANES), lambda b, hh: (b, hh)),
                  pl.BlockSpec((dh, seq), lambda b, hh: (hh, b)),
                  pl.BlockSpec((seq, dh), lambda b, hh: (b, 2 * h + hh))],
        out_specs=pl.BlockSpec((seq, dh), lambda b, hh: (b, hh)),
        scratch_shapes=[pltpu.VMEM((1, 1, tq), F32),
                        pltpu.VMEM((1, dh + ONES_ROWS, tq), F32),
                        pltpu.VMEM((ATTN_TK, tq), F32), pltpu.VMEM((ATTN_TK, tq), F32),
                        pltpu.VMEM((1, tq), F32), pltpu.VMEM((1, tq), F32),
                        pltpu.VMEM((dh + LANES, tq), BF16),
                        pltpu.VMEM((dh + ONES_ROWS, seq), BF16)],
        compiler_params=_params("parallel", "parallel"),
    )(qkz, qx, qkz, kx, v_t, qkz)


BAND_TQ = 256
BAND_WIN = BAND_PAST + BAND_TQ
BAND_HPB = LANES // BAND_HEAD_DIM
BAND_REV = 1024


def _band_bias_kernel(row_ref, o_ref):
    x = jnp.broadcast_to(row_ref[0], (BAND_WIN, BAND_REV))
    x = pltpu.roll(x, BAND_REV - BAND_WIN, 1, stride=1, stride_axis=0)
    x = x[:, :BAND_TQ]
    kc = lax.broadcasted_iota(jnp.int32, x.shape, 0) // CHUNK
    qc = lax.broadcasted_iota(jnp.int32, x.shape, 1) // CHUNK
    band = (kc >= qc) & (kc <= qc + BAND_PAST // CHUNK)
    o_ref[0] = jnp.where(band, x * LOG2E, NEG_INF)


def _band_bias(rel_table):
    nh = rel_table.shape[0]
    t = rel_table.astype(F32)
    row = jnp.concatenate([t, jnp.broadcast_to(t[:, 2 * REL_CLIP:], (nh, BAND_REV - t.shape[1]))], axis=1)
    return pl.pallas_call(
        _band_bias_kernel,
        name="band_bias",
        out_shape=jax.ShapeDtypeStruct((nh, BAND_WIN, BAND_TQ), F32),
        grid=(nh,),
        in_specs=[pl.BlockSpec((1, 1, BAND_REV), lambda h: (h, 0, 0))],
        out_specs=pl.BlockSpec((1, BAND_WIN, BAND_TQ), lambda h: (h, 0, 0)),
        compiler_params=_params("parallel"),
    )(row.reshape(nh, 1, BAND_REV))


def _band_attn_kernel(q_ref, k_ref, vt_ref, z_ref, bias_ref, o_ref, kp_ref, vtp_ref, s0_ref, s1_ref,
                      mx0_ref, mx1_ref, *, nq):
    hd = BAND_HEAD_DIM
    for hh in range(BAND_HPB):
        kp_ref[hh, :BAND_PAST, :] = jnp.zeros((BAND_PAST, hd), BF16)
        kp_ref[hh, BAND_PAST:, :] = k_ref[:, hh * hd:(hh + 1) * hd]
        vtp_ref[hh, :hd, :BAND_PAST] = jnp.zeros((hd, BAND_PAST), BF16)
        vtp_ref[hh, :hd, BAND_PAST:] = vt_ref[hh * hd:(hh + 1) * hd, :]
        vtp_ref[hh, hd:, :] = jnp.ones((ONES_ROWS, vtp_ref.shape[2]), BF16)

    def scores(j, buf):
        s_ref, mx_ref = buf
        start = pl.multiple_of(j * BAND_TQ, BAND_TQ)
        q_t = q_ref[pl.ds(start, BAND_TQ), :].T
        for hh in range(BAND_HPB):
            s_t = jnp.dot(kp_ref[hh, pl.ds(start, BAND_WIN), :], q_t[hh * hd:(hh + 1) * hd, :],
                          preferred_element_type=F32) + bias_ref[hh]
            s_ref[hh] = s_t
            mx_ref[hh] = jnp.max(s_t, axis=0, keepdims=True)

    def update(j, buf, masked):
        s_ref, mx_ref = buf
        start = pl.multiple_of(j * BAND_TQ, BAND_TQ)
        outs = []
        for hh in range(BAND_HPB):
            s_t = s_ref[hh]
            s_max = mx_ref[hh]
            if masked:
                kpos = start - BAND_PAST + lax.broadcasted_iota(jnp.int32, (BAND_WIN, 1), 0)
                s_t = jnp.where(kpos >= 0, s_t, NEG_INF)
                s_max = jnp.max(s_t, axis=0, keepdims=True)
            p_t = jnp.exp2(s_t - s_max).astype(BF16)
            ol = jnp.dot(vtp_ref[hh, :, pl.ds(start, BAND_WIN)], p_t, preferred_element_type=F32)
            outs.append(ol[:hd] * (1.0 / ol[hd:hd + 1]))
        o = jnp.concatenate(outs, axis=0).T
        z = z_ref[pl.ds(start, BAND_TQ), :].astype(F32)
        o_ref[pl.ds(start, BAND_TQ), :] = (o * _silu(z)).astype(o_ref.dtype)

    buf0, buf1 = (s0_ref, mx0_ref), (s1_ref, mx1_ref)
    scores(0, buf0)
    scores(1, buf1)
    update(0, buf0, True)
    scores(2, buf0)
    update(1, buf1, True)

    def pairs(j, n):
        for _ in range(n):
            scores(j + 1, buf1)
            update(j, buf0, False)
            scores(j + 2, buf0)
            update(j + 1, buf1, False)
            j = j + 2

    def quad(jj, carry):
        pairs(2 + 4 * jj, 2)
        return carry

    n_pairs = (nq - 4) // 2
    lax.fori_loop(0, n_pairs // 2, quad, 0)
    pairs(2 + 4 * (n_pairs // 2), n_pairs % 2)
    scores(nq - 1, buf1)
    update(nq - 2, buf0, False)
    update(nq - 1, buf1, False)


def _band_attention(qkz, v_t, bias_t, batch, seq):
    t = qkz.shape[0]
    nq = seq // BAND_TQ
    assert nq >= 4 and nq % 2 == 0
    hp = BAND_HEADS // BAND_HPB
    return pl.pallas_call(
        functools.partial(_band_attn_kernel, nq=nq),
        name="band_attn",
        out_shape=jax.ShapeDtypeStruct((t, BAND_HEADS * BAND_HEAD_DIM), BF16),
        grid=(batch, hp),
        in_specs=[pl.BlockSpec((seq, LANES), lambda b, g: (b, g)),
                  pl.BlockSpec((seq, LANES), lambda b, g: (b, hp + g)),
                  pl.BlockSpec((LANES, seq), lambda b, g: (g, b)),
                  pl.BlockSpec((seq, LANES), lambda b, g: (b, 2 * hp + g)),
                  pl.BlockSpec((BAND_HPB, BAND_WIN, BAND_TQ), lambda b, g: (g, 0, 0))],
        out_specs=pl.BlockSpec((seq, LANES), lambda b, g: (b, g)),
        scratch_shapes=[pltpu.VMEM((BAND_HPB, seq + BAND_PAST, BAND_HEAD_DIM), BF16),
                        pltpu.VMEM((BAND_HPB, BAND_HEAD_DIM + ONES_ROWS, seq + BAND_PAST), BF16),
                        pltpu.VMEM((BAND_HPB, BAND_WIN, BAND_TQ), F32),
                        pltpu.VMEM((BAND_HPB, BAND_WIN, BAND_TQ), F32),
                        pltpu.VMEM((BAND_HPB, 1, BAND_TQ), F32),
                        pltpu.VMEM((BAND_HPB, 1, BAND_TQ), F32)],
        compiler_params=_params("parallel", "parallel"),
    )(qkz, qkz, v_t, qkz, bias_t)


def _pool_kernel(u_ref, up_ref, z_ref, w_ref, ps_ref, o_ref, *, tm, nb):
    i = pl.program_id(0)
    t0 = (i % nb) * tm
    has_history = t0 > 0
    t = t0 + lax.broadcasted_iota(jnp.int32, (tm, 1), 0)
    for g, win in enumerate(POOL_WINDOWS):
        sl = slice(g * POOL_GROUP, (g + 1) * POOL_GROUP)
        u = u_ref[:, sl]
        acc = jnp.concatenate([jnp.where(has_history, up_ref[:, sl], 0.0), u], axis=0)
        d = 1
        while d < win:
            acc = acc + pltpu.roll(acc, d, 0)
            d *= 2
        wsum = acc[POOL_HALO:, :]
        inv_cnt = 1.0 / jnp.minimum(t + 1, win).astype(F32)
        delta = (wsum * inv_cnt - u).astype(BF16)
        y = jnp.dot(delta, w_ref[g], preferred_element_type=F32) * ps_ref[:, sl]
        o_ref[:, sl] = (y * _silu(z_ref[:, sl])).astype(o_ref.dtype)


def _pool_mix(uz, seq, pool_w, pool_scale, tm=256):
    t = uz.shape[0]
    d = D_MODEL
    tm = min(tm, seq)
    nb = seq // tm
    hb = tm // POOL_HALO
    return pl.pallas_call(
        functools.partial(_pool_kernel, tm=tm, nb=nb),
        name="pool_mix",
        out_shape=jax.ShapeDtypeStruct((t, d), BF16),
        grid=(t // tm,),
        in_specs=[pl.BlockSpec((tm, d), lambda i: (i, 0)),
                  pl.BlockSpec((POOL_HALO, d), lambda i: (jnp.maximum(i * hb - 1, 0), 0)),
                  pl.BlockSpec((tm, d), lambda i: (i, 1)),
                  pl.BlockSpec((len(POOL_WINDOWS), POOL_GROUP, POOL_GROUP), lambda i: (0, 0, 0)),
                  pl.BlockSpec((1, d), lambda i: (0, 0))],
        out_specs=pl.BlockSpec((tm, d), lambda i: (i, 0)),
        compiler_params=_params("parallel"),
    )(uz, uz, uz, pool_w, pool_scale.reshape(1, d).astype(F32))


PROJ_TN = 1024


def _transpose_cast_kernel(x_ref, o_ref):
    o_ref[...] = x_ref[...].T.astype(o_ref.dtype)


def _bf16_weights(w_in, branch, tile=512):
    d = w_in.shape[0]
    first = 2 * branch // tile
    src = w_in
    if w_in.shape[1] % LANES:
        src, first = w_in[:, 2 * branch:3 * branch], 0
    w_vt = pl.pallas_call(
        _transpose_cast_kernel,
        name="transpose_cast",
        out_shape=jax.ShapeDtypeStruct((branch, d), BF16),
        grid=(d // tile, branch // tile),
        in_specs=[pl.BlockSpec((tile, tile), lambda i, j: (i, first + j))],
        out_specs=pl.BlockSpec((tile, tile), lambda i, j: (j, i)),
        compiler_params=_params("parallel", "parallel"),
    )(src)
    return w_in.astype(BF16), w_vt


def _qkz_scale(qk_dim, branch):
    return jnp.concatenate([jnp.full((branch,), qk_dim ** -0.5 * LOG2E, F32), jnp.ones((2 * branch,), F32)])


def kernel(x, c, ada_w, ada_b, norm_pre, norm_post, diff_w_in, diff_w_out, diff_lambda_q1, diff_lambda_k1, diff_lambda_q2, diff_lambda_k2, diff_subln, band_w_in, band_w_out, band_rel_bias, fox_w_in, fox_w_out, fox_forget_bias, pool_w_in, pool_w_out, pool_group_w, pool_scale):
    batch, seq, d = x.shape
    depth = ada_w.shape[0]
    branch = d
    nblk = branch // PROJ_TN
    qkz_blocks = list(range(2 * nblk)) + list(range(3 * nblk, 4 * nblk))
    mod = _ada_mod(c, ada_w, ada_b)
    x2 = x.reshape(batch * seq, d)
    for i in range(depth):
        mod3 = mod[i].reshape(mod.shape[1], 1, 3 * d)
        kind = i % 4
        if kind == 0:
            lambda_init = 0.8 - 0.6 * math.exp(-0.3 * i)
            w, w_vt = _bf16_weights(diff_w_in, branch)
            qkz, h = _norm_proj(x2, seq, norm_pre[i], mod3, w, qkz_blocks,
                                _qkz_scale(DIFF_QK_DIM, branch), emit_h=True)
            v_t = _proj_t(w_vt, h)
            o = _diff_attention(qkz, v_t, batch, seq, diff_lambda_q1, diff_lambda_k1, diff_lambda_q2,
                                diff_lambda_k2, diff_subln, lambda_init)
            w_out = diff_w_out
        elif kind == 1:
            w, w_vt = _bf16_weights(band_w_in, branch)
            qkz, h = _norm_proj(x2, seq, norm_pre[i], mod3, w, qkz_blocks,
                                _qkz_scale(BAND_HEAD_DIM, branch), emit_h=True)
            v_t = _proj_t(w_vt, h)
            o = _band_attention(qkz, v_t, _band_bias(band_rel_bias), batch, seq)
            w_out = band_w_out
        elif kind == 2:
            wf = jnp.pad(fox_w_in[:, 4 * branch:], ((0, 0), (0, LANES - FOX_HEADS))).astype(BF16)
            w, w_vt = _bf16_weights(fox_w_in, branch)
            qkz, h, f_logit = _norm_proj(x2, seq, norm_pre[i], mod3, w, qkz_blocks,
                                         _qkz_scale(FOX_HEAD_DIM, branch), wf=wf, emit_h=True)
            v_t = _proj_t(w_vt, h)
            kx, qx = _fox_extend(_fox_gates(f_logit, fox_forget_bias, batch, seq))
            o = _fox_attention(qkz, v_t, kx, qx, batch, seq)
            w_out = fox_w_out
        else:
            uz, = _norm_proj(x2, seq, norm_pre[i], mod3, pool_w_in.astype(BF16), list(range(2 * nblk)),
                             jnp.ones((2 * branch,), F32), out_dtype=F32)
            o = _pool_mix(uz, seq, pool_group_w.astype(BF16), pool_scale)
            w_out = pool_w_out
        x2 = _out_proj_residual(o, w_out.astype(BF16), x2, seq, mod3, norm_post[i])
    return x2.reshape(batch, seq, d)
```

```python
import functools
import math

import jax
import jax.numpy as jnp
from jax import lax
from jax.experimental import pallas as pl
from jax.experimental.pallas import tpu as pltpu

F32 = jnp.float32
BF16 = jnp.bfloat16

D_MODEL = 2048
EPS = 1e-6
NEG_INF = -1e30
LOG2E = math.log2(math.e)
CHUNK = 64
DIFF_HEADS, DIFF_V_DIM, DIFF_QK_DIM = 8, 256, 128
BAND_HEADS, BAND_HEAD_DIM, BAND_PAST, REL_CLIP = 32, 64, 512, 256
FOX_HEADS, FOX_HEAD_DIM = 16, 128
POOL_WINDOWS = (2, 4, 8, 16)
POOL_GROUP = D_MODEL // len(POOL_WINDOWS)
POOL_HALO = 16

LANES = 128
VMEM_LIMIT = 52 * 1024 * 1024

NT_DIMS = (((1,), (1,)), ((), ()))


def _silu(x):
    return x / (1.0 + jnp.exp(-x))


def _params(*sem):
    return pltpu.CompilerParams(dimension_semantics=sem, vmem_limit_bytes=VMEM_LIMIT)


def _ada_kernel(c_ref, w_ref, b_ref, o_ref):
    ca = _silu(c_ref[...]).astype(BF16)
    o_ref[0] = jnp.dot(ca, w_ref[0].astype(BF16), preferred_element_type=F32) + b_ref[0]


def _ada_mod(c, ada_w, ada_b):
    b, d = c.shape
    bp = -(-b // 8) * 8
    depth, _, n3 = ada_w.shape
    tn = 768
    cp = jnp.pad(c, ((0, bp - b), (0, 0)))
    return pl.pallas_call(
        _ada_kernel,
        name="ada_mod",
        out_shape=jax.ShapeDtypeStruct((depth, bp, n3), F32),
        grid=(depth, n3 // tn),
        in_specs=[pl.BlockSpec((bp, d), lambda l, j: (0, 0)),
                  pl.BlockSpec((1, d, tn), lambda l, j: (l, 0, j)),
                  pl.BlockSpec((1, 1, tn), lambda l, j: (l, 0, j))],
        out_specs=pl.BlockSpec((1, bp, tn), lambda l, j: (l, 0, j)),
        compiler_params=_params("parallel", "parallel"),
    )(cp, ada_w, ada_b.reshape(depth, 1, n3))


def _norm_proj_kernel(x_ref, g_ref, sh_ref, sc_ref, w_ref, cs_ref, *rest, has_f, emit_h, sub):
    rest = list(rest)
    wf_ref = rest.pop(0) if has_f else None
    o_ref = rest.pop(0)
    hout_ref = rest.pop(0) if emit_h else None
    f_ref = rest.pop(0) if has_f else None
    h_ref = rest.pop(0)

    @pl.when(pl.program_id(1) == 0)
    def _():
        g = g_ref[...]
        sc = 1.0 + sc_ref[0]
        sh = sh_ref[0]
        for r in range(x_ref.shape[0] // sub):
            rows = slice(r * sub, (r + 1) * sub)
            x = x_ref[rows, :]
            y = x * lax.rsqrt(jnp.mean(x * x, axis=-1, keepdims=True) + EPS) * g
            h = (y * sc + sh).astype(BF16)
            h_ref[rows, :] = h
            if emit_h:
                hout_ref[rows, :] = h
            if has_f:
                f_ref[rows, :] = jnp.dot(h, wf_ref[...], preferred_element_type=F32)
            y = jnp.dot(h, w_ref[...], preferred_element_type=F32)
            o_ref[rows, :] = (y * cs_ref[...]).astype(o_ref.dtype)

    @pl.when(pl.program_id(1) > 0)
    def _():
        y = jnp.dot(h_ref[...], w_ref[...], preferred_element_type=F32)
        o_ref[...] = (y * cs_ref[...]).astype(o_ref.dtype)


def _norm_proj(x2, seq, g, mod3, w, col_blocks, col_scale, wf=None, emit_h=False, out_dtype=BF16,
               tm=1024, tn=1024):
    t, d = x2.shape
    tm = min(tm, seq)
    nb = seq // tm
    nj = len(col_blocks)
    n = nj * tn
    first, skip_from, skip = col_blocks[0], None, 0
    for a, b2 in zip(col_blocks, col_blocks[1:]):
        if b2 != a + 1:
            skip_from, skip = a - first + 1, b2 - a - 1
    if skip_from is None:
        wmap = lambda i, j: (0, j + first)
    else:
        wmap = lambda i, j: (0, j + first + jnp.where(j >= skip_from, skip, 0))
    in_specs = [pl.BlockSpec((tm, d), lambda i, j: (i, 0)),
                pl.BlockSpec((1, d), lambda i, j: (0, 0)),
                pl.BlockSpec((1, 1, d), lambda i, j: (i // nb, 0, 0)),
                pl.BlockSpec((1, 1, d), lambda i, j: (i // nb, 0, 1)),
                pl.BlockSpec((d, tn), wmap),
                pl.BlockSpec((1, tn), lambda i, j: (0, j))]
    out_shape = [jax.ShapeDtypeStruct((t, n), out_dtype)]
    out_specs = [pl.BlockSpec((tm, tn), lambda i, j: (i, j))]
    args = [x2, g.reshape(1, d), mod3, mod3, w, col_scale.reshape(1, n).astype(F32)]
    if wf is not None:
        in_specs.append(pl.BlockSpec((d, LANES), lambda i, j: (0, 0)))
        args.append(wf)
    if emit_h:
        out_shape.append(jax.ShapeDtypeStruct((t, d), BF16))
        out_specs.append(pl.BlockSpec((tm, d), lambda i, j: (i, 0)))
    if wf is not None:
        out_shape.append(jax.ShapeDtypeStruct((t, LANES), F32))
        out_specs.append(pl.BlockSpec((tm, LANES), lambda i, j: (i, 0)))
    return pl.pallas_call(
        functools.partial(_norm_proj_kernel, has_f=wf is not None, emit_h=emit_h, sub=min(256, tm)),
        name="norm_proj",
        out_shape=out_shape,
        grid=(t // tm, nj),
        in_specs=in_specs,
        out_specs=out_specs,
        scratch_shapes=[pltpu.VMEM((tm, d), BF16)],
        compiler_params=_params("parallel", "arbitrary"),
    )(*args)


def _proj_t_kernel(w_ref, h_ref, o_ref):
    o_ref[...] = lax.dot_general(w_ref[...], h_ref[...], NT_DIMS,
                                 preferred_element_type=F32).astype(o_ref.dtype)


def _proj_t(w_t, h, tn=1024, tm=2048):
    n, d = w_t.shape
    t = h.shape[0]
    tm = min(tm, t)
    return pl.pallas_call(
        _proj_t_kernel,
        name="proj_t",
        out_shape=jax.ShapeDtypeStruct((n, t), BF16),
        grid=(t // tm, n // tn),
        in_specs=[pl.BlockSpec((tn, d), lambda i, j: (j, 0)),
                  pl.BlockSpec((tm, d), lambda i, j: (i, 0))],
        out_specs=pl.BlockSpec((tn, tm), lambda i, j: (j, i)),
        compiler_params=_params("parallel", "arbitrary"),
    )(w_t, h)


def _out_proj_kernel(o_ref, w_ref, x_ref, gate_ref, g_ref, out_ref, *, sub):
    blocks = [slice(r * sub, (r + 1) * sub) for r in range(o_ref.shape[0] // sub)]
    ys = [jnp.dot(o_ref[rows, :], w_ref[...], preferred_element_type=F32) for rows in blocks]
    for rows, y in zip(blocks, ys):
        yn = y * lax.rsqrt(jnp.mean(y * y, axis=-1, keepdims=True) + EPS) * g_ref[...]
        out_ref[rows, :] = x_ref[rows, :] + gate_ref[0] * yn


def _out_proj_residual(o2, w, x2, seq, mod3, g, tm=512):
    t, d = x2.shape
    tm = min(tm, seq)
    nb = seq // tm
    return pl.pallas_call(
        functools.partial(_out_proj_kernel, sub=min(256, tm)),
        name="out_proj",
        out_shape=jax.ShapeDtypeStruct((t, d), F32),
        grid=(t // tm,),
        in_specs=[pl.BlockSpec((tm, d), lambda i: (i, 0)),
                  pl.BlockSpec((d, d), lambda i: (0, 0), pipeline_mode=pl.Buffered(1)),
                  pl.BlockSpec((tm, d), lambda i: (i, 0)),
                  pl.BlockSpec((1, 1, d), lambda i: (i // nb, 0, 2)),
                  pl.BlockSpec((1, d), lambda i: (0, 0))],
        out_specs=pl.BlockSpec((tm, d), lambda i: (i, 0)),
        compiler_params=_params("parallel"),
    )(o2, w, x2, mod3, g.reshape(1, d))


def _flash_step_t(s_t, s_max, v_t, m_ref, l_ref, acc_ref, idx, cols):
    m_prev = m_ref[idx, :, cols]
    m_new = jnp.maximum(m_prev, s_max)
    alpha = jnp.exp2(m_prev - m_new)
    p_t = jnp.exp2(s_t - m_new)
    if l_ref is not None:
        l_ref[idx, :, cols] = alpha * l_ref[idx, :, cols] + jnp.sum(p_t, axis=0, keepdims=True)
    acc_ref[idx, :, cols] = (alpha * acc_ref[idx, :, cols]
                             + jnp.dot(v_t, p_t.astype(BF16), preferred_element_type=F32))
    m_ref[idx, :, cols] = m_new


def _flash_init(m_ref, l_ref, acc_ref):
    m_ref[...] = jnp.full(m_ref.shape, NEG_INF, F32)
    if l_ref is not None:
        l_ref[...] = jnp.zeros(l_ref.shape, F32)
    acc_ref[...] = jnp.zeros(acc_ref.shape, F32)


ONES_ROWS = 16


ATTN_TK = 512


def _mask_diagonal(s_ref, mx_ref, c0, granule):
    t = LANES
    kg = lax.broadcasted_iota(jnp.int32, (t, t), 0) // granule
    qg = lax.broadcasted_iota(jnp.int32, (t, t), 1) // granule
    visible = kg <= qg
    for r in range(ATTN_TK // t):
        rows = slice(r * t, (r + 1) * t)
        if r:
            s_ref[rows, c0:c0 + r * t] = jnp.full((t, r * t), NEG_INF, F32)
        diag = slice(c0 + r * t, c0 + (r + 1) * t)
        s_ref[rows, diag] = jnp.where(visible, s_ref[rows, diag], NEG_INF)
    cols = slice(c0, c0 + ATTN_TK)
    mx_ref[:, cols] = jnp.max(s_ref[:, cols], axis=0, keepdims=True)


def _causal_blocks(nq, begin, scores, mask, update, finish, buf0, buf1):
    every, late = slice(None), slice(ATTN_TK, 2 * ATTN_TK)
    begin(0)
    scores(0, buf0, every)

    def pairs(j, n):
        for _ in range(n):
            scores(j + 1, buf1, every)
            update(j, buf0, every)
            scores(j + 2, buf0, every)
            update(j + 1, buf1, every)
            j = j + 2

    def qblock(qi, carry):
        def quad(jj, c):
            pairs(4 * jj, 2)
            return c

        lax.fori_loop(0, qi // 2, quad, 0)

        @pl.when(qi % 2 == 1)
        def _():
            pairs(2 * qi - 2, 1)

        scores(2 * qi + 1, buf1, late)
        mask(buf0, 0)
        update(2 * qi, buf0, every)
        mask(buf1, ATTN_TK)
        update(2 * qi + 1, buf1, late)
        finish(qi)
        begin(jnp.minimum(qi + 1, nq - 1))
        scores(0, buf0, every)
        return carry

    lax.fori_loop(0, nq, qblock, 0)


def _diff_attn_kernel(q_ref, k_ref, vt_ref, z_ref, lq1_ref, lk1_ref, lq2_ref, lk2_ref, g_ref,
                      o_ref, m_ref, l_ref, acc_ref, s0_ref, s1_ref, mx0_ref, mx1_ref, qc_ref,
                      *, nq, lambda_init):
    tq = 2 * ATTN_TK
    lam = (jnp.exp(jnp.sum(lq1_ref[...] * lk1_ref[...], axis=-1, keepdims=True))
           - jnp.exp(jnp.sum(lq2_ref[...] * lk2_ref[...], axis=-1, keepdims=True)) + lambda_init)
    g = g_ref[...] * (1.0 - lambda_init)

    def begin(qi):
        qc_ref[...] = q_ref[pl.ds(pl.multiple_of(qi * tq, tq), tq), :].T
        _flash_init(m_ref, l_ref, acc_ref)

    def finish(qi):
        rows = pl.ds(pl.multiple_of(qi * tq, tq), tq)
        o_t = acc_ref[0] * (1.0 / l_ref[0]) - acc_ref[1] * (lam / l_ref[1])
        o = o_t.T
        on = o * lax.rsqrt(jnp.mean(o * o, axis=-1, keepdims=True) + EPS) * g
        o_ref[rows, :] = (on * _silu(z_ref[rows, :].astype(F32))).astype(o_ref.dtype)

    def scores(ki, buf, cols):
        s_ref, mx_ref = buf
        k = k_ref[pl.ds(pl.multiple_of(ki * ATTN_TK, ATTN_TK), ATTN_TK), :]
        for m in range(2):
            sl = slice(m * DIFF_QK_DIM, (m + 1) * DIFF_QK_DIM)
            s_t = jnp.dot(k[:, sl], qc_ref[sl, cols], preferred_element_type=F32)
            s_ref[m, :, cols] = s_t
            mx_ref[m, :, cols] = jnp.max(s_t, axis=0, keepdims=True)

    def mask(buf, c0):
        s_ref, mx_ref = buf
        for m in range(2):
            _mask_diagonal(s_ref.at[m], mx_ref.at[m], c0, CHUNK)

    def update(ki, buf, cols):
        s_ref, mx_ref = buf
        v_t = vt_ref[:, pl.ds(pl.multiple_of(ki * ATTN_TK, ATTN_TK), ATTN_TK)]
        for m in range(2):
            _flash_step_t(s_ref[m, :, cols], mx_ref[m, :, cols], v_t, m_ref, l_ref, acc_ref, m, cols)

    _causal_blocks(nq, begin, scores, mask, update, finish, (s0_ref, mx0_ref), (s1_ref, mx1_ref))


def _diff_attention(qkz, v_t, batch, seq, lq1, lk1, lq2, lk2, subln_g, lambda_init):
    t = qkz.shape[0]
    tq = 2 * ATTN_TK
    nq = seq // tq
    h, dv = DIFF_HEADS, DIFF_V_DIM
    vec = lambda a: a.reshape(1, -1).astype(F32)
    small = lambda n: pl.BlockSpec((1, n), lambda b, hh: (0, 0))
    once = pl.Buffered(1)
    return pl.pallas_call(
        functools.partial(_diff_attn_kernel, nq=nq, lambda_init=lambda_init),
        name="diff_attn",
        out_shape=jax.ShapeDtypeStruct((t, h * dv), BF16),
        grid=(batch, h),
        in_specs=[pl.BlockSpec((seq, dv), lambda b, hh: (b, hh)),
                  pl.BlockSpec((seq, dv), lambda b, hh: (b, h + hh), pipeline_mode=once),
                  pl.BlockSpec((dv, seq), lambda b, hh: (hh, b), pipeline_mode=once),
                  pl.BlockSpec((seq, dv), lambda b, hh: (b, 2 * h + hh)),
                  small(DIFF_QK_DIM), small(DIFF_QK_DIM), small(DIFF_QK_DIM), small(DIFF_QK_DIM),
                  small(dv)],
        out_specs=pl.BlockSpec((seq, dv), lambda b, hh: (b, hh)),
        scratch_shapes=[pltpu.VMEM((2, 1, tq), F32), pltpu.VMEM((2, 1, tq), F32),
                        pltpu.VMEM((2, dv, tq), F32),
                        pltpu.VMEM((2, ATTN_TK, tq), F32), pltpu.VMEM((2, ATTN_TK, tq), F32),
                        pltpu.VMEM((2, 1, tq), F32), pltpu.VMEM((2, 1, tq), F32),
                        pltpu.VMEM((dv, tq), BF16)],
        compiler_params=_params("parallel", "parallel"),
    )(qkz, qkz, v_t, qkz, vec(lq1), vec(lk1), vec(lq2), vec(lk2), vec(subln_g))


FOX_PIECES = 3


def _fox_gate_kernel(f_ref, b_ref, o_ref):
    x = f_ref[...] + b_ref[...]
    lf = jnp.minimum(x, 0.0) - jnp.log1p(jnp.exp(-jnp.abs(x)))
    n = lf.shape[0]
    row = lax.broadcasted_iota(jnp.int32, lf.shape, 0)
    d = 1
    while d < n:
        lf = lf + jnp.where(row >= d, pltpu.roll(lf, d, 0), 0.0)
        d *= 2
    rest = lf * LOG2E
    lane = lax.broadcasted_iota(jnp.int32, lf.shape, 1)
    out = jnp.zeros(lf.shape, F32)
    for p in range(FOX_PIECES):
        piece = rest.astype(BF16).astype(F32)
        rest = rest - piece
        moved = piece if p == 0 else pltpu.roll(piece, p * FOX_HEADS, 1)
        out = jnp.where((lane >= p * FOX_HEADS) & (lane < (p + 1) * FOX_HEADS), moved, out)
    o_ref[...] = out.astype(BF16)


def _fox_gates(f_logit, bias, batch, seq):
    bpad = jnp.zeros((1, LANES), F32).at[0, :FOX_HEADS].set(bias.astype(F32))
    return pl.pallas_call(
        _fox_gate_kernel,
        name="fox_gates",
        out_shape=jax.ShapeDtypeStruct((batch * seq, LANES), BF16),
        grid=(batch,),
        in_specs=[pl.BlockSpec((seq, LANES), lambda b: (b, 0)),
                  pl.BlockSpec((1, LANES), lambda b: (0, 0))],
        out_specs=pl.BlockSpec((seq, LANES), lambda b: (b, 0)),
        compiler_params=_params("parallel"),
    )(f_logit, bpad)


def _fox_place_kernel(f_ref, pk_ref, pq_ref, ck_ref, cq_ref, kx_ref, qx_ref):
    f = f_ref[...]
    kx_ref[...] = (jnp.dot(f, pk_ref[...], preferred_element_type=F32) + ck_ref[...]).astype(BF16)
    qx_ref[...] = (jnp.dot(f, pq_ref[...], preferred_element_type=F32) + cq_ref[...]).astype(BF16)


def _fox_extend(pieces, tm=1024):
    t = pieces.shape[0]
    tm = min(tm, t)
    n = FOX_HEADS * LANES
    src = jnp.arange(LANES)[:, None]
    dst = jnp.arange(n)[None, :]
    p_src, h_src = src // FOX_HEADS, src % FOX_HEADS
    h_dst, c_dst = dst // LANES, dst % LANES
    live = (p_src < FOX_PIECES) & (h_src == h_dst)
    pk = jnp.where(live & (c_dst == p_src), -1.0, 0.0).astype(BF16)
    pq = jnp.where(live & (c_dst == p_src + FOX_PIECES), 1.0, 0.0).astype(BF16)
    ck = ((c_dst >= FOX_PIECES) & (c_dst < 2 * FOX_PIECES)).astype(F32)
    cq = (c_dst < FOX_PIECES).astype(F32)
    mat = pl.BlockSpec((LANES, n), lambda i: (0, 0))
    vec = pl.BlockSpec((1, n), lambda i: (0, 0))
    return pl.pallas_call(
        _fox_place_kernel,
        name="fox_extend",
        out_shape=[jax.ShapeDtypeStruct((t, n), BF16), jax.ShapeDtypeStruct((t, n), BF16)],
        grid=(t // tm,),
        in_specs=[pl.BlockSpec((tm, LANES), lambda i: (i, 0)), mat, mat, vec, vec],
        out_specs=[pl.BlockSpec((tm, n), lambda i: (i, 0)), pl.BlockSpec((tm, n), lambda i: (i, 0))],
        compiler_params=_params("parallel"),
    )(pieces, pk, pq, ck, cq)


def _fox_attn_kernel(q_ref, qx_ref, k_ref, kx_ref, vt_ref, z_ref, o_ref, m_ref, acc_ref,
                     s0_ref, s1_ref, mx0_ref, mx1_ref, qt_ref, vte_ref, *, nq):
    tq = 2 * ATTN_TK
    dh = FOX_HEAD_DIM
    vte_ref[:dh, :] = vt_ref[...]
    vte_ref[dh:, :] = jnp.ones((ONES_ROWS, vte_ref.shape[1]), BF16)

    def begin(qi):
        rows = pl.ds(pl.multiple_of(qi * tq, tq), tq)
        qt_ref[:dh, :] = q_ref[rows, :].T
        qt_ref[dh:, :] = qx_ref[rows, :].T
        _flash_init(m_ref, None, acc_ref)

    def finish(qi):
        rows = pl.ds(pl.multiple_of(qi * tq, tq), tq)
        o = (acc_ref[0, :dh, :] * (1.0 / acc_ref[0, dh:dh + 1, :])).T
        o_ref[rows, :] = (o * _silu(z_ref[rows, :].astype(F32))).astype(o_ref.dtype)

    def scores(ki, buf, cols):
        s_ref, mx_ref = buf
        off = pl.multiple_of(ki * ATTN_TK, ATTN_TK)
        k = jnp.concatenate([k_ref[pl.ds(off, ATTN_TK), :], kx_ref[pl.ds(off, ATTN_TK), :]], axis=1)
        s_t = jnp.dot(k, qt_ref[:, cols], preferred_element_type=F32)
        s_ref[:, cols] = s_t
        mx_ref[:, cols] = jnp.max(s_t, axis=0, keepdims=True)

    def mask(buf, c0):
        _mask_diagonal(buf[0], buf[1], c0, 1)

    def update(ki, buf, cols):
        s_ref, mx_ref = buf
        v_t = vte_ref[:, pl.ds(pl.multiple_of(ki * ATTN_TK, ATTN_TK), ATTN_TK)]
        _flash_step_t(s_ref[:, cols], mx_ref[:, cols], v_t, m_ref, None, acc_ref, 0, cols)

    _causal_blocks(nq, begin, scores, mask, update, finish, (s0_ref, mx0_ref), (s1_ref, mx1_ref))


def _fox_attention(qkz, v_t, kx, qx, batch, seq):
    t = qkz.shape[0]
    tq = 2 * ATTN_TK
    nq = seq // tq
    h, dh = FOX_HEADS, FOX_HEAD_DIM
    return pl.pallas_call(
        functools.partial(_fox_attn_kernel, nq=nq),
        name="fox_attn",
        out_shape=jax.ShapeDtypeStruct((t, h * dh), BF16),
        grid=(batch, h),
        in_specs=[pl.BlockSpec((seq, dh), lambda b, hh: (b, hh)),
                  pl.BlockSpec((seq, LANES), lambda b, hh: (b, hh)),
                  pl.BlockSpec((seq, dh), lambda b, hh: (b, h + hh)),
                  pl.BlockSpec((seq, LANES), lambda b, hh: (b, hh)),
                  pl.BlockSpec((dh, seq), lambda b, hh: (hh, b)),
                  pl.BlockSpec((seq, dh), lambda b, hh: (b, 2 * h + hh))],
        out_specs=pl.BlockSpec((seq, dh), lambda b, hh: (b, hh)),
        scratch_shapes=[pltpu.VMEM((1, 1, tq), F32),
                        pltpu.VMEM((1, dh + ONES_ROWS, tq), F32),
                        pltpu.VMEM((ATTN_TK, tq), F32), pltpu.VMEM((ATTN_TK, tq), F32),
                        pltpu.VMEM((1, tq), F32), pltpu.VMEM((1, tq), F32),
                        pltpu.VMEM((dh + LANES, tq), BF16),
                        pltpu.VMEM((dh + ONES_ROWS, seq), BF16)],
        compiler_params=_params("parallel", "parallel"),
    )(qkz, qx, qkz, kx, v_t, qkz)


BAND_TQ = 256
BAND_WIN = BAND_PAST + BAND_TQ
BAND_HPB = LANES // BAND_HEAD_DIM
BAND_REV = 1024


def _band_bias_kernel(row_ref, o_ref):
    x = jnp.broadcast_to(row_ref[0], (BAND_WIN, BAND_REV))
    x = pltpu.roll(x, BAND_REV - BAND_WIN, 1, stride=1, stride_axis=0)
    x = x[:, :BAND_TQ]
    kc = lax.broadcasted_iota(jnp.int32, x.shape, 0) // CHUNK
    qc = lax.broadcasted_iota(jnp.int32, x.shape, 1) // CHUNK
    band = (kc >= qc) & (kc <= qc + BAND_PAST // CHUNK)
    o_ref[0] = jnp.where(band, x * LOG2E, NEG_INF)


def _band_bias(rel_table):
    nh = rel_table.shape[0]
    t = rel_table.astype(F32)
    row = jnp.concatenate([t, jnp.broadcast_to(t[:, 2 * REL_CLIP:], (nh, BAND_REV - t.shape[1]))], axis=1)
    return pl.pallas_call(
        _band_bias_kernel,
        name="band_bias",
        out_shape=jax.ShapeDtypeStruct((nh, BAND_WIN, BAND_TQ), F32),
        grid=(nh,),
        in_specs=[pl.BlockSpec((1, 1, BAND_REV), lambda h: (h, 0, 0))],
        out_specs=pl.BlockSpec((1, BAND_WIN, BAND_TQ), lambda h: (h, 0, 0)),
        compiler_params=_params("parallel"),
    )(row.reshape(nh, 1, BAND_REV))


def _band_attn_kernel(q_ref, k_ref, vt_ref, z_ref, bias_ref, o_ref, kp_ref, vtp_ref, s0_ref, s1_ref,
                      mx0_ref, mx1_ref, *, nq):
    hd = BAND_HEAD_DIM
    for hh in range(BAND_HPB):
        kp_ref[hh, :BAND_PAST, :] = jnp.zeros((BAND_PAST, hd), BF16)
        kp_ref[hh, BAND_PAST:, :] = k_ref[:, hh * hd:(hh + 1) * hd]
        vtp_ref[hh, :hd, :BAND_PAST] = jnp.zeros((hd, BAND_PAST), BF16)
        vtp_ref[hh, :hd, BAND_PAST:] = vt_ref[hh * hd:(hh + 1) * hd, :]
        vtp_ref[hh, hd:, :] = jnp.ones((ONES_ROWS, vtp_ref.shape[2]), BF16)

    def scores(j, buf):
        s_ref, mx_ref = buf
        start = pl.multiple_of(j * BAND_TQ, BAND_TQ)
        q_t = q_ref[pl.ds(start, BAND_TQ), :].T
        for hh in range(BAND_HPB):
            s_t = jnp.dot(kp_ref[hh, pl.ds(start, BAND_WIN), :], q_t[hh * hd:(hh + 1) * hd, :],
                          preferred_element_type=F32) + bias_ref[hh]
            s_ref[hh] = s_t
            mx_ref[hh] = jnp.max(s_t, axis=0, keepdims=True)

    def update(j, buf, masked):
        s_ref, mx_ref = buf
        start = pl.multiple_of(j * BAND_TQ, BAND_TQ)
        outs = []
        for hh in range(BAND_HPB):
            s_t = s_ref[hh]
            s_max = mx_ref[hh]
            if masked:
                kpos = start - BAND_PAST + lax.broadcasted_iota(jnp.int32, (BAND_WIN, 1), 0)
                s_t = jnp.where(kpos >= 0, s_t, NEG_INF)
                s_max = jnp.max(s_t, axis=0, keepdims=True)
            p_t = jnp.exp2(s_t - s_max).astype(BF16)
            ol = jnp.dot(vtp_ref[hh, :, pl.ds(start, BAND_WIN)], p_t, preferred_element_type=F32)
            outs.append(ol[:hd] * (1.0 / ol[hd:hd + 1]))
        o = jnp.concatenate(outs, axis=0).T
        z = z_ref[pl.ds(start, BAND_TQ), :].astype(F32)
        o_ref[pl.ds(start, BAND_TQ), :] = (o * _silu(z)).astype(o_ref.dtype)

    buf0, buf1 = (s0_ref, mx0_ref), (s1_ref, mx1_ref)
    scores(0, buf0)
    scores(1, buf1)
    update(0, buf0, True)
    scores(2, buf0)
    update(1, buf1, True)

    def pairs(j, n):
        for _ in range(n):
            scores(j + 1, buf1)
            update(j, buf0, False)
            scores(j + 2, buf0)
            update(j + 1, buf1, False)
            j = j + 2

    def quad(jj, carry):
        pairs(2 + 4 * jj, 2)
        return carry

    n_pairs = (nq - 4) // 2
    lax.fori_loop(0, n_pairs // 2, quad, 0)
    pairs(2 + 4 * (n_pairs // 2), n_pairs % 2)
    scores(nq - 1, buf1)
    update(nq - 2, buf0, False)
    update(nq - 1, buf1, False)


def _band_attention(qkz, v_t, bias_t, batch, seq):
    t = qkz.shape[0]
    nq = seq // BAND_TQ
    assert nq >= 4 and nq % 2 == 0
    hp = BAND_HEADS // BAND_HPB
    return pl.pallas_call(
        functools.partial(_band_attn_kernel, nq=nq),
        name="band_attn",
        out_shape=jax.ShapeDtypeStruct((t, BAND_HEADS * BAND_HEAD_DIM), BF16),
        grid=(batch, hp),
        in_specs=[pl.BlockSpec((seq, LANES), lambda b, g: (b, g)),
                  pl.BlockSpec((seq, LANES), lambda b, g: (b, hp + g)),
                  pl.BlockSpec((LANES, seq), lambda b, g: (g, b)),
                  pl.BlockSpec((seq, LANES), lambda b, g: (b, 2 * hp + g)),
                  pl.BlockSpec((BAND_HPB, BAND_WIN, BAND_TQ), lambda b, g: (g, 0, 0))],
        out_specs=pl.BlockSpec((seq, LANES), lambda b, g: (b, g)),
        scratch_shapes=[pltpu.VMEM((BAND_HPB, seq + BAND_PAST, BAND_HEAD_DIM), BF16),
                        pltpu.VMEM((BAND_HPB, BAND_HEAD_DIM + ONES_ROWS, seq + BAND_PAST), BF16),
                        pltpu.VMEM((BAND_HPB, BAND_WIN, BAND_TQ), F32),
                        pltpu.VMEM((BAND_HPB, BAND_WIN, BAND_TQ), F32),
                        pltpu.VMEM((BAND_HPB, 1, BAND_TQ), F32),
                        pltpu.VMEM((BAND_HPB, 1, BAND_TQ), F32)],
        compiler_params=_params("parallel", "parallel"),
    )(qkz, qkz, v_t, qkz, bias_t)


def _pool_kernel(u_ref, up_ref, z_ref, w_ref, ps_ref, o_ref, *, tm, nb):
    i = pl.program_id(0)
    t0 = (i % nb) * tm
    has_history = t0 > 0
    t = t0 + lax.broadcasted_iota(jnp.int32, (tm, 1), 0)
    for g, win in enumerate(POOL_WINDOWS):
        sl = slice(g * POOL_GROUP, (g + 1) * POOL_GROUP)
        u = u_ref[:, sl]
        acc = jnp.concatenate([jnp.where(has_history, up_ref[:, sl], 0.0), u], axis=0)
        d = 1
        while d < win:
            acc = acc + pltpu.roll(acc, d, 0)
            d *= 2
        wsum = acc[POOL_HALO:, :]
        inv_cnt = 1.0 / jnp.minimum(t + 1, win).astype(F32)
        delta = (wsum * inv_cnt - u).astype(BF16)
        y = jnp.dot(delta, w_ref[g], preferred_element_type=F32) * ps_ref[:, sl]
        o_ref[:, sl] = (y * _silu(z_ref[:, sl])).astype(o_ref.dtype)


def _pool_mix(uz, seq, pool_w, pool_scale, tm=512):
    t = uz.shape[0]
    d = D_MODEL
    tm = min(tm, seq)
    nb = seq // tm
    hb = tm // POOL_HALO
    return pl.pallas_call(
        functools.partial(_pool_kernel, tm=tm, nb=nb),
        name="pool_mix",
        out_shape=jax.ShapeDtypeStruct((t, d), BF16),
        grid=(t // tm,),
        in_specs=[pl.BlockSpec((tm, d), lambda i: (i, 0)),
                  pl.BlockSpec((POOL_HALO, d), lambda i: (jnp.maximum(i * hb - 1, 0), 0)),
                  pl.BlockSpec((tm, d), lambda i: (i, 1)),
                  pl.BlockSpec((len(POOL_WINDOWS), POOL_GROUP, POOL_GROUP), lambda i: (0, 0, 0)),
                  pl.BlockSpec((1, d), lambda i: (0, 0))],
        out_specs=pl.BlockSpec((tm, d), lambda i: (i, 0)),
        compiler_params=_params("parallel"),
    )(uz, uz, uz, pool_w, pool_scale.reshape(1, d).astype(F32))


PROJ_TN = 1024


def _transpose_cast_kernel(x_ref, o_ref):
    o_ref[...] = x_ref[...].T.astype(o_ref.dtype)


def _bf16_weights(w_in, branch, tile=512):
    d = w_in.shape[0]
    first = 2 * branch // tile
    w = w_in.astype(BF16)
    w_vt = pl.pallas_call(
        _transpose_cast_kernel,
        name="transpose_cast",
        out_shape=jax.ShapeDtypeStruct((branch, d), BF16),
        grid=(d // tile, branch // tile),
        in_specs=[pl.BlockSpec((tile, tile), lambda i, j: (i, first + j))],
        out_specs=pl.BlockSpec((tile, tile), lambda i, j: (j, i)),
        compiler_params=_params("parallel", "parallel"),
    )(w)
    return w, w_vt


def _qkz_scale(qk_dim, branch):
    return jnp.concatenate([jnp.full((branch,), qk_dim ** -0.5 * LOG2E, F32), jnp.ones((2 * branch,), F32)])


def kernel(x, c, ada_w, ada_b, norm_pre, norm_post, diff_w_in, diff_w_out, diff_lambda_q1, diff_lambda_k1, diff_lambda_q2, diff_lambda_k2, diff_subln, band_w_in, band_w_out, band_rel_bias, fox_w_in, fox_w_out, fox_forget_bias, pool_w_in, pool_w_out, pool_group_w, pool_scale):
    batch, seq, d = x.shape
    depth = ada_w.shape[0]
    branch = d
    nblk = branch // PROJ_TN
    qkz_blocks = list(range(2 * nblk)) + list(range(3 * nblk, 4 * nblk))
    mod = _ada_mod(c, ada_w, ada_b)
    x2 = x.reshape(batch * seq, d)
    for i in range(depth):
        mod3 = mod[i].reshape(mod.shape[1], 1, 3 * d)
        kind = i % 4
        if kind == 0:
            lambda_init = 0.8 - 0.6 * math.exp(-0.3 * i)
            w, w_vt = _bf16_weights(diff_w_in, branch)
            qkz, h = _norm_proj(x2, seq, norm_pre[i], mod3, w, qkz_blocks,
                                _qkz_scale(DIFF_QK_DIM, branch), emit_h=True)
            v_t = _proj_t(w_vt, h)
            o = _diff_attention(qkz, v_t, batch, seq, diff_lambda_q1, diff_lambda_k1, diff_lambda_q2,
                                diff_lambda_k2, diff_subln, lambda_init)
            w_out = diff_w_out
        elif kind == 1:
            w, w_vt = _bf16_weights(band_w_in, branch)
            qkz, h = _norm_proj(x2, seq, norm_pre[i], mod3, w, qkz_blocks,
                                _qkz_scale(BAND_HEAD_DIM, branch), emit_h=True)
            v_t = _proj_t(w_vt, h)
            o = _band_attention(qkz, v_t, _band_bias(band_rel_bias), batch, seq)
            w_out = band_w_out
        elif kind == 2:
            w, w_vt = _bf16_weights(fox_w_in, branch)
            wf = jnp.pad(w[:, 4 * branch:], ((0, 0), (0, LANES - FOX_HEADS)))
            qkz, h, f_logit = _norm_proj(x2, seq, norm_pre[i], mod3, w, qkz_blocks,
                                         _qkz_scale(FOX_HEAD_DIM, branch), wf=wf, emit_h=True)
            v_t = _proj_t(w_vt, h)
            kx, qx = _fox_extend(_fox_gates(f_logit, fox_forget_bias, batch, seq))
            o = _fox_attention(qkz, v_t, kx, qx, batch, seq)
            w_out = fox_w_out
        else:
            uz, = _norm_proj(x2, seq, norm_pre[i], mod3, pool_w_in.astype(BF16), list(range(2 * nblk)),
                             jnp.ones((2 * branch,), F32), out_dtype=F32)
            o = _pool_mix(uz, seq, pool_group_w.astype(BF16), pool_scale)
            w_out = pool_w_out
        x2 = _out_proj_residual(o, w_out.astype(BF16), x2, seq, mod3, norm_post[i])
    return x2.reshape(batch, seq, d)
```

```python
import functools
import math

import jax
import jax.numpy as jnp
from jax import lax
from jax.experimental import pallas as pl
from jax.experimental.pallas import tpu as pltpu

F32 = jnp.float32
BF16 = jnp.bfloat16

D_MODEL = 2048
EPS = 1e-6
NEG_INF = -1e30
LOG2E = math.log2(math.e)
CHUNK = 64
DIFF_HEADS, DIFF_V_DIM, DIFF_QK_DIM = 8, 256, 128
BAND_HEADS, BAND_HEAD_DIM, BAND_PAST, REL_CLIP = 32, 64, 512, 256
FOX_HEADS, FOX_HEAD_DIM = 16, 128
POOL_WINDOWS = (2, 4, 8, 16)
POOL_GROUP = D_MODEL // len(POOL_WINDOWS)
POOL_HALO = 16

LANES = 128
VMEM_LIMIT = 58 * 1024 * 1024

NT_DIMS = (((1,), (1,)), ((), ()))


def _silu(x):
    return x / (1.0 + jnp.exp(-x))


def _params(*sem):
    return pltpu.CompilerParams(dimension_semantics=sem, vmem_limit_bytes=VMEM_LIMIT)


def _ada_kernel(c_ref, w_ref, b_ref, o_ref):
    ca = _silu(c_ref[...]).astype(BF16)
    o_ref[0] = jnp.dot(ca, w_ref[0].astype(BF16), preferred_element_type=F32) + b_ref[0]


def _ada_mod(c, ada_w, ada_b):
    b, d = c.shape
    bp = -(-b // 8) * 8
    depth, _, n3 = ada_w.shape
    tn = 768
    cp = jnp.pad(c, ((0, bp - b), (0, 0)))
    return pl.pallas_call(
        _ada_kernel,
        name="ada_mod",
        out_shape=jax.ShapeDtypeStruct((depth, bp, n3), F32),
        grid=(depth, n3 // tn),
        in_specs=[pl.BlockSpec((bp, d), lambda l, j: (0, 0)),
                  pl.BlockSpec((1, d, tn), lambda l, j: (l, 0, j)),
                  pl.BlockSpec((1, 1, tn), lambda l, j: (l, 0, j))],
        out_specs=pl.BlockSpec((1, bp, tn), lambda l, j: (l, 0, j)),
        compiler_params=_params("parallel", "parallel"),
    )(cp, ada_w, ada_b.reshape(depth, 1, n3))


def _norm_proj_kernel(x_ref, g_ref, sh_ref, sc_ref, w_ref, cs_ref, *rest, has_f, emit_h, sub):
    rest = list(rest)
    wf_ref = rest.pop(0) if has_f else None
    o_ref = rest.pop(0)
    hout_ref = rest.pop(0) if emit_h else None
    f_ref = rest.pop(0) if has_f else None
    h_ref = rest.pop(0)

    @pl.when(pl.program_id(1) == 0)
    def _():
        g = g_ref[...]
        sc = 1.0 + sc_ref[0]
        sh = sh_ref[0]
        for r in range(x_ref.shape[0] // sub):
            rows = slice(r * sub, (r + 1) * sub)
            x = x_ref[rows, :]
            y = x * lax.rsqrt(jnp.mean(x * x, axis=-1, keepdims=True) + EPS) * g
            h = (y * sc + sh).astype(BF16)
            h_ref[rows, :] = h
            if emit_h:
                hout_ref[rows, :] = h
            if has_f:
                f_ref[rows, :] = jnp.dot(h, wf_ref[...], preferred_element_type=F32)
            y = jnp.dot(h, w_ref[...], preferred_element_type=F32)
            o_ref[rows, :] = (y * cs_ref[...]).astype(o_ref.dtype)

    @pl.when(pl.program_id(1) > 0)
    def _():
        y = jnp.dot(h_ref[...], w_ref[...], preferred_element_type=F32)
        o_ref[...] = (y * cs_ref[...]).astype(o_ref.dtype)


def _norm_proj(x2, seq, g, mod3, w, col_blocks, col_scale, wf=None, emit_h=False, out_dtype=BF16,
               tm=1024, tn=1024):
    t, d = x2.shape
    tm = min(tm, seq)
    nb = seq // tm
    nj = len(col_blocks)
    n = nj * tn
    first, skip_from, skip = col_blocks[0], None, 0
    for a, b2 in zip(col_blocks, col_blocks[1:]):
        if b2 != a + 1:
            skip_from, skip = a - first + 1, b2 - a - 1
    if skip_from is None:
        wmap = lambda i, j: (0, j + first)
    else:
        wmap = lambda i, j: (0, j + first + jnp.where(j >= skip_from, skip, 0))
    in_specs = [pl.BlockSpec((tm, d), lambda i, j: (i, 0)),
                pl.BlockSpec((1, d), lambda i, j: (0, 0)),
                pl.BlockSpec((1, 1, d), lambda i, j: (i // nb, 0, 0)),
                pl.BlockSpec((1, 1, d), lambda i, j: (i // nb, 0, 1)),
                pl.BlockSpec((d, tn), wmap),
                pl.BlockSpec((1, tn), lambda i, j: (0, j))]
    out_shape = [jax.ShapeDtypeStruct((t, n), out_dtype)]
    out_specs = [pl.BlockSpec((tm, tn), lambda i, j: (i, j))]
    args = [x2, g.reshape(1, d), mod3, mod3, w, col_scale.reshape(1, n).astype(F32)]
    if wf is not None:
        in_specs.append(pl.BlockSpec((d, LANES), lambda i, j: (0, 0)))
        args.append(wf)
    if emit_h:
        out_shape.append(jax.ShapeDtypeStruct((t, d), BF16))
        out_specs.append(pl.BlockSpec((tm, d), lambda i, j: (i, 0)))
    if wf is not None:
        out_shape.append(jax.ShapeDtypeStruct((t, LANES), F32))
        out_specs.append(pl.BlockSpec((tm, LANES), lambda i, j: (i, 0)))
    return pl.pallas_call(
        functools.partial(_norm_proj_kernel, has_f=wf is not None, emit_h=emit_h, sub=min(256, tm)),
        name="norm_proj",
        out_shape=out_shape,
        grid=(t // tm, nj),
        in_specs=in_specs,
        out_specs=out_specs,
        scratch_shapes=[pltpu.VMEM((tm, d), BF16)],
        compiler_params=_params("parallel", "arbitrary"),
    )(*args)


def _proj_t_kernel(w_ref, h_ref, o_ref):
    o_ref[...] = lax.dot_general(w_ref[...], h_ref[...], NT_DIMS,
                                 preferred_element_type=F32).astype(o_ref.dtype)


def _proj_t(w_t, h, tn=1024, tm=2048):
    n, d = w_t.shape
    t = h.shape[0]
    tm = min(tm, t)
    return pl.pallas_call(
        _proj_t_kernel,
        name="proj_t",
        out_shape=jax.ShapeDtypeStruct((n, t), BF16),
        grid=(t // tm, n // tn),
        in_specs=[pl.BlockSpec((tn, d), lambda i, j: (j, 0)),
                  pl.BlockSpec((tm, d), lambda i, j: (i, 0))],
        out_specs=pl.BlockSpec((tn, tm), lambda i, j: (j, i)),
        compiler_params=_params("parallel", "arbitrary"),
    )(w_t, h)


def _out_proj_kernel(o_ref, w_ref, x_ref, gate_ref, g_ref, out_ref, *, sub):
    blocks = [slice(r * sub, (r + 1) * sub) for r in range(o_ref.shape[0] // sub)]
    ys = [jnp.dot(o_ref[rows, :], w_ref[...], preferred_element_type=F32) for rows in blocks]
    for rows, y in zip(blocks, ys):
        yn = y * lax.rsqrt(jnp.mean(y * y, axis=-1, keepdims=True) + EPS) * g_ref[...]
        out_ref[rows, :] = x_ref[rows, :] + gate_ref[0] * yn


def _out_proj_residual(o2, w, x2, seq, mod3, g, tm=512):
    t, d = x2.shape
    tm = min(tm, seq)
    nb = seq // tm
    return pl.pallas_call(
        functools.partial(_out_proj_kernel, sub=min(256, tm)),
        name="out_proj",
        out_shape=jax.ShapeDtypeStruct((t, d), F32),
        grid=(t // tm,),
        in_specs=[pl.BlockSpec((tm, d), lambda i: (i, 0)),
                  pl.BlockSpec((d, d), lambda i: (0, 0), pipeline_mode=pl.Buffered(1)),
                  pl.BlockSpec((tm, d), lambda i: (i, 0)),
                  pl.BlockSpec((1, 1, d), lambda i: (i // nb, 0, 2)),
                  pl.BlockSpec((1, d), lambda i: (0, 0))],
        out_specs=pl.BlockSpec((tm, d), lambda i: (i, 0)),
        compiler_params=_params("parallel"),
    )(o2, w, x2, mod3, g.reshape(1, d))


def _flash_step_t(s_t, s_max, v_t, m_ref, l_ref, acc_ref, idx, cols):
    m_prev = m_ref[idx, :, cols]
    m_new = jnp.maximum(m_prev, s_max)
    alpha = jnp.exp2(m_prev - m_new)
    p_t = jnp.exp2(s_t - m_new)
    if l_ref is not None:
        l_ref[idx, :, cols] = alpha * l_ref[idx, :, cols] + jnp.sum(p_t, axis=0, keepdims=True)
    acc_ref[idx, :, cols] = (alpha * acc_ref[idx, :, cols]
                             + jnp.dot(v_t, p_t.astype(BF16), preferred_element_type=F32))
    m_ref[idx, :, cols] = m_new


def _flash_init(m_ref, l_ref, acc_ref):
    m_ref[...] = jnp.full(m_ref.shape, NEG_INF, F32)
    if l_ref is not None:
        l_ref[...] = jnp.zeros(l_ref.shape, F32)
    acc_ref[...] = jnp.zeros(acc_ref.shape, F32)


ONES_ROWS = 16


ATTN_TK = 512


def _mask_diagonal(s_ref, mx_ref, c0, granule):
    t = LANES
    kg = lax.broadcasted_iota(jnp.int32, (t, t), 0) // granule
    qg = lax.broadcasted_iota(jnp.int32, (t, t), 1) // granule
    visible = kg <= qg
    for r in range(ATTN_TK // t):
        rows = slice(r * t, (r + 1) * t)
        if r:
            s_ref[rows, c0:c0 + r * t] = jnp.full((t, r * t), NEG_INF, F32)
        diag = slice(c0 + r * t, c0 + (r + 1) * t)
        s_ref[rows, diag] = jnp.where(visible, s_ref[rows, diag], NEG_INF)
    cols = slice(c0, c0 + ATTN_TK)
    mx_ref[:, cols] = jnp.max(s_ref[:, cols], axis=0, keepdims=True)


def _causal_blocks(nq, begin, scores, mask, update, finish, buf0, buf1):
    every, late = slice(None), slice(ATTN_TK, 2 * ATTN_TK)
    begin(0)
    scores(0, buf0, every)

    def pairs(j, n):
        for _ in range(n):
            scores(j + 1, buf1, every)
            update(j, buf0, every)
            scores(j + 2, buf0, every)
            update(j + 1, buf1, every)
            j = j + 2

    def qblock(qi, carry):
        def quad(jj, c):
            pairs(4 * jj, 2)
            return c

        lax.fori_loop(0, qi // 2, quad, 0)

        @pl.when(qi % 2 == 1)
        def _():
            pairs(2 * qi - 2, 1)

        scores(2 * qi + 1, buf1, late)
        mask(buf0, 0)
        update(2 * qi, buf0, every)
        mask(buf1, ATTN_TK)
        update(2 * qi + 1, buf1, late)
        finish(qi)
        begin(jnp.minimum(qi + 1, nq - 1))
        scores(0, buf0, every)
        return carry

    lax.fori_loop(0, nq, qblock, 0)


def _diff_attn_kernel(q_ref, k_ref, vt_ref, z_ref, lq1_ref, lk1_ref, lq2_ref, lk2_ref, g_ref,
                      o_ref, m_ref, l_ref, acc_ref, s0_ref, s1_ref, mx0_ref, mx1_ref, qc_ref,
                      *, nq, lambda_init):
    tq = 2 * ATTN_TK
    lam = (jnp.exp(jnp.sum(lq1_ref[...] * lk1_ref[...], axis=-1, keepdims=True))
           - jnp.exp(jnp.sum(lq2_ref[...] * lk2_ref[...], axis=-1, keepdims=True)) + lambda_init)
    g = g_ref[...] * (1.0 - lambda_init)

    def begin(qi):
        qc_ref[...] = q_ref[pl.ds(pl.multiple_of(qi * tq, tq), tq), :].T
        _flash_init(m_ref, l_ref, acc_ref)

    def finish(qi):
        rows = pl.ds(pl.multiple_of(qi * tq, tq), tq)
        o_t = acc_ref[0] * (1.0 / l_ref[0]) - acc_ref[1] * (lam / l_ref[1])
        o = o_t.T
        on = o * lax.rsqrt(jnp.mean(o * o, axis=-1, keepdims=True) + EPS) * g
        o_ref[rows, :] = (on * _silu(z_ref[rows, :].astype(F32))).astype(o_ref.dtype)

    def scores(ki, buf, cols):
        s_ref, mx_ref = buf
        k = k_ref[pl.ds(pl.multiple_of(ki * ATTN_TK, ATTN_TK), ATTN_TK), :]
        for m in range(2):
            sl = slice(m * DIFF_QK_DIM, (m + 1) * DIFF_QK_DIM)
            s_t = jnp.dot(k[:, sl], qc_ref[sl, cols], preferred_element_type=F32)
            s_ref[m, :, cols] = s_t
            mx_ref[m, :, cols] = jnp.max(s_t, axis=0, keepdims=True)

    def mask(buf, c0):
        s_ref, mx_ref = buf
        for m in range(2):
            _mask_diagonal(s_ref.at[m], mx_ref.at[m], c0, CHUNK)

    def update(ki, buf, cols):
        s_ref, mx_ref = buf
        v_t = vt_ref[:, pl.ds(pl.multiple_of(ki * ATTN_TK, ATTN_TK), ATTN_TK)]
        for m in range(2):
            _flash_step_t(s_ref[m, :, cols], mx_ref[m, :, cols], v_t, m_ref, l_ref, acc_ref, m, cols)

    _causal_blocks(nq, begin, scores, mask, update, finish, (s0_ref, mx0_ref), (s1_ref, mx1_ref))


def _diff_attention(qkz, v_t, batch, seq, lq1, lk1, lq2, lk2, subln_g, lambda_init):
    t = qkz.shape[0]
    tq = 2 * ATTN_TK
    nq = seq // tq
    h, dv = DIFF_HEADS, DIFF_V_DIM
    vec = lambda a: a.reshape(1, -1).astype(F32)
    small = lambda n: pl.BlockSpec((1, n), lambda b, hh: (0, 0))
    return pl.pallas_call(
        functools.partial(_diff_attn_kernel, nq=nq, lambda_init=lambda_init),
        name="diff_attn",
        out_shape=jax.ShapeDtypeStruct((t, h * dv), BF16),
        grid=(batch, h),
        in_specs=[pl.BlockSpec((seq, dv), lambda b, hh: (b, hh)),
                  pl.BlockSpec((seq, dv), lambda b, hh: (b, h + hh)),
                  pl.BlockSpec((dv, seq), lambda b, hh: (hh, b)),
                  pl.BlockSpec((seq, dv), lambda b, hh: (b, 2 * h + hh)),
                  small(DIFF_QK_DIM), small(DIFF_QK_DIM), small(DIFF_QK_DIM), small(DIFF_QK_DIM),
                  small(dv)],
        out_specs=pl.BlockSpec((seq, dv), lambda b, hh: (b, hh)),
        scratch_shapes=[pltpu.VMEM((2, 1, tq), F32), pltpu.VMEM((2, 1, tq), F32),
                        pltpu.VMEM((2, dv, tq), F32),
                        pltpu.VMEM((2, ATTN_TK, tq), F32), pltpu.VMEM((2, ATTN_TK, tq), F32),
                        pltpu.VMEM((2, 1, tq), F32), pltpu.VMEM((2, 1, tq), F32),
                        pltpu.VMEM((dv, tq), BF16)],
        compiler_params=_params("parallel", "parallel"),
    )(qkz, qkz, v_t, qkz, vec(lq1), vec(lk1), vec(lq2), vec(lk2), vec(subln_g))


FOX_PIECES = 3


def _fox_gate_kernel(f_ref, b_ref, o_ref):
    x = f_ref[...] + b_ref[...]
    lf = jnp.minimum(x, 0.0) - jnp.log1p(jnp.exp(-jnp.abs(x)))
    n = lf.shape[0]
    row = lax.broadcasted_iota(jnp.int32, lf.shape, 0)
    d = 1
    while d < n:
        lf = lf + jnp.where(row >= d, pltpu.roll(lf, d, 0), 0.0)
        d *= 2
    rest = lf * LOG2E
    lane = lax.broadcasted_iota(jnp.int32, lf.shape, 1)
    out = jnp.zeros(lf.shape, F32)
    for p in range(FOX_PIECES):
        piece = rest.astype(BF16).astype(F32)
        rest = rest - piece
        moved = piece if p == 0 else pltpu.roll(piece, p * FOX_HEADS, 1)
        out = jnp.where((lane >= p * FOX_HEADS) & (lane < (p + 1) * FOX_HEADS), moved, out)
    o_ref[...] = out.astype(BF16)


def _fox_gates(f_logit, bias, batch, seq):
    bpad = jnp.zeros((1, LANES), F32).at[0, :FOX_HEADS].set(bias.astype(F32))
    return pl.pallas_call(
        _fox_gate_kernel,
        name="fox_gates",
        out_shape=jax.ShapeDtypeStruct((batch * seq, LANES), BF16),
        grid=(batch,),
        in_specs=[pl.BlockSpec((seq, LANES), lambda b: (b, 0)),
                  pl.BlockSpec((1, LANES), lambda b: (0, 0))],
        out_specs=pl.BlockSpec((seq, LANES), lambda b: (b, 0)),
        compiler_params=_params("parallel"),
    )(f_logit, bpad)


def _fox_place_kernel(f_ref, pk_ref, pq_ref, ck_ref, cq_ref, kx_ref, qx_ref):
    f = f_ref[...]
    kx_ref[...] = (jnp.dot(f, pk_ref[...], preferred_element_type=F32) + ck_ref[...]).astype(BF16)
    qx_ref[...] = (jnp.dot(f, pq_ref[...], preferred_element_type=F32) + cq_ref[...]).astype(BF16)


def _fox_extend(pieces, tm=1024):
    t = pieces.shape[0]
    tm = min(tm, t)
    n = FOX_HEADS * LANES
    src = jnp.arange(LANES)[:, None]
    dst = jnp.arange(n)[None, :]
    p_src, h_src = src // FOX_HEADS, src % FOX_HEADS
    h_dst, c_dst = dst // LANES, dst % LANES
    live = (p_src < FOX_PIECES) & (h_src == h_dst)
    pk = jnp.where(live & (c_dst == p_src), -1.0, 0.0).astype(BF16)
    pq = jnp.where(live & (c_dst == p_src + FOX_PIECES), 1.0, 0.0).astype(BF16)
    ck = ((c_dst >= FOX_PIECES) & (c_dst < 2 * FOX_PIECES)).astype(F32)
    cq = (c_dst < FOX_PIECES).astype(F32)
    mat = pl.BlockSpec((LANES, n), lambda i: (0, 0))
    vec = pl.BlockSpec((1, n), lambda i: (0, 0))
    return pl.pallas_call(
        _fox_place_kernel,
        name="fox_extend",
        out_shape=[jax.ShapeDtypeStruct((t, n), BF16), jax.ShapeDtypeStruct((t, n), BF16)],
        grid=(t // tm,),
        in_specs=[pl.BlockSpec((tm, LANES), lambda i: (i, 0)), mat, mat, vec, vec],
        out_specs=[pl.BlockSpec((tm, n), lambda i: (i, 0)), pl.BlockSpec((tm, n), lambda i: (i, 0))],
        compiler_params=_params("parallel"),
    )(pieces, pk, pq, ck, cq)


def _fox_attn_kernel(q_ref, qx_ref, k_ref, kx_ref, vt_ref, z_ref, o_ref, m_ref, acc_ref,
                     s0_ref, s1_ref, mx0_ref, mx1_ref, qt_ref, vte_ref, *, nq):
    tq = 2 * ATTN_TK
    dh = FOX_HEAD_DIM
    vte_ref[:dh, :] = vt_ref[...]
    vte_ref[dh:, :] = jnp.ones((ONES_ROWS, vte_ref.shape[1]), BF16)

    def begin(qi):
        rows = pl.ds(pl.multiple_of(qi * tq, tq), tq)
        qt_ref[:dh, :] = q_ref[rows, :].T
        qt_ref[dh:, :] = qx_ref[rows, :].T
        _flash_init(m_ref, None, acc_ref)

    def finish(qi):
        rows = pl.ds(pl.multiple_of(qi * tq, tq), tq)
        o = (acc_ref[0, :dh, :] * (1.0 / acc_ref[0, dh:dh + 1, :])).T
        o_ref[rows, :] = (o * _silu(z_ref[rows, :].astype(F32))).astype(o_ref.dtype)

    def scores(ki, buf, cols):
        s_ref, mx_ref = buf
        off = pl.multiple_of(ki * ATTN_TK, ATTN_TK)
        k = jnp.concatenate([k_ref[pl.ds(off, ATTN_TK), :], kx_ref[pl.ds(off, ATTN_TK), :]], axis=1)
        s_t = jnp.dot(k, qt_ref[:, cols], preferred_element_type=F32)
        s_ref[:, cols] = s_t
        mx_ref[:, cols] = jnp.max(s_t, axis=0, keepdims=True)

    def mask(buf, c0):
        _mask_diagonal(buf[0], buf[1], c0, 1)

    def update(ki, buf, cols):
        s_ref, mx_ref = buf
        v_t = vte_ref[:, pl.ds(pl.multiple_of(ki * ATTN_TK, ATTN_TK), ATTN_TK)]
        _flash_step_t(s_ref[:, cols], mx_ref[:, cols], v_t, m_ref, None, acc_ref, 0, cols)

    _causal_blocks(nq, begin, scores, mask, update, finish, (s0_ref, mx0_ref), (s1_ref, mx1_ref))


def _fox_attention(qkz, v_t, kx, qx, batch, seq):
    t = qkz.shape[0]
    tq = 2 * ATTN_TK
    nq = seq // tq
    h, dh = FOX_HEADS, FOX_HEAD_DIM
    return pl.pallas_call(
        functools.partial(_fox_attn_kernel, nq=nq),
        name="fox_attn",
        out_shape=jax.ShapeDtypeStruct((t, h * dh), BF16),
        grid=(batch, h),
        in_specs=[pl.BlockSpec((seq, dh), lambda b, hh: (b, hh)),
                  pl.BlockSpec((seq, LANES), lambda b, hh: (b, hh)),
                  pl.BlockSpec((seq, dh), lambda b, hh: (b, h + hh)),
                  pl.BlockSpec((seq, LANES), lambda b, hh: (b, hh)),
                  pl.BlockSpec((dh, seq), lambda b, hh: (hh, b)),
                  pl.BlockSpec((seq, dh), lambda b, hh: (b, 2 * h + hh))],
        out_specs=pl.BlockSpec((seq, dh), lambda b, hh: (b, hh)),
        scratch_shapes=[pltpu.VMEM((1, 1, tq), F32),
                        pltpu.VMEM((1, dh + ONES_ROWS, tq), F32),
                        pltpu.VMEM((ATTN_TK, tq), F32), pltpu.VMEM((ATTN_TK, tq), F32),
                        pltpu.VMEM((1, tq), F32), pltpu.VMEM((1, tq), F32),
                        pltpu.VMEM((dh + LANES, tq), BF16),
                        pltpu.VMEM((dh + ONES_ROWS, seq), BF16)],
        compiler_params=_params("parallel", "parallel"),
    )(qkz, qx, qkz, kx, v_t, qkz)


BAND_TQ = 256
BAND_WIN = BAND_PAST + BAND_TQ
BAND_HPB = LANES // BAND_HEAD_DIM
BAND_REV = 1024


def _band_bias_kernel(row_ref, o_ref):
    x = jnp.broadcast_to(row_ref[0], (BAND_WIN, BAND_REV))
    x = pltpu.roll(x, BAND_REV - BAND_WIN, 1, stride=1, stride_axis=0)
    x = x[:, :BAND_TQ]
    kc = lax.broadcasted_iota(jnp.int32, x.shape, 0) // CHUNK
    qc = lax.broadcasted_iota(jnp.int32, x.shape, 1) // CHUNK
    band = (kc >= qc) & (kc <= qc + BAND_PAST // CHUNK)
    o_ref[0] = jnp.where(band, x * LOG2E, NEG_INF)


def _band_bias(rel_table):
    nh = rel_table.shape[0]
    t = rel_table.astype(F32)
    row = jnp.concatenate([t, jnp.broadcast_to(t[:, 2 * REL_CLIP:], (nh, BAND_REV - t.shape[1]))], axis=1)
    return pl.pallas_call(
        _band_bias_kernel,
        name="band_bias",
        out_shape=jax.ShapeDtypeStruct((nh, BAND_WIN, BAND_TQ), F32),
        grid=(nh,),
        in_specs=[pl.BlockSpec((1, 1, BAND_REV), lambda h: (h, 0, 0))],
        out_specs=pl.BlockSpec((1, BAND_WIN, BAND_TQ), lambda h: (h, 0, 0)),
        compiler_params=_params("parallel"),
    )(row.reshape(nh, 1, BAND_REV))


def _band_attn_kernel(q_ref, k_ref, vt_ref, z_ref, bias_ref, o_ref, kp_ref, vtp_ref, s0_ref, s1_ref,
                      mx0_ref, mx1_ref, *, nq):
    hd = BAND_HEAD_DIM
    for hh in range(BAND_HPB):
        kp_ref[hh, :BAND_PAST, :] = jnp.zeros((BAND_PAST, hd), BF16)
        kp_ref[hh, BAND_PAST:, :] = k_ref[:, hh * hd:(hh + 1) * hd]
        vtp_ref[hh, :hd, :BAND_PAST] = jnp.zeros((hd, BAND_PAST), BF16)
        vtp_ref[hh, :hd, BAND_PAST:] = vt_ref[hh * hd:(hh + 1) * hd, :]
        vtp_ref[hh, hd:, :] = jnp.ones((ONES_ROWS, vtp_ref.shape[2]), BF16)

    def scores(j, buf):
        s_ref, mx_ref = buf
        start = pl.multiple_of(j * BAND_TQ, BAND_TQ)
        q_t = q_ref[pl.ds(start, BAND_TQ), :].T
        for hh in range(BAND_HPB):
            s_t = jnp.dot(kp_ref[hh, pl.ds(start, BAND_WIN), :], q_t[hh * hd:(hh + 1) * hd, :],
                          preferred_element_type=F32) + bias_ref[hh]
            s_ref[hh] = s_t
            mx_ref[hh] = jnp.max(s_t, axis=0, keepdims=True)

    def update(j, buf, masked):
        s_ref, mx_ref = buf
        start = pl.multiple_of(j * BAND_TQ, BAND_TQ)
        outs = []
        for hh in range(BAND_HPB):
            s_t = s_ref[hh]
            s_max = mx_ref[hh]
            if masked:
                kpos = start - BAND_PAST + lax.broadcasted_iota(jnp.int32, (BAND_WIN, 1), 0)
                s_t = jnp.where(kpos >= 0, s_t, NEG_INF)
                s_max = jnp.max(s_t, axis=0, keepdims=True)
            p_t = jnp.exp2(s_t - s_max).astype(BF16)
            ol = jnp.dot(vtp_ref[hh, :, pl.ds(start, BAND_WIN)], p_t, preferred_element_type=F32)
            outs.append(ol[:hd] * (1.0 / ol[hd:hd + 1]))
        o = jnp.concatenate(outs, axis=0).T
        z = z_ref[pl.ds(start, BAND_TQ), :].astype(F32)
        o_ref[pl.ds(start, BAND_TQ), :] = (o * _silu(z)).astype(o_ref.dtype)

    buf0, buf1 = (s0_ref, mx0_ref), (s1_ref, mx1_ref)
    scores(0, buf0)
    scores(1, buf1)
    update(0, buf0, True)
    scores(2, buf0)
    update(1, buf1, True)

    def pairs(j, n):
        for _ in range(n):
            scores(j + 1, buf1)
            update(j, buf0, False)
            scores(j + 2, buf0)
            update(j + 1, buf1, False)
            j = j + 2

    def quad(jj, carry):
        pairs(2 + 4 * jj, 2)
        return carry

    n_pairs = (nq - 4) // 2
    lax.fori_loop(0, n_pairs // 2, quad, 0)
    pairs(2 + 4 * (n_pairs // 2), n_pairs % 2)
    scores(nq - 1, buf1)
    update(nq - 2, buf0, False)
    update(nq - 1, buf1, False)


def _band_attention(qkz, v_t, bias_t, batch, seq):
    t = qkz.shape[0]
    nq = seq // BAND_TQ
    assert nq >= 4 and nq % 2 == 0
    hp = BAND_HEADS // BAND_HPB
    return pl.pallas_call(
        functools.partial(_band_attn_kernel, nq=nq),
        name="band_attn",
        out_shape=jax.ShapeDtypeStruct((t, BAND_HEADS * BAND_HEAD_DIM), BF16),
        grid=(batch, hp),
        in_specs=[pl.BlockSpec((seq, LANES), lambda b, g: (b, g)),
                  pl.BlockSpec((seq, LANES), lambda b, g: (b, hp + g)),
                  pl.BlockSpec((LANES, seq), lambda b, g: (g, b)),
                  pl.BlockSpec((seq, LANES), lambda b, g: (b, 2 * hp + g)),
                  pl.BlockSpec((BAND_HPB, BAND_WIN, BAND_TQ), lambda b, g: (g, 0, 0))],
        out_specs=pl.BlockSpec((seq, LANES), lambda b, g: (b, g)),
        scratch_shapes=[pltpu.VMEM((BAND_HPB, seq + BAND_PAST, BAND_HEAD_DIM), BF16),
                        pltpu.VMEM((BAND_HPB, BAND_HEAD_DIM + ONES_ROWS, seq + BAND_PAST), BF16),
                        pltpu.VMEM((BAND_HPB, BAND_WIN, BAND_TQ), F32),
                        pltpu.VMEM((BAND_HPB, BAND_WIN, BAND_TQ), F32),
                        pltpu.VMEM((BAND_HPB, 1, BAND_TQ), F32),
                        pltpu.VMEM((BAND_HPB, 1, BAND_TQ), F32)],
        compiler_params=_params("parallel", "parallel"),
    )(qkz, qkz, v_t, qkz, bias_t)


def _pool_kernel(u_ref, up_ref, z_ref, w_ref, ps_ref, o_ref, *, tm, nb):
    i = pl.program_id(0)
    t0 = (i % nb) * tm
    has_history = t0 > 0
    t = t0 + lax.broadcasted_iota(jnp.int32, (tm, 1), 0)
    for g, win in enumerate(POOL_WINDOWS):
        sl = slice(g * POOL_GROUP, (g + 1) * POOL_GROUP)
        u = u_ref[:, sl]
        acc = jnp.concatenate([jnp.where(has_history, up_ref[:, sl], 0.0), u], axis=0)
        d = 1
        while d < win:
            acc = acc + pltpu.roll(acc, d, 0)
            d *= 2
        wsum = acc[POOL_HALO:, :]
        inv_cnt = 1.0 / jnp.minimum(t + 1, win).astype(F32)
        delta = (wsum * inv_cnt - u).astype(BF16)
        y = jnp.dot(delta, w_ref[g], preferred_element_type=F32) * ps_ref[:, sl]
        o_ref[:, sl] = (y * _silu(z_ref[:, sl])).astype(o_ref.dtype)


def _pool_mix(uz, seq, pool_w, pool_scale, tm=512):
    t = uz.shape[0]
    d = D_MODEL
    tm = min(tm, seq)
    nb = seq // tm
    hb = tm // POOL_HALO
    return pl.pallas_call(
        functools.partial(_pool_kernel, tm=tm, nb=nb),
        name="pool_mix",
        out_shape=jax.ShapeDtypeStruct((t, d), BF16),
        grid=(t // tm,),
        in_specs=[pl.BlockSpec((tm, d), lambda i: (i, 0)),
                  pl.BlockSpec((POOL_HALO, d), lambda i: (jnp.maximum(i * hb - 1, 0), 0)),
                  pl.BlockSpec((tm, d), lambda i: (i, 1)),
                  pl.BlockSpec((len(POOL_WINDOWS), POOL_GROUP, POOL_GROUP), lambda i: (0, 0, 0)),
                  pl.BlockSpec((1, d), lambda i: (0, 0))],
        out_specs=pl.BlockSpec((tm, d), lambda i: (i, 0)),
        compiler_params=_params("parallel"),
    )(uz, uz, uz, pool_w, pool_scale.reshape(1, d).astype(F32))


QKZ_TN = 2048
POOL_TN = 1024


def _transpose_cast_kernel(x_ref, o_ref):
    o_ref[...] = x_ref[...].T.astype(o_ref.dtype)


def _bf16_weights(w_in, branch, tile=512):
    d = w_in.shape[0]
    first = 2 * branch // tile
    w = w_in.astype(BF16)
    w_vt = pl.pallas_call(
        _transpose_cast_kernel,
        name="transpose_cast",
        out_shape=jax.ShapeDtypeStruct((branch, d), BF16),
        grid=(d // tile, branch // tile),
        in_specs=[pl.BlockSpec((tile, tile), lambda i, j: (i, first + j))],
        out_specs=pl.BlockSpec((tile, tile), lambda i, j: (j, i)),
        compiler_params=_params("parallel", "parallel"),
    )(w)
    return w, w_vt


def _qkz_scale(qk_dim, branch):
    return jnp.concatenate([jnp.full((branch,), qk_dim ** -0.5 * LOG2E, F32), jnp.ones((2 * branch,), F32)])


def kernel(x, c, ada_w, ada_b, norm_pre, norm_post, diff_w_in, diff_w_out, diff_lambda_q1, diff_lambda_k1, diff_lambda_q2, diff_lambda_k2, diff_subln, band_w_in, band_w_out, band_rel_bias, fox_w_in, fox_w_out, fox_forget_bias, pool_w_in, pool_w_out, pool_group_w, pool_scale):
    batch, seq, d = x.shape
    depth = ada_w.shape[0]
    branch = d
    nblk = branch // QKZ_TN
    qkz_blocks = list(range(2 * nblk)) + list(range(3 * nblk, 4 * nblk))
    mod = _ada_mod(c, ada_w, ada_b)
    x2 = x.reshape(batch * seq, d)
    for i in range(depth):
        mod3 = mod[i].reshape(mod.shape[1], 1, 3 * d)
        kind = i % 4
        if kind == 0:
            lambda_init = 0.8 - 0.6 * math.exp(-0.3 * i)
            w, w_vt = _bf16_weights(diff_w_in, branch)
            qkz, h = _norm_proj(x2, seq, norm_pre[i], mod3, w, qkz_blocks,
                                _qkz_scale(DIFF_QK_DIM, branch), emit_h=True, tn=QKZ_TN)
            v_t = _proj_t(w_vt, h)
            o = _diff_attention(qkz, v_t, batch, seq, diff_lambda_q1, diff_lambda_k1, diff_lambda_q2,
                                diff_lambda_k2, diff_subln, lambda_init)
            w_out = diff_w_out
        elif kind == 1:
            w, w_vt = _bf16_weights(band_w_in, branch)
            qkz, h = _norm_proj(x2, seq, norm_pre[i], mod3, w, qkz_blocks,
                                _qkz_scale(BAND_HEAD_DIM, branch), emit_h=True, tn=QKZ_TN)
            v_t = _proj_t(w_vt, h)
            o = _band_attention(qkz, v_t, _band_bias(band_rel_bias), batch, seq)
            w_out = band_w_out
        elif kind == 2:
            w, w_vt = _bf16_weights(fox_w_in, branch)
            wf = jnp.pad(w[:, 4 * branch:], ((0, 0), (0, LANES - FOX_HEADS)))
            qkz, h, f_logit = _norm_proj(x2, seq, norm_pre[i], mod3, w, qkz_blocks,
                                         _qkz_scale(FOX_HEAD_DIM, branch), wf=wf, emit_h=True, tn=QKZ_TN)
            v_t = _proj_t(w_vt, h)
            kx, qx = _fox_extend(_fox_gates(f_logit, fox_forget_bias, batch, seq))
            o = _fox_attention(qkz, v_t, kx, qx, batch, seq)
            w_out = fox_w_out
        else:
            uz, = _norm_proj(x2, seq, norm_pre[i], mod3, pool_w_in.astype(BF16),
                             list(range(2 * branch // POOL_TN)), jnp.ones((2 * branch,), F32),
                             out_dtype=F32, tn=POOL_TN)
            o = _pool_mix(uz, seq, pool_group_w.astype(BF16), pool_scale)
            w_out = pool_w_out
        x2 = _out_proj_residual(o, w_out.astype(BF16), x2, seq, mod3, norm_post[i])
    return x2.reshape(batch, seq, d)
```

```python
import functools
import math

import jax
import jax.numpy as jnp
from jax import lax
from jax.experimental import pallas as pl
from jax.experimental.pallas import tpu as pltpu

F32 = jnp.float32
BF16 = jnp.bfloat16

D_MODEL = 2048
EPS = 1e-6
NEG_INF = -1e30
LOG2E = math.log2(math.e)
CHUNK = 64
DIFF_HEADS, DIFF_V_DIM, DIFF_QK_DIM = 8, 256, 128
BAND_HEADS, BAND_HEAD_DIM, BAND_PAST, REL_CLIP = 32, 64, 512, 256
FOX_HEADS, FOX_HEAD_DIM = 16, 128
POOL_WINDOWS = (2, 4, 8, 16)
POOL_GROUP = D_MODEL // len(POOL_WINDOWS)
POOL_HALO = 16

LANES = 128
VMEM_LIMIT = 58 * 1024 * 1024

NT_DIMS = (((1,), (1,)), ((), ()))


def _silu(x):
    return x / (1.0 + jnp.exp(-x))


def _params(*sem):
    return pltpu.CompilerParams(dimension_semantics=sem, vmem_limit_bytes=VMEM_LIMIT)


def _ada_kernel(c_ref, w_ref, b_ref, o_ref):
    ca = _silu(c_ref[...]).astype(BF16)
    o_ref[0] = jnp.dot(ca, w_ref[0].astype(BF16), preferred_element_type=F32) + b_ref[0]


def _ada_mod(c, ada_w, ada_b):
    b, d = c.shape
    bp = -(-b // 8) * 8
    depth, _, n3 = ada_w.shape
    tn = 1536
    cp = jnp.pad(c, ((0, bp - b), (0, 0)))
    return pl.pallas_call(
        _ada_kernel,
        name="ada_mod",
        out_shape=jax.ShapeDtypeStruct((depth, bp, n3), F32),
        grid=(depth, n3 // tn),
        in_specs=[pl.BlockSpec((bp, d), lambda l, j: (0, 0)),
                  pl.BlockSpec((1, d, tn), lambda l, j: (l, 0, j)),
                  pl.BlockSpec((1, 1, tn), lambda l, j: (l, 0, j))],
        out_specs=pl.BlockSpec((1, bp, tn), lambda l, j: (l, 0, j)),
        compiler_params=_params("parallel", "parallel"),
    )(cp, ada_w, ada_b.reshape(depth, 1, n3))


def _norm_proj_kernel(x_ref, g_ref, sh_ref, sc_ref, w_ref, cs_ref, *rest, has_f, emit_h, sub):
    rest = list(rest)
    wf_ref = rest.pop(0) if has_f else None
    o_ref = rest.pop(0)
    hout_ref = rest.pop(0) if emit_h else None
    f_ref = rest.pop(0) if has_f else None
    h_ref = rest.pop(0)

    @pl.when(pl.program_id(1) == 0)
    def _():
        g = g_ref[...]
        sc = 1.0 + sc_ref[0]
        sh = sh_ref[0]
        for r in range(x_ref.shape[0] // sub):
            rows = slice(r * sub, (r + 1) * sub)
            x = x_ref[rows, :]
            y = x * lax.rsqrt(jnp.mean(x * x, axis=-1, keepdims=True) + EPS) * g
            h = (y * sc + sh).astype(BF16)
            h_ref[rows, :] = h
            if emit_h:
                hout_ref[rows, :] = h
            if has_f:
                f_ref[rows, :] = jnp.dot(h, wf_ref[...], preferred_element_type=F32)
            y = jnp.dot(h, w_ref[...], preferred_element_type=F32)
            o_ref[rows, :] = (y * cs_ref[...]).astype(o_ref.dtype)

    @pl.when(pl.program_id(1) > 0)
    def _():
        y = jnp.dot(h_ref[...], w_ref[...], preferred_element_type=F32)
        o_ref[...] = (y * cs_ref[...]).astype(o_ref.dtype)


def _norm_proj(x2, seq, g, mod3, w, col_blocks, col_scale, wf=None, emit_h=False, out_dtype=BF16,
               tm=1024, tn=1024):
    t, d = x2.shape
    tm = min(tm, seq)
    nb = seq // tm
    nj = len(col_blocks)
    n = nj * tn
    first, skip_from, skip = col_blocks[0], None, 0
    for a, b2 in zip(col_blocks, col_blocks[1:]):
        if b2 != a + 1:
            skip_from, skip = a - first + 1, b2 - a - 1
    if skip_from is None:
        wmap = lambda i, j: (0, j + first)
    else:
        wmap = lambda i, j: (0, j + first + jnp.where(j >= skip_from, skip, 0))
    in_specs = [pl.BlockSpec((tm, d), lambda i, j: (i, 0)),
                pl.BlockSpec((1, d), lambda i, j: (0, 0)),
                pl.BlockSpec((1, 1, d), lambda i, j: (i // nb, 0, 0)),
                pl.BlockSpec((1, 1, d), lambda i, j: (i // nb, 0, 1)),
                pl.BlockSpec((d, tn), wmap),
                pl.BlockSpec((1, tn), lambda i, j: (0, j))]
    out_shape = [jax.ShapeDtypeStruct((t, n), out_dtype)]
    out_specs = [pl.BlockSpec((tm, tn), lambda i, j: (i, j))]
    args = [x2, g.reshape(1, d), mod3, mod3, w, col_scale.reshape(1, n).astype(F32)]
    if wf is not None:
        in_specs.append(pl.BlockSpec((d, LANES), lambda i, j: (0, 0)))
        args.append(wf)
    if emit_h:
        out_shape.append(jax.ShapeDtypeStruct((t, d), BF16))
        out_specs.append(pl.BlockSpec((tm, d), lambda i, j: (i, 0)))
    if wf is not None:
        out_shape.append(jax.ShapeDtypeStruct((t, LANES), F32))
        out_specs.append(pl.BlockSpec((tm, LANES), lambda i, j: (i, 0)))
    return pl.pallas_call(
        functools.partial(_norm_proj_kernel, has_f=wf is not None, emit_h=emit_h, sub=min(256, tm)),
        name="norm_proj",
        out_shape=out_shape,
        grid=(t // tm, nj),
        in_specs=in_specs,
        out_specs=out_specs,
        scratch_shapes=[pltpu.VMEM((tm, d), BF16)],
        compiler_params=_params("parallel", "arbitrary"),
    )(*args)


def _proj_t_kernel(w_ref, h_ref, o_ref):
    o_ref[...] = lax.dot_general(w_ref[...], h_ref[...], NT_DIMS,
                                 preferred_element_type=F32).astype(o_ref.dtype)


def _proj_t(w_t, h, tn=1024, tm=2048):
    n, d = w_t.shape
    t = h.shape[0]
    tm = min(tm, t)
    return pl.pallas_call(
        _proj_t_kernel,
        name="proj_t",
        out_shape=jax.ShapeDtypeStruct((n, t), BF16),
        grid=(t // tm, n // tn),
        in_specs=[pl.BlockSpec((tn, d), lambda i, j: (j, 0)),
                  pl.BlockSpec((tm, d), lambda i, j: (i, 0))],
        out_specs=pl.BlockSpec((tn, tm), lambda i, j: (j, i)),
        compiler_params=_params("parallel", "arbitrary"),
    )(w_t, h)


def _out_proj_kernel(o_ref, w_ref, x_ref, gate_ref, g_ref, out_ref, *, sub):
    blocks = [slice(r * sub, (r + 1) * sub) for r in range(o_ref.shape[0] // sub)]
    ys = [jnp.dot(o_ref[rows, :], w_ref[...], preferred_element_type=F32) for rows in blocks]
    for rows, y in zip(blocks, ys):
        yn = y * lax.rsqrt(jnp.mean(y * y, axis=-1, keepdims=True) + EPS) * g_ref[...]
        out_ref[rows, :] = x_ref[rows, :] + gate_ref[0] * yn


def _out_proj_residual(o2, w, x2, seq, mod3, g, tm=512):
    t, d = x2.shape
    tm = min(tm, seq)
    nb = seq // tm
    return pl.pallas_call(
        functools.partial(_out_proj_kernel, sub=min(256, tm)),
        name="out_proj",
        out_shape=jax.ShapeDtypeStruct((t, d), F32),
        grid=(t // tm,),
        in_specs=[pl.BlockSpec((tm, d), lambda i: (i, 0)),
                  pl.BlockSpec((d, d), lambda i: (0, 0), pipeline_mode=pl.Buffered(1)),
                  pl.BlockSpec((tm, d), lambda i: (i, 0)),
                  pl.BlockSpec((1, 1, d), lambda i: (i // nb, 0, 2)),
                  pl.BlockSpec((1, d), lambda i: (0, 0))],
        out_specs=pl.BlockSpec((tm, d), lambda i: (i, 0)),
        compiler_params=_params("parallel"),
    )(o2, w, x2, mod3, g.reshape(1, d))


def _flash_step_t(s_t, s_max, v_t, m_ref, l_ref, acc_ref, idx, cols):
    m_prev = m_ref[idx, :, cols]
    m_new = jnp.maximum(m_prev, s_max)
    alpha = jnp.exp2(m_prev - m_new)
    p_t = jnp.exp2(s_t - m_new)
    if l_ref is not None:
        l_ref[idx, :, cols] = alpha * l_ref[idx, :, cols] + jnp.sum(p_t, axis=0, keepdims=True)
    acc_ref[idx, :, cols] = (alpha * acc_ref[idx, :, cols]
                             + jnp.dot(v_t, p_t.astype(BF16), preferred_element_type=F32))
    m_ref[idx, :, cols] = m_new


def _flash_init(m_ref, l_ref, acc_ref):
    m_ref[...] = jnp.full(m_ref.shape, NEG_INF, F32)
    if l_ref is not None:
        l_ref[...] = jnp.zeros(l_ref.shape, F32)
    acc_ref[...] = jnp.zeros(acc_ref.shape, F32)


ONES_ROWS = 16


ATTN_TK = 512


def _mask_diagonal(s_ref, mx_ref, c0, granule):
    t = LANES
    kg = lax.broadcasted_iota(jnp.int32, (t, t), 0) // granule
    qg = lax.broadcasted_iota(jnp.int32, (t, t), 1) // granule
    visible = kg <= qg
    for r in range(ATTN_TK // t):
        rows = slice(r * t, (r + 1) * t)
        if r:
            s_ref[rows, c0:c0 + r * t] = jnp.full((t, r * t), NEG_INF, F32)
        diag = slice(c0 + r * t, c0 + (r + 1) * t)
        s_ref[rows, diag] = jnp.where(visible, s_ref[rows, diag], NEG_INF)
    cols = slice(c0, c0 + ATTN_TK)
    mx_ref[:, cols] = jnp.max(s_ref[:, cols], axis=0, keepdims=True)


def _causal_blocks(nq, begin, scores, mask, update, finish, buf0, buf1):
    every, late = slice(None), slice(ATTN_TK, 2 * ATTN_TK)
    begin(0)
    scores(0, buf0, every)

    def pairs(j, n):
        for _ in range(n):
            scores(j + 1, buf1, every)
            update(j, buf0, every)
            scores(j + 2, buf0, every)
            update(j + 1, buf1, every)
            j = j + 2

    def qblock(qi, carry):
        def quad(jj, c):
            pairs(4 * jj, 2)
            return c

        lax.fori_loop(0, qi // 2, quad, 0)

        @pl.when(qi % 2 == 1)
        def _():
            pairs(2 * qi - 2, 1)

        scores(2 * qi + 1, buf1, late)
        mask(buf0, 0)
        update(2 * qi, buf0, every)
        mask(buf1, ATTN_TK)
        update(2 * qi + 1, buf1, late)
        finish(qi)
        begin(jnp.minimum(qi + 1, nq - 1))
        scores(0, buf0, every)
        return carry

    lax.fori_loop(0, nq, qblock, 0)


def _diff_attn_kernel(q_ref, k_ref, vt_ref, z_ref, lq1_ref, lk1_ref, lq2_ref, lk2_ref, g_ref,
                      o_ref, m_ref, l_ref, acc_ref, s0_ref, s1_ref, mx0_ref, mx1_ref, qc_ref,
                      *, nq, lambda_init):
    tq = 2 * ATTN_TK
    lam = (jnp.exp(jnp.sum(lq1_ref[...] * lk1_ref[...], axis=-1, keepdims=True))
           - jnp.exp(jnp.sum(lq2_ref[...] * lk2_ref[...], axis=-1, keepdims=True)) + lambda_init)
    g = g_ref[...] * (1.0 - lambda_init)

    def begin(qi):
        qc_ref[...] = q_ref[pl.ds(pl.multiple_of(qi * tq, tq), tq), :].T
        _flash_init(m_ref, l_ref, acc_ref)

    def finish(qi):
        rows = pl.ds(pl.multiple_of(qi * tq, tq), tq)
        o_t = acc_ref[0] * (1.0 / l_ref[0]) - acc_ref[1] * (lam / l_ref[1])
        o = o_t.T
        on = o * lax.rsqrt(jnp.mean(o * o, axis=-1, keepdims=True) + EPS) * g
        o_ref[rows, :] = (on * _silu(z_ref[rows, :].astype(F32))).astype(o_ref.dtype)

    def scores(ki, buf, cols):
        s_ref, mx_ref = buf
        k = k_ref[pl.ds(pl.multiple_of(ki * ATTN_TK, ATTN_TK), ATTN_TK), :]
        for m in range(2):
            sl = slice(m * DIFF_QK_DIM, (m + 1) * DIFF_QK_DIM)
            s_t = jnp.dot(k[:, sl], qc_ref[sl, cols], preferred_element_type=F32)
            s_ref[m, :, cols] = s_t
            mx_ref[m, :, cols] = jnp.max(s_t, axis=0, keepdims=True)

    def mask(buf, c0):
        s_ref, mx_ref = buf
        for m in range(2):
            _mask_diagonal(s_ref.at[m], mx_ref.at[m], c0, CHUNK)

    def update(ki, buf, cols):
        s_ref, mx_ref = buf
        v_t = vt_ref[:, pl.ds(pl.multiple_of(ki * ATTN_TK, ATTN_TK), ATTN_TK)]
        for m in range(2):
            _flash_step_t(s_ref[m, :, cols], mx_ref[m, :, cols], v_t, m_ref, l_ref, acc_ref, m, cols)

    _causal_blocks(nq, begin, scores, mask, update, finish, (s0_ref, mx0_ref), (s1_ref, mx1_ref))


def _diff_attention(qkz, v_t, batch, seq, lq1, lk1, lq2, lk2, subln_g, lambda_init):
    t = qkz.shape[0]
    tq = 2 * ATTN_TK
    nq = seq // tq
    h, dv = DIFF_HEADS, DIFF_V_DIM
    vec = lambda a: a.reshape(1, -1).astype(F32)
    small = lambda n: pl.BlockSpec((1, n), lambda b, hh: (0, 0))
    return pl.pallas_call(
        functools.partial(_diff_attn_kernel, nq=nq, lambda_init=lambda_init),
        name="diff_attn",
        out_shape=jax.ShapeDtypeStruct((t, h * dv), BF16),
        grid=(batch, h),
        in_specs=[pl.BlockSpec((seq, dv), lambda b, hh: (b, hh)),
                  pl.BlockSpec((seq, dv), lambda b, hh: (b, h + hh)),
                  pl.BlockSpec((dv, seq), lambda b, hh: (hh, b)),
                  pl.BlockSpec((seq, dv), lambda b, hh: (b, 2 * h + hh)),
                  small(DIFF_QK_DIM), small(DIFF_QK_DIM), small(DIFF_QK_DIM), small(DIFF_QK_DIM),
                  small(dv)],
        out_specs=pl.BlockSpec((seq, dv), lambda b, hh: (b, hh)),
        scratch_shapes=[pltpu.VMEM((2, 1, tq), F32), pltpu.VMEM((2, 1, tq), F32),
                        pltpu.VMEM((2, dv, tq), F32),
                        pltpu.VMEM((2, ATTN_TK, tq), F32), pltpu.VMEM((2, ATTN_TK, tq), F32),
                        pltpu.VMEM((2, 1, tq), F32), pltpu.VMEM((2, 1, tq), F32),
                        pltpu.VMEM((dv, tq), BF16)],
        compiler_params=_params("parallel", "parallel"),
    )(qkz, qkz, v_t, qkz, vec(lq1), vec(lk1), vec(lq2), vec(lk2), vec(subln_g))


FOX_PIECES = 3


def _fox_gate_kernel(f_ref, b_ref, o_ref):
    x = f_ref[...] + b_ref[...]
    lf = jnp.minimum(x, 0.0) - jnp.log1p(jnp.exp(-jnp.abs(x)))
    n = lf.shape[0]
    row = lax.broadcasted_iota(jnp.int32, lf.shape, 0)
    d = 1
    while d < n:
        lf = lf + jnp.where(row >= d, pltpu.roll(lf, d, 0), 0.0)
        d *= 2
    rest = lf * LOG2E
    lane = lax.broadcasted_iota(jnp.int32, lf.shape, 1)
    out = jnp.zeros(lf.shape, F32)
    for p in range(FOX_PIECES):
        piece = rest.astype(BF16).astype(F32)
        rest = rest - piece
        moved = piece if p == 0 else pltpu.roll(piece, p * FOX_HEADS, 1)
        out = jnp.where((lane >= p * FOX_HEADS) & (lane < (p + 1) * FOX_HEADS), moved, out)
    o_ref[...] = out.astype(BF16)


def _fox_gates(f_logit, bias, batch, seq):
    bpad = jnp.zeros((1, LANES), F32).at[0, :FOX_HEADS].set(bias.astype(F32))
    return pl.pallas_call(
        _fox_gate_kernel,
        name="fox_gates",
        out_shape=jax.ShapeDtypeStruct((batch * seq, LANES), BF16),
        grid=(batch,),
        in_specs=[pl.BlockSpec((seq, LANES), lambda b: (b, 0)),
                  pl.BlockSpec((1, LANES), lambda b: (0, 0))],
        out_specs=pl.BlockSpec((seq, LANES), lambda b: (b, 0)),
        compiler_params=_params("parallel"),
    )(f_logit, bpad)


def _fox_place_kernel(f_ref, pk_ref, pq_ref, ck_ref, cq_ref, kx_ref, qx_ref):
    f = f_ref[...]
    kx_ref[...] = (jnp.dot(f, pk_ref[...], preferred_element_type=F32) + ck_ref[...]).astype(BF16)
    qx_ref[...] = (jnp.dot(f, pq_ref[...], preferred_element_type=F32) + cq_ref[...]).astype(BF16)


def _fox_extend(pieces, tm=2048):
    t = pieces.shape[0]
    tm = min(tm, t)
    n = FOX_HEADS * LANES
    src = jnp.arange(LANES)[:, None]
    dst = jnp.arange(n)[None, :]
    p_src, h_src = src // FOX_HEADS, src % FOX_HEADS
    h_dst, c_dst = dst // LANES, dst % LANES
    live = (p_src < FOX_PIECES) & (h_src == h_dst)
    pk = jnp.where(live & (c_dst == p_src), -1.0, 0.0).astype(BF16)
    pq = jnp.where(live & (c_dst == p_src + FOX_PIECES), 1.0, 0.0).astype(BF16)
    ck = ((c_dst >= FOX_PIECES) & (c_dst < 2 * FOX_PIECES)).astype(F32)
    cq = (c_dst < FOX_PIECES).astype(F32)
    mat = pl.BlockSpec((LANES, n), lambda i: (0, 0))
    vec = pl.BlockSpec((1, n), lambda i: (0, 0))
    return pl.pallas_call(
        _fox_place_kernel,
        name="fox_extend",
        out_shape=[jax.ShapeDtypeStruct((t, n), BF16), jax.ShapeDtypeStruct((t, n), BF16)],
        grid=(t // tm,),
        in_specs=[pl.BlockSpec((tm, LANES), lambda i: (i, 0)), mat, mat, vec, vec],
        out_specs=[pl.BlockSpec((tm, n), lambda i: (i, 0)), pl.BlockSpec((tm, n), lambda i: (i, 0))],
        compiler_params=_params("parallel"),
    )(pieces, pk, pq, ck, cq)


def _fox_attn_kernel(q_ref, qx_ref, k_ref, kx_ref, vt_ref, z_ref, o_ref, m_ref, acc_ref,
                     s0_ref, s1_ref, mx0_ref, mx1_ref, qt_ref, vte_ref, *, nq):
    tq = 2 * ATTN_TK
    dh = FOX_HEAD_DIM
    vte_ref[:dh, :] = vt_ref[...]
    vte_ref[dh:, :] = jnp.ones((ONES_ROWS, vte_ref.shape[1]), BF16)

    def begin(qi):
        rows = pl.ds(pl.multiple_of(qi * tq, tq), tq)
        qt_ref[:dh, :] = q_ref[rows, :].T
        qt_ref[dh:, :] = qx_ref[rows, :].T
        _flash_init(m_ref, None, acc_ref)

    def finish(qi):
        rows = pl.ds(pl.multiple_of(qi * tq, tq), tq)
        o = (acc_ref[0, :dh, :] * (1.0 / acc_ref[0, dh:dh + 1, :])).T
        o_ref[rows, :] = (o * _silu(z_ref[rows, :].astype(F32))).astype(o_ref.dtype)

    def scores(ki, buf, cols):
        s_ref, mx_ref = buf
        off = pl.multiple_of(ki * ATTN_TK, ATTN_TK)
        k = jnp.concatenate([k_ref[pl.ds(off, ATTN_TK), :], kx_ref[pl.ds(off, ATTN_TK), :]], axis=1)
        s_t = jnp.dot(k, qt_ref[:, cols], preferred_element_type=F32)
        s_ref[:, cols] = s_t
        mx_ref[:, cols] = jnp.max(s_t, axis=0, keepdims=True)

    def mask(buf, c0):
        _mask_diagonal(buf[0], buf[1], c0, 1)

    def update(ki, buf, cols):
        s_ref, mx_ref = buf
        v_t = vte_ref[:, pl.ds(pl.multiple_of(ki * ATTN_TK, ATTN_TK), ATTN_TK)]
        _flash_step_t(s_ref[:, cols], mx_ref[:, cols], v_t, m_ref, None, acc_ref, 0, cols)

    _causal_blocks(nq, begin, scores, mask, update, finish, (s0_ref, mx0_ref), (s1_ref, mx1_ref))


def _fox_attention(qkz, v_t, kx, qx, batch, seq):
    t = qkz.shape[0]
    tq = 2 * ATTN_TK
    nq = seq // tq
    h, dh = FOX_HEADS, FOX_HEAD_DIM
    return pl.pallas_call(
        functools.partial(_fox_attn_kernel, nq=nq),
        name="fox_attn",
        out_shape=jax.ShapeDtypeStruct((t, h * dh), BF16),
        grid=(batch, h),
        in_specs=[pl.BlockSpec((seq, dh), lambda b, hh: (b, hh)),
                  pl.BlockSpec((seq, LANES), lambda b, hh: (b, hh)),
                  pl.BlockSpec((seq, dh), lambda b, hh: (b, h + hh)),
                  pl.BlockSpec((seq, LANES), lambda b, hh: (b, hh)),
                  pl.BlockSpec((dh, seq), lambda b, hh: (hh, b)),
                  pl.BlockSpec((seq, dh), lambda b, hh: (b, 2 * h + hh))],
        out_specs=pl.BlockSpec((seq, dh), lambda b, hh: (b, hh)),
        scratch_shapes=[pltpu.VMEM((1, 1, tq), F32),
                        pltpu.VMEM((1, dh + ONES_ROWS, tq), F32),
                        pltpu.VMEM((ATTN_TK, tq), F32), pltpu.VMEM((ATTN_TK, tq), F32),
                        pltpu.VMEM((1, tq), F32), pltpu.VMEM((1, tq), F32),
                        pltpu.VMEM((dh + LANES, tq), BF16),
                        pltpu.VMEM((dh + ONES_ROWS, seq), BF16)],
        compiler_params=_params("parallel", "parallel"),
    )(qkz, qx, qkz, kx, v_t, qkz)


BAND_TQ = 256
BAND_WIN = BAND_PAST + BAND_TQ
BAND_HPB = LANES // BAND_HEAD_DIM
BAND_REV = 1024


def _band_bias_kernel(row_ref, o_ref):
    x = jnp.broadcast_to(row_ref[0], (BAND_WIN, BAND_REV))
    x = pltpu.roll(x, BAND_REV - BAND_WIN, 1, stride=1, stride_axis=0)
    x = x[:, :BAND_TQ]
    kc = lax.broadcasted_iota(jnp.int32, x.shape, 0) // CHUNK
    qc = lax.broadcasted_iota(jnp.int32, x.shape, 1) // CHUNK
    band = (kc >= qc) & (kc <= qc + BAND_PAST // CHUNK)
    o_ref[0] = jnp.where(band, x * LOG2E, NEG_INF)


def _band_bias(rel_table):
    nh = rel_table.shape[0]
    t = rel_table.astype(F32)
    row = jnp.concatenate([t, jnp.broadcast_to(t[:, 2 * REL_CLIP:], (nh, BAND_REV - t.shape[1]))], axis=1)
    return pl.pallas_call(
        _band_bias_kernel,
        name="band_bias",
        out_shape=jax.ShapeDtypeStruct((nh, BAND_WIN, BAND_TQ), F32),
        grid=(nh,),
        in_specs=[pl.BlockSpec((1, 1, BAND_REV), lambda h: (h, 0, 0))],
        out_specs=pl.BlockSpec((1, BAND_WIN, BAND_TQ), lambda h: (h, 0, 0)),
        compiler_params=_params("parallel"),
    )(row.reshape(nh, 1, BAND_REV))


def _band_attn_kernel(q_ref, k_ref, vt_ref, z_ref, bias_ref, o_ref, kp_ref, vtp_ref, s0_ref, s1_ref,
                      mx0_ref, mx1_ref, *, nq):
    hd = BAND_HEAD_DIM
    for hh in range(BAND_HPB):
        kp_ref[hh, :BAND_PAST, :] = jnp.zeros((BAND_PAST, hd), BF16)
        kp_ref[hh, BAND_PAST:, :] = k_ref[:, hh * hd:(hh + 1) * hd]
        vtp_ref[hh, :hd, :BAND_PAST] = jnp.zeros((hd, BAND_PAST), BF16)
        vtp_ref[hh, :hd, BAND_PAST:] = vt_ref[hh * hd:(hh + 1) * hd, :]
        vtp_ref[hh, hd:, :] = jnp.ones((ONES_ROWS, vtp_ref.shape[2]), BF16)

    def scores(j, buf):
        s_ref, mx_ref = buf
        start = pl.multiple_of(j * BAND_TQ, BAND_TQ)
        q_t = q_ref[pl.ds(start, BAND_TQ), :].T
        for hh in range(BAND_HPB):
            s_t = jnp.dot(kp_ref[hh, pl.ds(start, BAND_WIN), :], q_t[hh * hd:(hh + 1) * hd, :],
                          preferred_element_type=F32) + bias_ref[hh]
            s_ref[hh] = s_t
            mx_ref[hh] = jnp.max(s_t, axis=0, keepdims=True)

    def update(j, buf, masked):
        s_ref, mx_ref = buf
        start = pl.multiple_of(j * BAND_TQ, BAND_TQ)
        outs = []
        for hh in range(BAND_HPB):
            s_t = s_ref[hh]
            s_max = mx_ref[hh]
            if masked:
                kpos = start - BAND_PAST + lax.broadcasted_iota(jnp.int32, (BAND_WIN, 1), 0)
                s_t = jnp.where(kpos >= 0, s_t, NEG_INF)
                s_max = jnp.max(s_t, axis=0, keepdims=True)
            p_t = jnp.exp2(s_t - s_max).astype(BF16)
            ol = jnp.dot(vtp_ref[hh, :, pl.ds(start, BAND_WIN)], p_t, preferred_element_type=F32)
            outs.append(ol[:hd] * (1.0 / ol[hd:hd + 1]))
        o = jnp.concatenate(outs, axis=0).T
        z = z_ref[pl.ds(start, BAND_TQ), :].astype(F32)
        o_ref[pl.ds(start, BAND_TQ), :] = (o * _silu(z)).astype(o_ref.dtype)

    buf0, buf1 = (s0_ref, mx0_ref), (s1_ref, mx1_ref)
    scores(0, buf0)
    scores(1, buf1)
    update(0, buf0, True)
    scores(2, buf0)
    update(1, buf1, True)

    def pairs(j, n):
        for _ in range(n):
            scores(j + 1, buf1)
            update(j, buf0, False)
            scores(j + 2, buf0)
            update(j + 1, buf1, False)
            j = j + 2

    def quad(jj, carry):
        pairs(2 + 4 * jj, 2)
        return carry

    n_pairs = (nq - 4) // 2
    lax.fori_loop(0, n_pairs // 2, quad, 0)
    pairs(2 + 4 * (n_pairs // 2), n_pairs % 2)
    scores(nq - 1, buf1)
    update(nq - 2, buf0, False)
    update(nq - 1, buf1, False)


def _band_attention(qkz, v_t, bias_t, batch, seq):
    t = qkz.shape[0]
    nq = seq // BAND_TQ
    assert nq >= 4 and nq % 2 == 0
    hp = BAND_HEADS // BAND_HPB
    return pl.pallas_call(
        functools.partial(_band_attn_kernel, nq=nq),
        name="band_attn",
        out_shape=jax.ShapeDtypeStruct((t, BAND_HEADS * BAND_HEAD_DIM), BF16),
        grid=(batch, hp),
        in_specs=[pl.BlockSpec((seq, LANES), lambda b, g: (b, g)),
                  pl.BlockSpec((seq, LANES), lambda b, g: (b, hp + g)),
                  pl.BlockSpec((LANES, seq), lambda b, g: (g, b)),
                  pl.BlockSpec((seq, LANES), lambda b, g: (b, 2 * hp + g)),
                  pl.BlockSpec((BAND_HPB, BAND_WIN, BAND_TQ), lambda b, g: (g, 0, 0))],
        out_specs=pl.BlockSpec((seq, LANES), lambda b, g: (b, g)),
        scratch_shapes=[pltpu.VMEM((BAND_HPB, seq + BAND_PAST, BAND_HEAD_DIM), BF16),
                        pltpu.VMEM((BAND_HPB, BAND_HEAD_DIM + ONES_ROWS, seq + BAND_PAST), BF16),
                        pltpu.VMEM((BAND_HPB, BAND_WIN, BAND_TQ), F32),
                        pltpu.VMEM((BAND_HPB, BAND_WIN, BAND_TQ), F32),
                        pltpu.VMEM((BAND_HPB, 1, BAND_TQ), F32),
                        pltpu.VMEM((BAND_HPB, 1, BAND_TQ), F32)],
        compiler_params=_params("parallel", "parallel"),
    )(qkz, qkz, v_t, qkz, bias_t)


def _pool_kernel(u_ref, up_ref, z_ref, w_ref, ps_ref, o_ref, *, tm, nb):
    i = pl.program_id(0)
    t0 = (i % nb) * tm
    has_history = t0 > 0
    t = t0 + lax.broadcasted_iota(jnp.int32, (tm, 1), 0)
    for g, win in enumerate(POOL_WINDOWS):
        sl = slice(g * POOL_GROUP, (g + 1) * POOL_GROUP)
        u = u_ref[:, sl]
        acc = jnp.concatenate([jnp.where(has_history, up_ref[:, sl], 0.0), u], axis=0)
        d = 1
        while d < win:
            acc = acc + pltpu.roll(acc, d, 0)
            d *= 2
        wsum = acc[POOL_HALO:, :]
        inv_cnt = 1.0 / jnp.minimum(t + 1, win).astype(F32)
        delta = (wsum * inv_cnt - u).astype(BF16)
        y = jnp.dot(delta, w_ref[g], preferred_element_type=F32) * ps_ref[:, sl]
        o_ref[:, sl] = (y * _silu(z_ref[:, sl])).astype(o_ref.dtype)


def _pool_mix(uz, seq, pool_w, pool_scale, tm=1024):
    t = uz.shape[0]
    d = D_MODEL
    tm = min(tm, seq)
    nb = seq // tm
    hb = tm // POOL_HALO
    return pl.pallas_call(
        functools.partial(_pool_kernel, tm=tm, nb=nb),
        name="pool_mix",
        out_shape=jax.ShapeDtypeStruct((t, d), BF16),
        grid=(t // tm,),
        in_specs=[pl.BlockSpec((tm, d), lambda i: (i, 0)),
                  pl.BlockSpec((POOL_HALO, d), lambda i: (jnp.maximum(i * hb - 1, 0), 0)),
                  pl.BlockSpec((tm, d), lambda i: (i, 1)),
                  pl.BlockSpec((len(POOL_WINDOWS), POOL_GROUP, POOL_GROUP), lambda i: (0, 0, 0)),
                  pl.BlockSpec((1, d), lambda i: (0, 0))],
        out_specs=pl.BlockSpec((tm, d), lambda i: (i, 0)),
        compiler_params=_params("parallel"),
    )(uz, uz, uz, pool_w, pool_scale.reshape(1, d).astype(F32))


QKZ_TN = 2048
POOL_TN = 2048


def _transpose_cast_kernel(x_ref, o_ref):
    o_ref[...] = x_ref[...].T.astype(o_ref.dtype)


def _bf16_weights(w_in, branch, tile=512):
    d = w_in.shape[0]
    first = 2 * branch // tile
    w = w_in.astype(BF16)
    w_vt = pl.pallas_call(
        _transpose_cast_kernel,
        name="transpose_cast",
        out_shape=jax.ShapeDtypeStruct((branch, d), BF16),
        grid=(d // tile, branch // tile),
        in_specs=[pl.BlockSpec((tile, tile), lambda i, j: (i, first + j))],
        out_specs=pl.BlockSpec((tile, tile), lambda i, j: (j, i)),
        compiler_params=_params("parallel", "parallel"),
    )(w)
    return w, w_vt


def _qkz_scale(qk_dim, branch):
    return jnp.concatenate([jnp.full((branch,), qk_dim ** -0.5 * LOG2E, F32), jnp.ones((2 * branch,), F32)])


def kernel(x, c, ada_w, ada_b, norm_pre, norm_post, diff_w_in, diff_w_out, diff_lambda_q1, diff_lambda_k1, diff_lambda_q2, diff_lambda_k2, diff_subln, band_w_in, band_w_out, band_rel_bias, fox_w_in, fox_w_out, fox_forget_bias, pool_w_in, pool_w_out, pool_group_w, pool_scale):
    batch, seq, d = x.shape
    depth = ada_w.shape[0]
    branch = d
    nblk = branch // QKZ_TN
    qkz_blocks = list(range(2 * nblk)) + list(range(3 * nblk, 4 * nblk))
    mod = _ada_mod(c, ada_w, ada_b)
    x2 = x.reshape(batch * seq, d)
    for i in range(depth):
        mod3 = mod[i].reshape(mod.shape[1], 1, 3 * d)
        kind = i % 4
        if kind == 0:
            lambda_init = 0.8 - 0.6 * math.exp(-0.3 * i)
            w, w_vt = _bf16_weights(diff_w_in, branch)
            qkz, h = _norm_proj(x2, seq, norm_pre[i], mod3, w, qkz_blocks,
                                _qkz_scale(DIFF_QK_DIM, branch), emit_h=True, tn=QKZ_TN)
            v_t = _proj_t(w_vt, h)
            o = _diff_attention(qkz, v_t, batch, seq, diff_lambda_q1, diff_lambda_k1, diff_lambda_q2,
                                diff_lambda_k2, diff_subln, lambda_init)
            w_out = diff_w_out
        elif kind == 1:
            w, w_vt = _bf16_weights(band_w_in, branch)
            qkz, h = _norm_proj(x2, seq, norm_pre[i], mod3, w, qkz_blocks,
                                _qkz_scale(BAND_HEAD_DIM, branch), emit_h=True, tn=QKZ_TN)
            v_t = _proj_t(w_vt, h)
            o = _band_attention(qkz, v_t, _band_bias(band_rel_bias), batch, seq)
            w_out = band_w_out
        elif kind == 2:
            w, w_vt = _bf16_weights(fox_w_in, branch)
            wf = jnp.pad(w[:, 4 * branch:], ((0, 0), (0, LANES - FOX_HEADS)))
            qkz, h, f_logit = _norm_proj(x2, seq, norm_pre[i], mod3, w, qkz_blocks,
                                         _qkz_scale(FOX_HEAD_DIM, branch), wf=wf, emit_h=True, tn=QKZ_TN)
            v_t = _proj_t(w_vt, h)
            kx, qx = _fox_extend(_fox_gates(f_logit, fox_forget_bias, batch, seq))
            o = _fox_attention(qkz, v_t, kx, qx, batch, seq)
            w_out = fox_w_out
        else:
            uz, = _norm_proj(x2, seq, norm_pre[i], mod3, pool_w_in.astype(BF16),
                             list(range(2 * branch // POOL_TN)), jnp.ones((2 * branch,), F32),
                             out_dtype=F32, tn=POOL_TN)
            o = _pool_mix(uz, seq, pool_group_w.astype(BF16), pool_scale)
            w_out = pool_w_out
        x2 = _out_proj_residual(o, w_out.astype(BF16), x2, seq, mod3, norm_post[i])
    return x2.reshape(batch, seq, d)
```

```python
import functools
import math

import jax
import jax.numpy as jnp
from jax import lax
from jax.experimental import pallas as pl
from jax.experimental.pallas import tpu as pltpu

F32 = jnp.float32
BF16 = jnp.bfloat16

D_MODEL = 2048
EPS = 1e-6
NEG_INF = -1e30
LOG2E = math.log2(math.e)
CHUNK = 64
DIFF_HEADS, DIFF_V_DIM, DIFF_QK_DIM = 8, 256, 128
BAND_HEADS, BAND_HEAD_DIM, BAND_PAST, REL_CLIP = 32, 64, 512, 256
FOX_HEADS, FOX_HEAD_DIM = 16, 128
POOL_WINDOWS = (2, 4, 8, 16)
POOL_GROUP = D_MODEL // len(POOL_WINDOWS)
POOL_HALO = 16

LANES = 128
VMEM_LIMIT = 58 * 1024 * 1024

NT_DIMS = (((1,), (1,)), ((), ()))


def _silu(x):
    return x / (1.0 + jnp.exp(-x))


def _params(*sem):
    return pltpu.CompilerParams(dimension_semantics=sem, vmem_limit_bytes=VMEM_LIMIT)


def _ada_kernel(c_ref, w_ref, b_ref, o_ref):
    ca = _silu(c_ref[...]).astype(BF16)
    o_ref[0] = jnp.dot(ca, w_ref[0].astype(BF16), preferred_element_type=F32) + b_ref[0]


def _ada_mod(c, ada_w, ada_b):
    b, d = c.shape
    bp = -(-b // 8) * 8
    depth, _, n3 = ada_w.shape
    tn = 768
    cp = jnp.pad(c, ((0, bp - b), (0, 0)))
    return pl.pallas_call(
        _ada_kernel,
        name="ada_mod",
        out_shape=jax.ShapeDtypeStruct((depth, bp, n3), F32),
        grid=(depth, n3 // tn),
        in_specs=[pl.BlockSpec((bp, d), lambda l, j: (0, 0)),
                  pl.BlockSpec((1, d, tn), lambda l, j: (l, 0, j)),
                  pl.BlockSpec((1, 1, tn), lambda l, j: (l, 0, j))],
        out_specs=pl.BlockSpec((1, bp, tn), lambda l, j: (l, 0, j)),
        compiler_params=_params("parallel", "parallel"),
    )(cp, ada_w, ada_b.reshape(depth, 1, n3))


def _norm_proj_kernel(x_ref, g_ref, sh_ref, sc_ref, w_ref, cs_ref, *rest, has_f, emit_h, sub):
    rest = list(rest)
    wf_ref = rest.pop(0) if has_f else None
    o_ref = rest.pop(0)
    hout_ref = rest.pop(0) if emit_h else None
    f_ref = rest.pop(0) if has_f else None
    h_ref = rest.pop(0)

    @pl.when(pl.program_id(1) == 0)
    def _():
        g = g_ref[...]
        sc = 1.0 + sc_ref[0]
        sh = sh_ref[0]
        for r in range(x_ref.shape[0] // sub):
            rows = slice(r * sub, (r + 1) * sub)
            x = x_ref[rows, :]
            y = x * lax.rsqrt(jnp.mean(x * x, axis=-1, keepdims=True) + EPS) * g
            h = (y * sc + sh).astype(BF16)
            h_ref[rows, :] = h
            if emit_h:
                hout_ref[rows, :] = h
            if has_f:
                f_ref[rows, :] = jnp.dot(h, wf_ref[...], preferred_element_type=F32)
            y = jnp.dot(h, w_ref[...], preferred_element_type=F32)
            o_ref[rows, :] = (y * cs_ref[...]).astype(o_ref.dtype)

    @pl.when(pl.program_id(1) > 0)
    def _():
        y = jnp.dot(h_ref[...], w_ref[...], preferred_element_type=F32)
        o_ref[...] = (y * cs_ref[...]).astype(o_ref.dtype)


def _norm_proj(x2, seq, g, mod3, w, col_blocks, col_scale, wf=None, emit_h=False, out_dtype=BF16,
               tm=1024, tn=1024):
    t, d = x2.shape
    tm = min(tm, seq)
    nb = seq // tm
    nj = len(col_blocks)
    n = nj * tn
    first, skip_from, skip = col_blocks[0], None, 0
    for a, b2 in zip(col_blocks, col_blocks[1:]):
        if b2 != a + 1:
            skip_from, skip = a - first + 1, b2 - a - 1
    if skip_from is None:
        wmap = lambda i, j: (0, j + first)
    else:
        wmap = lambda i, j: (0, j + first + jnp.where(j >= skip_from, skip, 0))
    in_specs = [pl.BlockSpec((tm, d), lambda i, j: (i, 0)),
                pl.BlockSpec((1, d), lambda i, j: (0, 0)),
                pl.BlockSpec((1, 1, d), lambda i, j: (i // nb, 0, 0)),
                pl.BlockSpec((1, 1, d), lambda i, j: (i // nb, 0, 1)),
                pl.BlockSpec((d, tn), wmap),
                pl.BlockSpec((1, tn), lambda i, j: (0, j))]
    out_shape = [jax.ShapeDtypeStruct((t, n), out_dtype)]
    out_specs = [pl.BlockSpec((tm, tn), lambda i, j: (i, j))]
    args = [x2, g.reshape(1, d), mod3, mod3, w, col_scale.reshape(1, n).astype(F32)]
    if wf is not None:
        in_specs.append(pl.BlockSpec((d, LANES), lambda i, j: (0, 0)))
        args.append(wf)
    if emit_h:
        out_shape.append(jax.ShapeDtypeStruct((t, d), BF16))
        out_specs.append(pl.BlockSpec((tm, d), lambda i, j: (i, 0)))
    if wf is not None:
        out_shape.append(jax.ShapeDtypeStruct((t, LANES), F32))
        out_specs.append(pl.BlockSpec((tm, LANES), lambda i, j: (i, 0)))
    return pl.pallas_call(
        functools.partial(_norm_proj_kernel, has_f=wf is not None, emit_h=emit_h, sub=min(256, tm)),
        name="norm_proj",
        out_shape=out_shape,
        grid=(t // tm, nj),
        in_specs=in_specs,
        out_specs=out_specs,
        scratch_shapes=[pltpu.VMEM((tm, d), BF16)],
        compiler_params=_params("parallel", "arbitrary"),
    )(*args)


def _proj_t_kernel(w_ref, h_ref, o_ref):
    o_ref[...] = lax.dot_general(w_ref[...], h_ref[...], NT_DIMS,
                                 preferred_element_type=F32).astype(o_ref.dtype)


def _proj_t(w_t, h, tn=2048, tm=2048):
    n, d = w_t.shape
    t = h.shape[0]
    tm = min(tm, t)
    return pl.pallas_call(
        _proj_t_kernel,
        name="proj_t",
        out_shape=jax.ShapeDtypeStruct((n, t), BF16),
        grid=(t // tm, n // tn),
        in_specs=[pl.BlockSpec((tn, d), lambda i, j: (j, 0)),
                  pl.BlockSpec((tm, d), lambda i, j: (i, 0))],
        out_specs=pl.BlockSpec((tn, tm), lambda i, j: (j, i)),
        compiler_params=_params("parallel", "arbitrary"),
    )(w_t, h)


def _out_proj_kernel(o_ref, w_ref, x_ref, gate_ref, g_ref, out_ref, *, sub):
    for r in range(o_ref.shape[0] // sub):
        rows = slice(r * sub, (r + 1) * sub)
        y = jnp.dot(o_ref[rows, :], w_ref[...], preferred_element_type=F32)
        yn = y * lax.rsqrt(jnp.mean(y * y, axis=-1, keepdims=True) + EPS) * g_ref[...]
        out_ref[rows, :] = x_ref[rows, :] + gate_ref[0] * yn


def _out_proj_residual(o2, w, x2, seq, mod3, g, tm=1024):
    t, d = x2.shape
    tm = min(tm, seq)
    nb = seq // tm
    return pl.pallas_call(
        functools.partial(_out_proj_kernel, sub=min(256, tm)),
        name="out_proj",
        out_shape=jax.ShapeDtypeStruct((t, d), F32),
        grid=(t // tm,),
        in_specs=[pl.BlockSpec((tm, d), lambda i: (i, 0)),
                  pl.BlockSpec((d, d), lambda i: (0, 0), pipeline_mode=pl.Buffered(1)),
                  pl.BlockSpec((tm, d), lambda i: (i, 0)),
                  pl.BlockSpec((1, 1, d), lambda i: (i // nb, 0, 2)),
                  pl.BlockSpec((1, d), lambda i: (0, 0))],
        out_specs=pl.BlockSpec((tm, d), lambda i: (i, 0)),
        compiler_params=_params("parallel"),
    )(o2, w, x2, mod3, g.reshape(1, d))


def _flash_step_t(s_t, s_max, v_t, m_ref, l_ref, acc_ref, idx, cols):
    m_prev = m_ref[idx, :, cols]
    m_new = jnp.maximum(m_prev, s_max)
    alpha = jnp.exp2(m_prev - m_new)
    p_t = jnp.exp2(s_t - m_new)
    if l_ref is not None:
        l_ref[idx, :, cols] = alpha * l_ref[idx, :, cols] + jnp.sum(p_t, axis=0, keepdims=True)
    acc_ref[idx, :, cols] = (alpha * acc_ref[idx, :, cols]
                             + jnp.dot(v_t, p_t.astype(BF16), preferred_element_type=F32))
    m_ref[idx, :, cols] = m_new


def _flash_init(m_ref, l_ref, acc_ref):
    m_ref[...] = jnp.full(m_ref.shape, NEG_INF, F32)
    if l_ref is not None:
        l_ref[...] = jnp.zeros(l_ref.shape, F32)
    acc_ref[...] = jnp.zeros(acc_ref.shape, F32)


ONES_ROWS = 16


ATTN_TK = 512


def _mask_diagonal(s_ref, mx_ref, c0, granule):
    t = LANES
    kg = lax.broadcasted_iota(jnp.int32, (t, t), 0) // granule
    qg = lax.broadcasted_iota(jnp.int32, (t, t), 1) // granule
    visible = kg <= qg
    for r in range(ATTN_TK // t):
        rows = slice(r * t, (r + 1) * t)
        if r:
            s_ref[rows, c0:c0 + r * t] = jnp.full((t, r * t), NEG_INF, F32)
        diag = slice(c0 + r * t, c0 + (r + 1) * t)
        s_ref[rows, diag] = jnp.where(visible, s_ref[rows, diag], NEG_INF)
    cols = slice(c0, c0 + ATTN_TK)
    mx_ref[:, cols] = jnp.max(s_ref[:, cols], axis=0, keepdims=True)


def _causal_blocks(nq, begin, scores, mask, update, finish, buf0, buf1):
    every, late = slice(None), slice(ATTN_TK, 2 * ATTN_TK)
    begin(0)
    scores(0, buf0, every)

    def pairs(j, n):
        for _ in range(n):
            scores(j + 1, buf1, every)
            update(j, buf0, every)
            scores(j + 2, buf0, every)
            update(j + 1, buf1, every)
            j = j + 2

    def qblock(qi, carry):
        def quad(jj, c):
            pairs(4 * jj, 2)
            return c

        lax.fori_loop(0, qi // 2, quad, 0)

        @pl.when(qi % 2 == 1)
        def _():
            pairs(2 * qi - 2, 1)

        scores(2 * qi + 1, buf1, late)
        mask(buf0, 0)
        update(2 * qi, buf0, every)
        mask(buf1, ATTN_TK)
        update(2 * qi + 1, buf1, late)
        finish(qi)
        begin(jnp.minimum(qi + 1, nq - 1))
        scores(0, buf0, every)
        return carry

    lax.fori_loop(0, nq, qblock, 0)


def _diff_attn_kernel(q_ref, k_ref, vt_ref, z_ref, lq1_ref, lk1_ref, lq2_ref, lk2_ref, g_ref,
                      o_ref, m_ref, l_ref, acc_ref, s0_ref, s1_ref, mx0_ref, mx1_ref, qc_ref,
                      *, nq, lambda_init):
    tq = 2 * ATTN_TK
    lam = (jnp.exp(jnp.sum(lq1_ref[...] * lk1_ref[...], axis=-1, keepdims=True))
           - jnp.exp(jnp.sum(lq2_ref[...] * lk2_ref[...], axis=-1, keepdims=True)) + lambda_init)
    g = g_ref[...] * (1.0 - lambda_init)

    def begin(qi):
        qc_ref[...] = q_ref[pl.ds(pl.multiple_of(qi * tq, tq), tq), :].T
        _flash_init(m_ref, l_ref, acc_ref)

    def finish(qi):
        rows = pl.ds(pl.multiple_of(qi * tq, tq), tq)
        o_t = acc_ref[0] * (1.0 / l_ref[0]) - acc_ref[1] * (lam / l_ref[1])
        o = o_t.T
        on = o * lax.rsqrt(jnp.mean(o * o, axis=-1, keepdims=True) + EPS) * g
        o_ref[rows, :] = (on * _silu(z_ref[rows, :].astype(F32))).astype(o_ref.dtype)

    def scores(ki, buf, cols):
        s_ref, mx_ref = buf
        k = k_ref[pl.ds(pl.multiple_of(ki * ATTN_TK, ATTN_TK), ATTN_TK), :]
        for m in range(2):
            sl = slice(m * DIFF_QK_DIM, (m + 1) * DIFF_QK_DIM)
            s_t = jnp.dot(k[:, sl], qc_ref[sl, cols], preferred_element_type=F32)
            s_ref[m, :, cols] = s_t
            mx_ref[m, :, cols] = jnp.max(s_t, axis=0, keepdims=True)

    def mask(buf, c0):
        s_ref, mx_ref = buf
        for m in range(2):
            _mask_diagonal(s_ref.at[m], mx_ref.at[m], c0, CHUNK)

    def update(ki, buf, cols):
        s_ref, mx_ref = buf
        v_t = vt_ref[:, pl.ds(pl.multiple_of(ki * ATTN_TK, ATTN_TK), ATTN_TK)]
        for m in range(2):
            _flash_step_t(s_ref[m, :, cols], mx_ref[m, :, cols], v_t, m_ref, l_ref, acc_ref, m, cols)

    _causal_blocks(nq, begin, scores, mask, update, finish, (s0_ref, mx0_ref), (s1_ref, mx1_ref))


def _diff_attention(qkz, v_t, batch, seq, lq1, lk1, lq2, lk2, subln_g, lambda_init):
    t = qkz.shape[0]
    tq = 2 * ATTN_TK
    nq = seq // tq
    h, dv = DIFF_HEADS, DIFF_V_DIM
    vec = lambda a: a.reshape(1, -1).astype(F32)
    small = lambda n: pl.BlockSpec((1, n), lambda b, hh: (0, 0))
    return pl.pallas_call(
        functools.partial(_diff_attn_kernel, nq=nq, lambda_init=lambda_init),
        name="diff_attn",
        out_shape=jax.ShapeDtypeStruct((t, h * dv), BF16),
        grid=(batch, h),
        in_specs=[pl.BlockSpec((seq, dv), lambda b, hh: (b, hh)),
                  pl.BlockSpec((seq, dv), lambda b, hh: (b, h + hh)),
                  pl.BlockSpec((dv, seq), lambda b, hh: (hh, b)),
                  pl.BlockSpec((seq, dv), lambda b, hh: (b, 2 * h + hh)),
                  small(DIFF_QK_DIM), small(DIFF_QK_DIM), small(DIFF_QK_DIM), small(DIFF_QK_DIM),
                  small(dv)],
        out_specs=pl.BlockSpec((seq, dv), lambda b, hh: (b, hh)),
        scratch_shapes=[pltpu.VMEM((2, 1, tq), F32), pltpu.VMEM((2, 1, tq), F32),
                        pltpu.VMEM((2, dv, tq), F32),
                        pltpu.VMEM((2, ATTN_TK, tq), F32), pltpu.VMEM((2, ATTN_TK, tq), F32),
                        pltpu.VMEM((2, 1, tq), F32), pltpu.VMEM((2, 1, tq), F32),
                        pltpu.VMEM((dv, tq), BF16)],
        compiler_params=_params("parallel", "parallel"),
    )(qkz, qkz, v_t, qkz, vec(lq1), vec(lk1), vec(lq2), vec(lk2), vec(subln_g))


FOX_PIECES = 3


def _fox_gate_kernel(f_ref, b_ref, o_ref):
    x = f_ref[...] + b_ref[...]
    lf = jnp.minimum(x, 0.0) - jnp.log1p(jnp.exp(-jnp.abs(x)))
    n = lf.shape[0]
    row = lax.broadcasted_iota(jnp.int32, lf.shape, 0)
    d = 1
    while d < n:
        lf = lf + jnp.where(row >= d, pltpu.roll(lf, d, 0), 0.0)
        d *= 2
    rest = lf * LOG2E
    lane = lax.broadcasted_iota(jnp.int32, lf.shape, 1)
    out = jnp.zeros(lf.shape, F32)
    for p in range(FOX_PIECES):
        piece = rest.astype(BF16).astype(F32)
        rest = rest - piece
        moved = piece if p == 0 else pltpu.roll(piece, p * FOX_HEADS, 1)
        out = jnp.where((lane >= p * FOX_HEADS) & (lane < (p + 1) * FOX_HEADS), moved, out)
    o_ref[...] = out.astype(BF16)


def _fox_gates(f_logit, bias, batch, seq):
    bpad = jnp.zeros((1, LANES), F32).at[0, :FOX_HEADS].set(bias.astype(F32))
    return pl.pallas_call(
        _fox_gate_kernel,
        name="fox_gates",
        out_shape=jax.ShapeDtypeStruct((batch * seq, LANES), BF16),
        grid=(batch,),
        in_specs=[pl.BlockSpec((seq, LANES), lambda b: (b, 0)),
                  pl.BlockSpec((1, LANES), lambda b: (0, 0))],
        out_specs=pl.BlockSpec((seq, LANES), lambda b: (b, 0)),
        compiler_params=_params("parallel"),
    )(f_logit, bpad)


def _fox_place_kernel(f_ref, pk_ref, pq_ref, ck_ref, cq_ref, kx_ref, qx_ref):
    f = f_ref[...]
    kx_ref[...] = (jnp.dot(f, pk_ref[...], preferred_element_type=F32) + ck_ref[...]).astype(BF16)
    qx_ref[...] = (jnp.dot(f, pq_ref[...], preferred_element_type=F32) + cq_ref[...]).astype(BF16)


def _fox_extend(pieces, tm=2048):
    t = pieces.shape[0]
    tm = min(tm, t)
    n = FOX_HEADS * LANES
    src = jnp.arange(LANES)[:, None]
    dst = jnp.arange(n)[None, :]
    p_src, h_src = src // FOX_HEADS, src % FOX_HEADS
    h_dst, c_dst = dst // LANES, dst % LANES
    live = (p_src < FOX_PIECES) & (h_src == h_dst)
    pk = jnp.where(live & (c_dst == p_src), -1.0, 0.0).astype(BF16)
    pq = jnp.where(live & (c_dst == p_src + FOX_PIECES), 1.0, 0.0).astype(BF16)
    ck = ((c_dst >= FOX_PIECES) & (c_dst < 2 * FOX_PIECES)).astype(F32)
    cq = (c_dst < FOX_PIECES).astype(F32)
    mat = pl.BlockSpec((LANES, n), lambda i: (0, 0))
    vec = pl.BlockSpec((1, n), lambda i: (0, 0))
    return pl.pallas_call(
        _fox_place_kernel,
        name="fox_extend",
        out_shape=[jax.ShapeDtypeStruct((t, n), BF16), jax.ShapeDtypeStruct((t, n), BF16)],
        grid=(t // tm,),
        in_specs=[pl.BlockSpec((tm, LANES), lambda i: (i, 0)), mat, mat, vec, vec],
        out_specs=[pl.BlockSpec((tm, n), lambda i: (i, 0)), pl.BlockSpec((tm, n), lambda i: (i, 0))],
        compiler_params=_params("parallel"),
    )(pieces, pk, pq, ck, cq)


def _fox_attn_kernel(q_ref, qx_ref, k_ref, kx_ref, vt_ref, z_ref, o_ref, m_ref, acc_ref,
                     s0_ref, s1_ref, mx0_ref, mx1_ref, qt_ref, vte_ref, *, nq):
    tq = 2 * ATTN_TK
    dh = FOX_HEAD_DIM
    vte_ref[:dh, :] = vt_ref[...]
    vte_ref[dh:, :] = jnp.ones((ONES_ROWS, vte_ref.shape[1]), BF16)

    def begin(qi):
        rows = pl.ds(pl.multiple_of(qi * tq, tq), tq)
        qt_ref[:dh, :] = q_ref[rows, :].T
        qt_ref[dh:, :] = qx_ref[rows, :].T
        _flash_init(m_ref, None, acc_ref)

    def finish(qi):
        rows = pl.ds(pl.multiple_of(qi * tq, tq), tq)
        o = (acc_ref[0, :dh, :] * (1.0 / acc_ref[0, dh:dh + 1, :])).T
        o_ref[rows, :] = (o * _silu(z_ref[rows, :].astype(F32))).astype(o_ref.dtype)

    def scores(ki, buf, cols):
        s_ref, mx_ref = buf
        off = pl.multiple_of(ki * ATTN_TK, ATTN_TK)
        k = jnp.concatenate([k_ref[pl.ds(off, ATTN_TK), :], kx_ref[pl.ds(off, ATTN_TK), :]], axis=1)
        s_t = jnp.dot(k, qt_ref[:, cols], preferred_element_type=F32)
        s_ref[:, cols] = s_t
        mx_ref[:, cols] = jnp.max(s_t, axis=0, keepdims=True)

    def mask(buf, c0):
        _mask_diagonal(buf[0], buf[1], c0, 1)

    def update(ki, buf, cols):
        s_ref, mx_ref = buf
        v_t = vte_ref[:, pl.ds(pl.multiple_of(ki * ATTN_TK, ATTN_TK), ATTN_TK)]
        _flash_step_t(s_ref[:, cols], mx_ref[:, cols], v_t, m_ref, None, acc_ref, 0, cols)

    _causal_blocks(nq, begin, scores, mask, update, finish, (s0_ref, mx0_ref), (s1_ref, mx1_ref))


def _fox_attention(qkz, v_t, kx, qx, batch, seq):
    t = qkz.shape[0]
    tq = 2 * ATTN_TK
    nq = seq // tq
    h, dh = FOX_HEADS, FOX_HEAD_DIM
    return pl.pallas_call(
        functools.partial(_fox_attn_kernel, nq=nq),
        name="fox_attn",
        out_shape=jax.ShapeDtypeStruct((t, h * dh), BF16),
        grid=(batch, h),
        in_specs=[pl.BlockSpec((seq, dh), lambda b, hh: (b, hh)),
                  pl.BlockSpec((seq, LANES), lambda b, hh: (b, hh)),
                  pl.BlockSpec((seq, dh), lambda b, hh: (b, h + hh)),
                  pl.BlockSpec((seq, LANES), lambda b, hh: (b, hh)),
                  pl.BlockSpec((dh, seq), lambda b, hh: (hh, b)),
                  pl.BlockSpec((seq, dh), lambda b, hh: (b, 2 * h + hh))],
        out_specs=pl.BlockSpec((seq, dh), lambda b, hh: (b, hh)),
        scratch_shapes=[pltpu.VMEM((1, 1, tq), F32),
                        pltpu.VMEM((1, dh + ONES_ROWS, tq), F32),
                        pltpu.VMEM((ATTN_TK, tq), F32), pltpu.VMEM((ATTN_TK, tq), F32),
                        pltpu.VMEM((1, tq), F32), pltpu.VMEM((1, tq), F32),
                        pltpu.VMEM((dh + LANES, tq), BF16),
                        pltpu.VMEM((dh + ONES_ROWS, seq), BF16)],
        compiler_params=_params("parallel", "parallel"),
    )(qkz, qx, qkz, kx, v_t, qkz)


BAND_TQ = 256
BAND_WIN = BAND_PAST + BAND_TQ
BAND_HPB = LANES // BAND_HEAD_DIM
BAND_REV = 1024


def _band_bias_kernel(row_ref, o_ref):
    x = jnp.broadcast_to(row_ref[0], (BAND_WIN, BAND_REV))
    x = pltpu.roll(x, BAND_REV - BAND_WIN, 1, stride=1, stride_axis=0)
    x = x[:, :BAND_TQ]
    kc = lax.broadcasted_iota(jnp.int32, x.shape, 0) // CHUNK
    qc = lax.broadcasted_iota(jnp.int32, x.shape, 1) // CHUNK
    band = (kc >= qc) & (kc <= qc + BAND_PAST // CHUNK)
    o_ref[0] = jnp.where(band, x * LOG2E, NEG_INF)


def _band_bias(rel_table):
    nh = rel_table.shape[0]
    t = rel_table.astype(F32)
    row = jnp.concatenate([t, jnp.broadcast_to(t[:, 2 * REL_CLIP:], (nh, BAND_REV - t.shape[1]))], axis=1)
    return pl.pallas_call(
        _band_bias_kernel,
        name="band_bias",
        out_shape=jax.ShapeDtypeStruct((nh, BAND_WIN, BAND_TQ), F32),
        grid=(nh,),
        in_specs=[pl.BlockSpec((1, 1, BAND_REV), lambda h: (h, 0, 0))],
        out_specs=pl.BlockSpec((1, BAND_WIN, BAND_TQ), lambda h: (h, 0, 0)),
        compiler_params=_params("parallel"),
    )(row.reshape(nh, 1, BAND_REV))


def _band_attn_kernel(q_ref, k_ref, vt_ref, z_ref, bias_ref, o_ref, kp_ref, vtp_ref, s0_ref, s1_ref,
                      mx0_ref, mx1_ref, *, nq):
    hd = BAND_HEAD_DIM
    for hh in range(BAND_HPB):
        kp_ref[hh, :BAND_PAST, :] = jnp.zeros((BAND_PAST, hd), BF16)
        kp_ref[hh, BAND_PAST:, :] = k_ref[:, hh * hd:(hh + 1) * hd]
        vtp_ref[hh, :hd, :BAND_PAST] = jnp.zeros((hd, BAND_PAST), BF16)
        vtp_ref[hh, :hd, BAND_PAST:] = vt_ref[hh * hd:(hh + 1) * hd, :]
        vtp_ref[hh, hd:, :] = jnp.ones((ONES_ROWS, vtp_ref.shape[2]), BF16)

    def scores(j, buf):
        s_ref, mx_ref = buf
        start = pl.multiple_of(j * BAND_TQ, BAND_TQ)
        q_t = q_ref[pl.ds(start, BAND_TQ), :].T
        for hh in range(BAND_HPB):
            s_t = jnp.dot(kp_ref[hh, pl.ds(start, BAND_WIN), :], q_t[hh * hd:(hh + 1) * hd, :],
                          preferred_element_type=F32) + bias_ref[hh]
            s_ref[hh] = s_t
            mx_ref[hh] = jnp.max(s_t, axis=0, keepdims=True)

    def update(j, buf, masked):
        s_ref, mx_ref = buf
        start = pl.multiple_of(j * BAND_TQ, BAND_TQ)
        outs = []
        for hh in range(BAND_HPB):
            s_t = s_ref[hh]
            s_max = mx_ref[hh]
            if masked:
                kpos = start - BAND_PAST + lax.broadcasted_iota(jnp.int32, (BAND_WIN, 1), 0)
                s_t = jnp.where(kpos >= 0, s_t, NEG_INF)
                s_max = jnp.max(s_t, axis=0, keepdims=True)
            p_t = jnp.exp2(s_t - s_max).astype(BF16)
            ol = jnp.dot(vtp_ref[hh, :, pl.ds(start, BAND_WIN)], p_t, preferred_element_type=F32)
            outs.append(ol[:hd] * (1.0 / ol[hd:hd + 1]))
        o = jnp.concatenate(outs, axis=0).T
        z = z_ref[pl.ds(start, BAND_TQ), :].astype(F32)
        o_ref[pl.ds(start, BAND_TQ), :] = (o * _silu(z)).astype(o_ref.dtype)

    buf0, buf1 = (s0_ref, mx0_ref), (s1_ref, mx1_ref)
    scores(0, buf0)
    scores(1, buf1)
    update(0, buf0, True)
    scores(2, buf0)
    update(1, buf1, True)

    def pairs(j, n):
        for _ in range(n):
            scores(j + 1, buf1)
            update(j, buf0, False)
            scores(j + 2, buf0)
            update(j + 1, buf1, False)
            j = j + 2

    def quad(jj, carry):
        pairs(2 + 4 * jj, 2)
        return carry

    n_pairs = (nq - 4) // 2
    lax.fori_loop(0, n_pairs // 2, quad, 0)
    pairs(2 + 4 * (n_pairs // 2), n_pairs % 2)
    scores(nq - 1, buf1)
    update(nq - 2, buf0, False)
    update(nq - 1, buf1, False)


def _band_attention(qkz, v_t, bias_t, batch, seq):
    t = qkz.shape[0]
    nq = seq // BAND_TQ
    assert nq >= 4 and nq % 2 == 0
    hp = BAND_HEADS // BAND_HPB
    return pl.pallas_call(
        functools.partial(_band_attn_kernel, nq=nq),
        name="band_attn",
        out_shape=jax.ShapeDtypeStruct((t, BAND_HEADS * BAND_HEAD_DIM), BF16),
        grid=(batch, hp),
        in_specs=[pl.BlockSpec((seq, LANES), lambda b, g: (b, g)),
                  pl.BlockSpec((seq, LANES), lambda b, g: (b, hp + g)),
                  pl.BlockSpec((LANES, seq), lambda b, g: (g, b)),
                  pl.BlockSpec((seq, LANES), lambda b, g: (b, 2 * hp + g)),
                  pl.BlockSpec((BAND_HPB, BAND_WIN, BAND_TQ), lambda b, g: (g, 0, 0))],
        out_specs=pl.BlockSpec((seq, LANES), lambda b, g: (b, g)),
        scratch_shapes=[pltpu.VMEM((BAND_HPB, seq + BAND_PAST, BAND_HEAD_DIM), BF16),
                        pltpu.VMEM((BAND_HPB, BAND_HEAD_DIM + ONES_ROWS, seq + BAND_PAST), BF16),
                        pltpu.VMEM((BAND_HPB, BAND_WIN, BAND_TQ), F32),
                        pltpu.VMEM((BAND_HPB, BAND_WIN, BAND_TQ), F32),
                        pltpu.VMEM((BAND_HPB, 1, BAND_TQ), F32),
                        pltpu.VMEM((BAND_HPB, 1, BAND_TQ), F32)],
        compiler_params=_params("parallel", "parallel"),
    )(qkz, qkz, v_t, qkz, bias_t)


def _pool_kernel(u_ref, up_ref, z_ref, w_ref, ps_ref, o_ref, *, tm, nb):
    i = pl.program_id(0)
    t0 = (i % nb) * tm
    has_history = t0 > 0
    t = t0 + lax.broadcasted_iota(jnp.int32, (tm, 1), 0)
    for g, win in enumerate(POOL_WINDOWS):
        sl = slice(g * POOL_GROUP, (g + 1) * POOL_GROUP)
        u = u_ref[:, sl]
        acc = jnp.concatenate([jnp.where(has_history, up_ref[:, sl], 0.0), u], axis=0)
        d = 1
        while d < win:
            acc = acc + pltpu.roll(acc, d, 0)
            d *= 2
        wsum = acc[POOL_HALO:, :]
        inv_cnt = 1.0 / jnp.minimum(t + 1, win).astype(F32)
        delta = (wsum * inv_cnt - u).astype(BF16)
        y = jnp.dot(delta, w_ref[g], preferred_element_type=F32) * ps_ref[:, sl]
        o_ref[:, sl] = (y * _silu(z_ref[:, sl])).astype(o_ref.dtype)


def _pool_mix(uz, seq, pool_w, pool_scale, tm=1024):
    t = uz.shape[0]
    d = D_MODEL
    tm = min(tm, seq)
    nb = seq // tm
    hb = tm // POOL_HALO
    return pl.pallas_call(
        functools.partial(_pool_kernel, tm=tm, nb=nb),
        name="pool_mix",
        out_shape=jax.ShapeDtypeStruct((t, d), BF16),
        grid=(t // tm,),
        in_specs=[pl.BlockSpec((tm, d), lambda i: (i, 0)),
                  pl.BlockSpec((POOL_HALO, d), lambda i: (jnp.maximum(i * hb - 1, 0), 0)),
                  pl.BlockSpec((tm, d), lambda i: (i, 1)),
                  pl.BlockSpec((len(POOL_WINDOWS), POOL_GROUP, POOL_GROUP), lambda i: (0, 0, 0)),
                  pl.BlockSpec((1, d), lambda i: (0, 0))],
        out_specs=pl.BlockSpec((tm, d), lambda i: (i, 0)),
        compiler_params=_params("parallel"),
    )(uz, uz, uz, pool_w, pool_scale.reshape(1, d).astype(F32))


QKZ_TN = 2048
POOL_TN = 2048


def _transpose_cast_kernel(x_ref, o_ref):
    o_ref[...] = x_ref[...].T.astype(o_ref.dtype)


def _bf16_weights(w_in, branch, tile=512):
    d = w_in.shape[0]
    first = 2 * branch // tile
    w = w_in.astype(BF16)
    w_vt = pl.pallas_call(
        _transpose_cast_kernel,
        name="transpose_cast",
        out_shape=jax.ShapeDtypeStruct((branch, d), BF16),
        grid=(d // tile, branch // tile),
        in_specs=[pl.BlockSpec((tile, tile), lambda i, j: (i, first + j))],
        out_specs=pl.BlockSpec((tile, tile), lambda i, j: (j, i)),
        compiler_params=_params("parallel", "parallel"),
    )(w)
    return w, w_vt


def _qkz_scale(qk_dim, branch):
    return jnp.concatenate([jnp.full((branch,), qk_dim ** -0.5 * LOG2E, F32), jnp.ones((2 * branch,), F32)])


def kernel(x, c, ada_w, ada_b, norm_pre, norm_post, diff_w_in, diff_w_out, diff_lambda_q1, diff_lambda_k1, diff_lambda_q2, diff_lambda_k2, diff_subln, band_w_in, band_w_out, band_rel_bias, fox_w_in, fox_w_out, fox_forget_bias, pool_w_in, pool_w_out, pool_group_w, pool_scale):
    batch, seq, d = x.shape
    depth = ada_w.shape[0]
    branch = d
    nblk = branch // QKZ_TN
    qkz_blocks = list(range(2 * nblk)) + list(range(3 * nblk, 4 * nblk))
    mod = _ada_mod(c, ada_w, ada_b)
    x2 = x.reshape(batch * seq, d)
    for i in range(depth):
        mod3 = mod[i].reshape(mod.shape[1], 1, 3 * d)
        kind = i % 4
        if kind == 0:
            lambda_init = 0.8 - 0.6 * math.exp(-0.3 * i)
            w, w_vt = _bf16_weights(diff_w_in, branch)
            qkz, h = _norm_proj(x2, seq, norm_pre[i], mod3, w, qkz_blocks,
                                _qkz_scale(DIFF_QK_DIM, branch), emit_h=True, tn=QKZ_TN)
            v_t = _proj_t(w_vt, h)
            o = _diff_attention(qkz, v_t, batch, seq, diff_lambda_q1, diff_lambda_k1, diff_lambda_q2,
                                diff_lambda_k2, diff_subln, lambda_init)
            w_out = diff_w_out
        elif kind == 1:
            w, w_vt = _bf16_weights(band_w_in, branch)
            qkz, h = _norm_proj(x2, seq, norm_pre[i], mod3, w, qkz_blocks,
                                _qkz_scale(BAND_HEAD_DIM, branch), emit_h=True, tn=QKZ_TN)
            v_t = _proj_t(w_vt, h)
            o = _band_attention(qkz, v_t, _band_bias(band_rel_bias), batch, seq)
            w_out = band_w_out
        elif kind == 2:
            w, w_vt = _bf16_weights(fox_w_in, branch)
            wf = jnp.pad(w[:, 4 * branch:], ((0, 0), (0, LANES - FOX_HEADS)))
            qkz, h, f_logit = _norm_proj(x2, seq, norm_pre[i], mod3, w, qkz_blocks,
                                         _qkz_scale(FOX_HEAD_DIM, branch), wf=wf, emit_h=True, tn=QKZ_TN)
            v_t = _proj_t(w_vt, h)
            kx, qx = _fox_extend(_fox_gates(f_logit, fox_forget_bias, batch, seq))
            o = _fox_attention(qkz, v_t, kx, qx, batch, seq)
            w_out = fox_w_out
        else:
            uz, = _norm_proj(x2, seq, norm_pre[i], mod3, pool_w_in.astype(BF16),
                             list(range(2 * branch // POOL_TN)), jnp.ones((2 * branch,), F32),
                             out_dtype=F32, tn=POOL_TN)
            o = _pool_mix(uz, seq, pool_group_w.astype(BF16), pool_scale)
            w_out = pool_w_out
        x2 = _out_proj_residual(o, w_out.astype(BF16), x2, seq, mod3, norm_post[i])
    return x2.reshape(batch, seq, d)
```

```python
import functools
import math

import jax
import jax.numpy as jnp
from jax import lax
from jax.experimental import pallas as pl
from jax.experimental.pallas import tpu as pltpu

F32 = jnp.float32
BF16 = jnp.bfloat16

D_MODEL = 2048
EPS = 1e-6
NEG_INF = -1e30
LOG2E = math.log2(math.e)
CHUNK = 64
DIFF_HEADS, DIFF_V_DIM, DIFF_QK_DIM = 8, 256, 128
BAND_HEADS, BAND_HEAD_DIM, BAND_PAST, REL_CLIP = 32, 64, 512, 256
FOX_HEADS, FOX_HEAD_DIM = 16, 128
POOL_WINDOWS = (2, 4, 8, 16)
POOL_GROUP = D_MODEL // len(POOL_WINDOWS)
POOL_HALO = 16

LANES = 128
VMEM_LIMIT = 58 * 1024 * 1024

NT_DIMS = (((1,), (1,)), ((), ()))


def _silu(x):
    return x / (1.0 + jnp.exp(-x))


def _params(*sem):
    return pltpu.CompilerParams(dimension_semantics=sem, vmem_limit_bytes=VMEM_LIMIT)


def _ada_kernel(c_ref, w_ref, b_ref, o_ref):
    ca = _silu(c_ref[...]).astype(BF16)
    o_ref[0] = jnp.dot(ca, w_ref[0].astype(BF16), preferred_element_type=F32) + b_ref[0]


def _ada_mod(c, ada_w, ada_b):
    b, d = c.shape
    bp = -(-b // 8) * 8
    depth, _, n3 = ada_w.shape
    tn = 768
    cp = jnp.pad(c, ((0, bp - b), (0, 0)))
    return pl.pallas_call(
        _ada_kernel,
        name="ada_mod",
        out_shape=jax.ShapeDtypeStruct((depth, bp, n3), F32),
        grid=(depth, n3 // tn),
        in_specs=[pl.BlockSpec((bp, d), lambda l, j: (0, 0)),
                  pl.BlockSpec((1, d, tn), lambda l, j: (l, 0, j)),
                  pl.BlockSpec((1, 1, tn), lambda l, j: (l, 0, j))],
        out_specs=pl.BlockSpec((1, bp, tn), lambda l, j: (l, 0, j)),
        compiler_params=_params("parallel", "parallel"),
    )(cp, ada_w, ada_b.reshape(depth, 1, n3))


def _norm_proj_kernel(x_ref, g_ref, sh_ref, sc_ref, w_ref, cs_ref, *rest, has_f, emit_h, sub):
    rest = list(rest)
    wf_ref = rest.pop(0) if has_f else None
    o_ref = rest.pop(0)
    hout_ref = rest.pop(0) if emit_h else None
    f_ref = rest.pop(0) if has_f else None
    h_ref = rest.pop(0)

    @pl.when(pl.program_id(1) == 0)
    def _():
        g = g_ref[...]
        sc = 1.0 + sc_ref[0]
        sh = sh_ref[0]
        for r in range(x_ref.shape[0] // sub):
            rows = slice(r * sub, (r + 1) * sub)
            x = x_ref[rows, :]
            y = x * lax.rsqrt(jnp.mean(x * x, axis=-1, keepdims=True) + EPS) * g
            h = (y * sc + sh).astype(BF16)
            h_ref[rows, :] = h
            if emit_h:
                hout_ref[rows, :] = h
            y = jnp.dot(h, w_ref[...], preferred_element_type=F32)
            o_ref[rows, :] = (y * cs_ref[...]).astype(o_ref.dtype)
        if has_f:
            f_ref[...] = jnp.dot(h_ref[...], wf_ref[...], preferred_element_type=F32)

    @pl.when(pl.program_id(1) > 0)
    def _():
        y = jnp.dot(h_ref[...], w_ref[...], preferred_element_type=F32)
        o_ref[...] = (y * cs_ref[...]).astype(o_ref.dtype)


def _norm_proj(x2, seq, g, mod3, w, col_blocks, col_scale, wf=None, emit_h=False, out_dtype=BF16,
               tm=1024, tn=1024):
    t, d = x2.shape
    tm = min(tm, seq)
    nb = seq // tm
    nj = len(col_blocks)
    n = nj * tn
    first, skip_from, skip = col_blocks[0], None, 0
    for a, b2 in zip(col_blocks, col_blocks[1:]):
        if b2 != a + 1:
            skip_from, skip = a - first + 1, b2 - a - 1
    if skip_from is None:
        wmap = lambda i, j: (0, j + first)
    else:
        wmap = lambda i, j: (0, j + first + jnp.where(j >= skip_from, skip, 0))
    in_specs = [pl.BlockSpec((tm, d), lambda i, j: (i, 0)),
                pl.BlockSpec((1, d), lambda i, j: (0, 0)),
                pl.BlockSpec((1, 1, d), lambda i, j: (i // nb, 0, 0)),
                pl.BlockSpec((1, 1, d), lambda i, j: (i // nb, 0, 1)),
                pl.BlockSpec((d, tn), wmap),
                pl.BlockSpec((1, tn), lambda i, j: (0, j))]
    out_shape = [jax.ShapeDtypeStruct((t, n), out_dtype)]
    out_specs = [pl.BlockSpec((tm, tn), lambda i, j: (i, j))]
    args = [x2, g.reshape(1, d), mod3, mod3, w, col_scale.reshape(1, n).astype(F32)]
    if wf is not None:
        in_specs.append(pl.BlockSpec((d, LANES), lambda i, j: (0, 0)))
        args.append(wf)
    if emit_h:
        out_shape.append(jax.ShapeDtypeStruct((t, d), BF16))
        out_specs.append(pl.BlockSpec((tm, d), lambda i, j: (i, 0)))
    if wf is not None:
        out_shape.append(jax.ShapeDtypeStruct((t, LANES), F32))
        out_specs.append(pl.BlockSpec((tm, LANES), lambda i, j: (i, 0)))
    return pl.pallas_call(
        functools.partial(_norm_proj_kernel, has_f=wf is not None, emit_h=emit_h, sub=min(256, tm)),
        name="norm_proj",
        out_shape=out_shape,
        grid=(t // tm, nj),
        in_specs=in_specs,
        out_specs=out_specs,
        scratch_shapes=[pltpu.VMEM((tm, d), BF16)],
        compiler_params=_params("parallel", "arbitrary"),
    )(*args)


def _proj_t_kernel(w_ref, h_ref, o_ref):
    o_ref[...] = lax.dot_general(w_ref[...], h_ref[...], NT_DIMS,
                                 preferred_element_type=F32).astype(o_ref.dtype)


def _proj_t(w_t, h, tn=2048, tm=2048):
    n, d = w_t.shape
    t = h.shape[0]
    tm = min(tm, t)
    return pl.pallas_call(
        _proj_t_kernel,
        name="proj_t",
        out_shape=jax.ShapeDtypeStruct((n, t), BF16),
        grid=(t // tm, n // tn),
        in_specs=[pl.BlockSpec((tn, d), lambda i, j: (j, 0)),
                  pl.BlockSpec((tm, d), lambda i, j: (i, 0))],
        out_specs=pl.BlockSpec((tn, tm), lambda i, j: (j, i)),
        compiler_params=_params("parallel", "arbitrary"),
    )(w_t, h)


def _out_proj_kernel(o_ref, w_ref, x_ref, gate_ref, g_ref, out_ref, *, sub):
    for r in range(o_ref.shape[0] // sub):
        rows = slice(r * sub, (r + 1) * sub)
        y = jnp.dot(o_ref[rows, :], w_ref[...], preferred_element_type=F32)
        yn = y * lax.rsqrt(jnp.mean(y * y, axis=-1, keepdims=True) + EPS) * g_ref[...]
        out_ref[rows, :] = x_ref[rows, :] + gate_ref[0] * yn


def _out_proj_residual(o2, w, x2, seq, mod3, g, tm=1024):
    t, d = x2.shape
    tm = min(tm, seq)
    nb = seq // tm
    return pl.pallas_call(
        functools.partial(_out_proj_kernel, sub=min(256, tm)),
        name="out_proj",
        out_shape=jax.ShapeDtypeStruct((t, d), F32),
        grid=(t // tm,),
        in_specs=[pl.BlockSpec((tm, d), lambda i: (i, 0)),
                  pl.BlockSpec((d, d), lambda i: (0, 0), pipeline_mode=pl.Buffered(1)),
                  pl.BlockSpec((tm, d), lambda i: (i, 0)),
                  pl.BlockSpec((1, 1, d), lambda i: (i // nb, 0, 2)),
                  pl.BlockSpec((1, d), lambda i: (0, 0))],
        out_specs=pl.BlockSpec((tm, d), lambda i: (i, 0)),
        compiler_params=_params("parallel"),
    )(o2, w, x2, mod3, g.reshape(1, d))


def _flash_step_t(s_t, s_max, v_t, m_ref, l_ref, acc_ref, idx, cols):
    m_prev = m_ref[idx, :, cols]
    m_new = jnp.maximum(m_prev, s_max)
    alpha = jnp.exp2(m_prev - m_new)
    p_t = jnp.exp2(s_t - m_new)
    if l_ref is not None:
        l_ref[idx, :, cols] = alpha * l_ref[idx, :, cols] + jnp.sum(p_t, axis=0, keepdims=True)
    acc_ref[idx, :, cols] = (alpha * acc_ref[idx, :, cols]
                             + jnp.dot(v_t, p_t.astype(BF16), preferred_element_type=F32))
    m_ref[idx, :, cols] = m_new


def _flash_init(m_ref, l_ref, acc_ref):
    m_ref[...] = jnp.full(m_ref.shape, NEG_INF, F32)
    if l_ref is not None:
        l_ref[...] = jnp.zeros(l_ref.shape, F32)
    acc_ref[...] = jnp.zeros(acc_ref.shape, F32)


ONES_ROWS = 16


ATTN_TK = 512


def _mask_diagonal(s_ref, mx_ref, c0, granule):
    t = LANES
    kg = lax.broadcasted_iota(jnp.int32, (t, t), 0) // granule
    qg = lax.broadcasted_iota(jnp.int32, (t, t), 1) // granule
    visible = kg <= qg
    for r in range(ATTN_TK // t):
        rows = slice(r * t, (r + 1) * t)
        if r:
            s_ref[rows, c0:c0 + r * t] = jnp.full((t, r * t), NEG_INF, F32)
        diag = slice(c0 + r * t, c0 + (r + 1) * t)
        s_ref[rows, diag] = jnp.where(visible, s_ref[rows, diag], NEG_INF)
    cols = slice(c0, c0 + ATTN_TK)
    mx_ref[:, cols] = jnp.max(s_ref[:, cols], axis=0, keepdims=True)


def _causal_blocks(nq, begin, scores, mask, update, finish, buf0, buf1):
    every, late = slice(None), slice(ATTN_TK, 2 * ATTN_TK)
    begin(0)
    scores(0, buf0, every)

    def pairs(j, n):
        for _ in range(n):
            scores(j + 1, buf1, every)
            update(j, buf0, every)
            scores(j + 2, buf0, every)
            update(j + 1, buf1, every)
            j = j + 2

    def qblock(qi, carry):
        def quad(jj, c):
            pairs(4 * jj, 2)
            return c

        lax.fori_loop(0, qi // 2, quad, 0)

        @pl.when(qi % 2 == 1)
        def _():
            pairs(2 * qi - 2, 1)

        scores(2 * qi + 1, buf1, late)
        mask(buf0, 0)
        update(2 * qi, buf0, every)
        mask(buf1, ATTN_TK)
        update(2 * qi + 1, buf1, late)
        finish(qi)
        begin(jnp.minimum(qi + 1, nq - 1))
        scores(0, buf0, every)
        return carry

    lax.fori_loop(0, nq, qblock, 0)


def _diff_attn_kernel(q_ref, k_ref, vt_ref, z_ref, lq1_ref, lk1_ref, lq2_ref, lk2_ref, g_ref,
                      o_ref, m_ref, l_ref, acc_ref, s0_ref, s1_ref, mx0_ref, mx1_ref, qc_ref,
                      *, nq, lambda_init):
    tq = 2 * ATTN_TK
    lam = (jnp.exp(jnp.sum(lq1_ref[...] * lk1_ref[...], axis=-1, keepdims=True))
           - jnp.exp(jnp.sum(lq2_ref[...] * lk2_ref[...], axis=-1, keepdims=True)) + lambda_init)
    g = g_ref[...] * (1.0 - lambda_init)

    def begin(qi):
        qc_ref[...] = q_ref[pl.ds(pl.multiple_of(qi * tq, tq), tq), :].T
        _flash_init(m_ref, l_ref, acc_ref)

    def finish(qi):
        rows = pl.ds(pl.multiple_of(qi * tq, tq), tq)
        o_t = acc_ref[0] * (1.0 / l_ref[0]) - acc_ref[1] * (lam / l_ref[1])
        o = o_t.T
        on = o * lax.rsqrt(jnp.mean(o * o, axis=-1, keepdims=True) + EPS) * g
        o_ref[rows, :] = (on * _silu(z_ref[rows, :].astype(F32))).astype(o_ref.dtype)

    def scores(ki, buf, cols):
        s_ref, mx_ref = buf
        k = k_ref[pl.ds(pl.multiple_of(ki * ATTN_TK, ATTN_TK), ATTN_TK), :]
        for m in range(2):
            sl = slice(m * DIFF_QK_DIM, (m + 1) * DIFF_QK_DIM)
            s_t = jnp.dot(k[:, sl], qc_ref[sl, cols], preferred_element_type=F32)
            s_ref[m, :, cols] = s_t
            mx_ref[m, :, cols] = jnp.max(s_t, axis=0, keepdims=True)

    def mask(buf, c0):
        s_ref, mx_ref = buf
        for m in range(2):
            _mask_diagonal(s_ref.at[m], mx_ref.at[m], c0, CHUNK)

    def update(ki, buf, cols):
        s_ref, mx_ref = buf
        v_t = vt_ref[:, pl.ds(pl.multiple_of(ki * ATTN_TK, ATTN_TK), ATTN_TK)]
        for m in range(2):
            _flash_step_t(s_ref[m, :, cols], mx_ref[m, :, cols], v_t, m_ref, l_ref, acc_ref, m, cols)

    _causal_blocks(nq, begin, scores, mask, update, finish, (s0_ref, mx0_ref), (s1_ref, mx1_ref))


def _diff_attention(qkz, v_t, batch, seq, lq1, lk1, lq2, lk2, subln_g, lambda_init):
    t = qkz.shape[0]
    tq = 2 * ATTN_TK
    nq = seq // tq
    h, dv = DIFF_HEADS, DIFF_V_DIM
    vec = lambda a: a.reshape(1, -1).astype(F32)
    small = lambda n: pl.BlockSpec((1, n), lambda b, hh: (0, 0))
    return pl.pallas_call(
        functools.partial(_diff_attn_kernel, nq=nq, lambda_init=lambda_init),
        name="diff_attn",
        out_shape=jax.ShapeDtypeStruct((t, h * dv), BF16),
        grid=(batch, h),
        in_specs=[pl.BlockSpec((seq, dv), lambda b, hh: (b, hh)),
                  pl.BlockSpec((seq, dv), lambda b, hh: (b, h + hh)),
                  pl.BlockSpec((dv, seq), lambda b, hh: (hh, b)),
                  pl.BlockSpec((seq, dv), lambda b, hh: (b, 2 * h + hh)),
                  small(DIFF_QK_DIM), small(DIFF_QK_DIM), small(DIFF_QK_DIM), small(DIFF_QK_DIM),
                  small(dv)],
        out_specs=pl.BlockSpec((seq, dv), lambda b, hh: (b, hh)),
        scratch_shapes=[pltpu.VMEM((2, 1, tq), F32), pltpu.VMEM((2, 1, tq), F32),
                        pltpu.VMEM((2, dv, tq), F32),
                        pltpu.VMEM((2, ATTN_TK, tq), F32), pltpu.VMEM((2, ATTN_TK, tq), F32),
                        pltpu.VMEM((2, 1, tq), F32), pltpu.VMEM((2, 1, tq), F32),
                        pltpu.VMEM((dv, tq), BF16)],
        compiler_params=_params("parallel", "parallel"),
    )(qkz, qkz, v_t, qkz, vec(lq1), vec(lk1), vec(lq2), vec(lk2), vec(subln_g))


FOX_PIECES = 3


def _fox_gate_kernel(f_ref, b_ref, o_ref):
    x = f_ref[...] + b_ref[...]
    lf = jnp.minimum(x, 0.0) - jnp.log1p(jnp.exp(-jnp.abs(x)))
    n = lf.shape[0]
    row = lax.broadcasted_iota(jnp.int32, lf.shape, 0)
    d = 1
    while d < n:
        lf = lf + jnp.where(row >= d, pltpu.roll(lf, d, 0), 0.0)
        d *= 2
    rest = lf * LOG2E
    lane = lax.broadcasted_iota(jnp.int32, lf.shape, 1)
    out = jnp.zeros(lf.shape, F32)
    for p in range(FOX_PIECES):
        piece = rest.astype(BF16).astype(F32)
        rest = rest - piece
        moved = piece if p == 0 else pltpu.roll(piece, p * FOX_HEADS, 1)
        out = jnp.where((lane >= p * FOX_HEADS) & (lane < (p + 1) * FOX_HEADS), moved, out)
    o_ref[...] = out.astype(BF16)


def _fox_gates(f_logit, bias, batch, seq):
    bpad = jnp.zeros((1, LANES), F32).at[0, :FOX_HEADS].set(bias.astype(F32))
    return pl.pallas_call(
        _fox_gate_kernel,
        name="fox_gates",
        out_shape=jax.ShapeDtypeStruct((batch * seq, LANES), BF16),
        grid=(batch,),
        in_specs=[pl.BlockSpec((seq, LANES), lambda b: (b, 0)),
                  pl.BlockSpec((1, LANES), lambda b: (0, 0))],
        out_specs=pl.BlockSpec((seq, LANES), lambda b: (b, 0)),
        compiler_params=_params("parallel"),
    )(f_logit, bpad)


def _fox_place_kernel(f_ref, pk_ref, pq_ref, ck_ref, cq_ref, kx_ref, qx_ref):
    f = f_ref[...]
    kx_ref[...] = (jnp.dot(f, pk_ref[...], preferred_element_type=F32) + ck_ref[...]).astype(BF16)
    qx_ref[...] = (jnp.dot(f, pq_ref[...], preferred_element_type=F32) + cq_ref[...]).astype(BF16)


def _fox_extend(pieces, tm=2048):
    t = pieces.shape[0]
    tm = min(tm, t)
    n = FOX_HEADS * LANES
    src = jnp.arange(LANES)[:, None]
    dst = jnp.arange(n)[None, :]
    p_src, h_src = src // FOX_HEADS, src % FOX_HEADS
    h_dst, c_dst = dst // LANES, dst % LANES
    live = (p_src < FOX_PIECES) & (h_src == h_dst)
    pk = jnp.where(live & (c_dst == p_src), -1.0, 0.0).astype(BF16)
    pq = jnp.where(live & (c_dst == p_src + FOX_PIECES), 1.0, 0.0).astype(BF16)
    ck = ((c_dst >= FOX_PIECES) & (c_dst < 2 * FOX_PIECES)).astype(F32)
    cq = (c_dst < FOX_PIECES).astype(F32)
    mat = pl.BlockSpec((LANES, n), lambda i: (0, 0))
    vec = pl.BlockSpec((1, n), lambda i: (0, 0))
    return pl.pallas_call(
        _fox_place_kernel,
        name="fox_extend",
        out_shape=[jax.ShapeDtypeStruct((t, n), BF16), jax.ShapeDtypeStruct((t, n), BF16)],
        grid=(t // tm,),
        in_specs=[pl.BlockSpec((tm, LANES), lambda i: (i, 0)), mat, mat, vec, vec],
        out_specs=[pl.BlockSpec((tm, n), lambda i: (i, 0)), pl.BlockSpec((tm, n), lambda i: (i, 0))],
        compiler_params=_params("parallel"),
    )(pieces, pk, pq, ck, cq)


def _fox_attn_kernel(q_ref, qx_ref, k_ref, kx_ref, vt_ref, z_ref, o_ref, m_ref, acc_ref,
                     s0_ref, s1_ref, mx0_ref, mx1_ref, qt_ref, vte_ref, *, nq):
    tq = 2 * ATTN_TK
    dh = FOX_HEAD_DIM
    vte_ref[:dh, :] = vt_ref[...]
    vte_ref[dh:, :] = jnp.ones((ONES_ROWS, vte_ref.shape[1]), BF16)

    def begin(qi):
        rows = pl.ds(pl.multiple_of(qi * tq, tq), tq)
        qt_ref[:dh, :] = q_ref[rows, :].T
        qt_ref[dh:, :] = qx_ref[rows, :].T
        _flash_init(m_ref, None, acc_ref)

    def finish(qi):
        rows = pl.ds(pl.multiple_of(qi * tq, tq), tq)
        o = (acc_ref[0, :dh, :] * (1.0 / acc_ref[0, dh:dh + 1, :])).T
        o_ref[rows, :] = (o * _silu(z_ref[rows, :].astype(F32))).astype(o_ref.dtype)

    def scores(ki, buf, cols):
        s_ref, mx_ref = buf
        off = pl.multiple_of(ki * ATTN_TK, ATTN_TK)
        k = jnp.concatenate([k_ref[pl.ds(off, ATTN_TK), :], kx_ref[pl.ds(off, ATTN_TK), :]], axis=1)
        s_t = jnp.dot(k, qt_ref[:, cols], preferred_element_type=F32)
        s_ref[:, cols] = s_t
        mx_ref[:, cols] = jnp.max(s_t, axis=0, keepdims=True)

    def mask(buf, c0):
        _mask_diagonal(buf[0], buf[1], c0, 1)

    def update(ki, buf, cols):
        s_ref, mx_ref = buf
        v_t = vte_ref[:, pl.ds(pl.multiple_of(ki * ATTN_TK, ATTN_TK), ATTN_TK)]
        _flash_step_t(s_ref[:, cols], mx_ref[:, cols], v_t, m_ref, None, acc_ref, 0, cols)

    _causal_blocks(nq, begin, scores, mask, update, finish, (s0_ref, mx0_ref), (s1_ref, mx1_ref))


def _fox_attention(qkz, v_t, kx, qx, batch, seq):
    t = qkz.shape[0]
    tq = 2 * ATTN_TK
    nq = seq // tq
    h, dh = FOX_HEADS, FOX_HEAD_DIM
    return pl.pallas_call(
        functools.partial(_fox_attn_kernel, nq=nq),
        name="fox_attn",
        out_shape=jax.ShapeDtypeStruct((t, h * dh), BF16),
        grid=(batch, h),
        in_specs=[pl.BlockSpec((seq, dh), lambda b, hh: (b, hh)),
                  pl.BlockSpec((seq, LANES), lambda b, hh: (b, hh)),
                  pl.BlockSpec((seq, dh), lambda b, hh: (b, h + hh)),
                  pl.BlockSpec((seq, LANES), lambda b, hh: (b, hh)),
                  pl.BlockSpec((dh, seq), lambda b, hh: (hh, b)),
                  pl.BlockSpec((seq, dh), lambda b, hh: (b, 2 * h + hh))],
        out_specs=pl.BlockSpec((seq, dh), lambda b, hh: (b, hh)),
        scratch_shapes=[pltpu.VMEM((1, 1, tq), F32),
                        pltpu.VMEM((1, dh + ONES_ROWS, tq), F32),
                        pltpu.VMEM((ATTN_TK, tq), F32), pltpu.VMEM((ATTN_TK, tq), F32),
                        pltpu.VMEM((1, tq), F32), pltpu.VMEM((1, tq), F32),
                        pltpu.VMEM((dh + LANES, tq), BF16),
                        pltpu.VMEM((dh + ONES_ROWS, seq), BF16)],
        compiler_params=_params("parallel", "parallel"),
    )(qkz, qx, qkz, kx, v_t, qkz)


BAND_TQ = 256
BAND_WIN = BAND_PAST + BAND_TQ
BAND_HPB = LANES // BAND_HEAD_DIM
BAND_REV = 1024


def _band_bias_kernel(row_ref, o_ref):
    x = jnp.broadcast_to(row_ref[0], (BAND_WIN, BAND_REV))
    x = pltpu.roll(x, BAND_REV - BAND_WIN, 1, stride=1, stride_axis=0)
    x = x[:, :BAND_TQ]
    kc = lax.broadcasted_iota(jnp.int32, x.shape, 0) // CHUNK
    qc = lax.broadcasted_iota(jnp.int32, x.shape, 1) // CHUNK
    band = (kc >= qc) & (kc <= qc + BAND_PAST // CHUNK)
    o_ref[0] = jnp.where(band, x * LOG2E, NEG_INF)


def _band_bias(rel_table):
    nh = rel_table.shape[0]
    t = rel_table.astype(F32)
    row = jnp.concatenate([t, jnp.broadcast_to(t[:, 2 * REL_CLIP:], (nh, BAND_REV - t.shape[1]))], axis=1)
    return pl.pallas_call(
        _band_bias_kernel,
        name="band_bias",
        out_shape=jax.ShapeDtypeStruct((nh, BAND_WIN, BAND_TQ), F32),
        grid=(nh,),
        in_specs=[pl.BlockSpec((1, 1, BAND_REV), lambda h: (h, 0, 0))],
        out_specs=pl.BlockSpec((1, BAND_WIN, BAND_TQ), lambda h: (h, 0, 0)),
        compiler_params=_params("parallel"),
    )(row.reshape(nh, 1, BAND_REV))


def _band_attn_kernel(q_ref, k_ref, vt_ref, z_ref, bias_ref, o_ref, kp_ref, vtp_ref, s0_ref, s1_ref,
                      mx0_ref, mx1_ref, *, nq):
    hd = BAND_HEAD_DIM
    for hh in range(BAND_HPB):
        kp_ref[hh, :BAND_PAST, :] = jnp.zeros((BAND_PAST, hd), BF16)
        kp_ref[hh, BAND_PAST:, :] = k_ref[:, hh * hd:(hh + 1) * hd]
        vtp_ref[hh, :hd, :BAND_PAST] = jnp.zeros((hd, BAND_PAST), BF16)
        vtp_ref[hh, :hd, BAND_PAST:] = vt_ref[hh * hd:(hh + 1) * hd, :]
        vtp_ref[hh, hd:, :] = jnp.ones((ONES_ROWS, vtp_ref.shape[2]), BF16)

    def scores(j, buf):
        s_ref, mx_ref = buf
        start = pl.multiple_of(j * BAND_TQ, BAND_TQ)
        q_t = q_ref[pl.ds(start, BAND_TQ), :].T
        for hh in range(BAND_HPB):
            s_t = jnp.dot(kp_ref[hh, pl.ds(start, BAND_WIN), :], q_t[hh * hd:(hh + 1) * hd, :],
                          preferred_element_type=F32) + bias_ref[hh]
            s_ref[hh] = s_t
            mx_ref[hh] = jnp.max(s_t, axis=0, keepdims=True)

    def update(j, buf, masked):
        s_ref, mx_ref = buf
        start = pl.multiple_of(j * BAND_TQ, BAND_TQ)
        outs = []
        for hh in range(BAND_HPB):
            s_t = s_ref[hh]
            s_max = mx_ref[hh]
            if masked:
                kpos = start - BAND_PAST + lax.broadcasted_iota(jnp.int32, (BAND_WIN, 1), 0)
                s_t = jnp.where(kpos >= 0, s_t, NEG_INF)
                s_max = jnp.max(s_t, axis=0, keepdims=True)
            p_t = jnp.exp2(s_t - s_max).astype(BF16)
            ol = jnp.dot(vtp_ref[hh, :, pl.ds(start, BAND_WIN)], p_t, preferred_element_type=F32)
            outs.append(ol[:hd] * (1.0 / ol[hd:hd + 1]))
        o = jnp.concatenate(outs, axis=0).T
        z = z_ref[pl.ds(start, BAND_TQ), :].astype(F32)
        o_ref[pl.ds(start, BAND_TQ), :] = (o * _silu(z)).astype(o_ref.dtype)

    buf0, buf1 = (s0_ref, mx0_ref), (s1_ref, mx1_ref)
    scores(0, buf0)
    scores(1, buf1)
    update(0, buf0, True)
    scores(2, buf0)
    update(1, buf1, True)

    def pairs(j, n):
        for _ in range(n):
            scores(j + 1, buf1)
            update(j, buf0, False)
            scores(j + 2, buf0)
            update(j + 1, buf1, False)
            j = j + 2

    def quad(jj, carry):
        pairs(2 + 4 * jj, 2)
        return carry

    n_pairs = (nq - 4) // 2
    lax.fori_loop(0, n_pairs // 2, quad, 0)
    pairs(2 + 4 * (n_pairs // 2), n_pairs % 2)
    scores(nq - 1, buf1)
    update(nq - 2, buf0, False)
    update(nq - 1, buf1, False)


def _band_attention(qkz, v_t, bias_t, batch, seq):
    t = qkz.shape[0]
    nq = seq // BAND_TQ
    assert nq >= 4 and nq % 2 == 0
    hp = BAND_HEADS // BAND_HPB
    return pl.pallas_call(
        functools.partial(_band_attn_kernel, nq=nq),
        name="band_attn",
        out_shape=jax.ShapeDtypeStruct((t, BAND_HEADS * BAND_HEAD_DIM), BF16),
        grid=(batch, hp),
        in_specs=[pl.BlockSpec((seq, LANES), lambda b, g: (b, g)),
                  pl.BlockSpec((seq, LANES), lambda b, g: (b, hp + g)),
                  pl.BlockSpec((LANES, seq), lambda b, g: (g, b)),
                  pl.BlockSpec((seq, LANES), lambda b, g: (b, 2 * hp + g)),
                  pl.BlockSpec((BAND_HPB, BAND_WIN, BAND_TQ), lambda b, g: (g, 0, 0))],
        out_specs=pl.BlockSpec((seq, LANES), lambda b, g: (b, g)),
        scratch_shapes=[pltpu.VMEM((BAND_HPB, seq + BAND_PAST, BAND_HEAD_DIM), BF16),
                        pltpu.VMEM((BAND_HPB, BAND_HEAD_DIM + ONES_ROWS, seq + BAND_PAST), BF16),
                        pltpu.VMEM((BAND_HPB, BAND_WIN, BAND_TQ), F32),
                        pltpu.VMEM((BAND_HPB, BAND_WIN, BAND_TQ), F32),
                        pltpu.VMEM((BAND_HPB, 1, BAND_TQ), F32),
                        pltpu.VMEM((BAND_HPB, 1, BAND_TQ), F32)],
        compiler_params=_params("parallel", "parallel"),
    )(qkz, qkz, v_t, qkz, bias_t)


def _pool_kernel(u_ref, up_ref, z_ref, w_ref, ps_ref, o_ref, *, tm, nb):
    i = pl.program_id(0)
    t0 = (i % nb) * tm
    has_history = t0 > 0
    t = t0 + lax.broadcasted_iota(jnp.int32, (tm, 1), 0)
    for g, win in enumerate(POOL_WINDOWS):
        sl = slice(g * POOL_GROUP, (g + 1) * POOL_GROUP)
        u = u_ref[:, sl]
        acc = jnp.concatenate([jnp.where(has_history, up_ref[:, sl], 0.0), u], axis=0)
        d = 1
        while d < win:
            acc = acc + pltpu.roll(acc, d, 0)
            d *= 2
        wsum = acc[POOL_HALO:, :]
        inv_cnt = 1.0 / jnp.minimum(t + 1, win).astype(F32)
        delta = (wsum * inv_cnt - u).astype(BF16)
        y = jnp.dot(delta, w_ref[g], preferred_element_type=F32) * ps_ref[:, sl]
        o_ref[:, sl] = (y * _silu(z_ref[:, sl])).astype(o_ref.dtype)


def _pool_mix(uz, seq, pool_w, pool_scale, tm=1024):
    t = uz.shape[0]
    d = D_MODEL
    tm = min(tm, seq)
    nb = seq // tm
    hb = tm // POOL_HALO
    return pl.pallas_call(
        functools.partial(_pool_kernel, tm=tm, nb=nb),
        name="pool_mix",
        out_shape=jax.ShapeDtypeStruct((t, d), BF16),
        grid=(t // tm,),
        in_specs=[pl.BlockSpec((tm, d), lambda i: (i, 0)),
                  pl.BlockSpec((POOL_HALO, d), lambda i: (jnp.maximum(i * hb - 1, 0), 0)),
                  pl.BlockSpec((tm, d), lambda i: (i, 1)),
                  pl.BlockSpec((len(POOL_WINDOWS), POOL_GROUP, POOL_GROUP), lambda i: (0, 0, 0)),
                  pl.BlockSpec((1, d), lambda i: (0, 0))],
        out_specs=pl.BlockSpec((tm, d), lambda i: (i, 0)),
        compiler_params=_params("parallel"),
    )(uz, uz, uz, pool_w, pool_scale.reshape(1, d).astype(F32))


QKZ_TN = 2048
POOL_TN = 2048


def _transpose_cast_kernel(x_ref, o_ref):
    o_ref[...] = x_ref[...].T.astype(o_ref.dtype)


def _bf16_weights(w_in, branch, tile=512):
    d = w_in.shape[0]
    first = 2 * branch // tile
    w = w_in.astype(BF16)
    w_vt = pl.pallas_call(
        _transpose_cast_kernel,
        name="transpose_cast",
        out_shape=jax.ShapeDtypeStruct((branch, d), BF16),
        grid=(d // tile, branch // tile),
        in_specs=[pl.BlockSpec((tile, tile), lambda i, j: (i, first + j))],
        out_specs=pl.BlockSpec((tile, tile), lambda i, j: (j, i)),
        compiler_params=_params("parallel", "parallel"),
    )(w)
    return w, w_vt


def _qkz_scale(qk_dim, branch):
    return jnp.concatenate([jnp.full((branch,), qk_dim ** -0.5 * LOG2E, F32), jnp.ones((2 * branch,), F32)])


def kernel(x, c, ada_w, ada_b, norm_pre, norm_post, diff_w_in, diff_w_out, diff_lambda_q1, diff_lambda_k1, diff_lambda_q2, diff_lambda_k2, diff_subln, band_w_in, band_w_out, band_rel_bias, fox_w_in, fox_w_out, fox_forget_bias, pool_w_in, pool_w_out, pool_group_w, pool_scale):
    batch, seq, d = x.shape
    depth = ada_w.shape[0]
    branch = d
    nblk = branch // QKZ_TN
    qkz_blocks = list(range(2 * nblk)) + list(range(3 * nblk, 4 * nblk))
    mod = _ada_mod(c, ada_w, ada_b)
    x2 = x.reshape(batch * seq, d)
    for i in range(depth):
        mod3 = mod[i].reshape(mod.shape[1], 1, 3 * d)
        kind = i % 4
        if kind == 0:
            lambda_init = 0.8 - 0.6 * math.exp(-0.3 * i)
            w, w_vt = _bf16_weights(diff_w_in, branch)
            qkz, h = _norm_proj(x2, seq, norm_pre[i], mod3, w, qkz_blocks,
                                _qkz_scale(DIFF_QK_DIM, branch), emit_h=True, tn=QKZ_TN)
            v_t = _proj_t(w_vt, h)
            o = _diff_attention(qkz, v_t, batch, seq, diff_lambda_q1, diff_lambda_k1, diff_lambda_q2,
                                diff_lambda_k2, diff_subln, lambda_init)
            w_out = diff_w_out
        elif kind == 1:
            w, w_vt = _bf16_weights(band_w_in, branch)
            qkz, h = _norm_proj(x2, seq, norm_pre[i], mod3, w, qkz_blocks,
                                _qkz_scale(BAND_HEAD_DIM, branch), emit_h=True, tn=QKZ_TN)
            v_t = _proj_t(w_vt, h)
            o = _band_attention(qkz, v_t, _band_bias(band_rel_bias), batch, seq)
            w_out = band_w_out
        elif kind == 2:
            w, w_vt = _bf16_weights(fox_w_in, branch)
            wf = jnp.pad(w[:, 4 * branch:], ((0, 0), (0, LANES - FOX_HEADS)))
            qkz, h, f_logit = _norm_proj(x2, seq, norm_pre[i], mod3, w, qkz_blocks,
                                         _qkz_scale(FOX_HEAD_DIM, branch), wf=wf, emit_h=True, tn=QKZ_TN)
            v_t = _proj_t(w_vt, h)
            kx, qx = _fox_extend(_fox_gates(f_logit, fox_forget_bias, batch, seq))
            o = _fox_attention(qkz, v_t, kx, qx, batch, seq)
            w_out = fox_w_out
        else:
            uz, = _norm_proj(x2, seq, norm_pre[i], mod3, pool_w_in.astype(BF16),
                             list(range(2 * branch // POOL_TN)), jnp.ones((2 * branch,), F32),
                             out_dtype=F32, tn=POOL_TN)
            o = _pool_mix(uz, seq, pool_group_w.astype(BF16), pool_scale)
            w_out = pool_w_out
        x2 = _out_proj_residual(o, w_out.astype(BF16), x2, seq, mod3, norm_post[i])
    return x2.reshape(batch, seq, d)
```
